```python
import math
import jax, jax.numpy as jnp
from jax import lax
import numpy as np

D_MODEL = 1024
BATCH = 4
SEQ = 4096
DEPTH = 2

CHUNK = 64
EPS = 1e-6
D_MIX = D_MODEL
CONV_K = 4

SSD_HEADS = 8
SSD_HEAD_DIM = 64
SSD_DIM = SSD_HEADS * SSD_HEAD_DIM
SSD_GROUPS = 2
SSD_STATE = 128
SSD_CONV_DIM = SSD_DIM + 2 * SSD_GROUPS * SSD_STATE

ATT_HEADS = 4
ATT_KV_HEADS = 2
ATT_HEAD_DIM = 64
ATT_DIM = ATT_HEADS * ATT_HEAD_DIM
ATT_KV_DIM = ATT_KV_HEADS * ATT_HEAD_DIM
WINDOW = 128
WIN_CHUNKS = WINDOW // CHUNK

GDN_HEADS = 4
GDN_HEAD_K = 64
GDN_HEAD_V = 64
GDN_KDIM = GDN_HEADS * GDN_HEAD_K
GDN_DIM = GDN_HEADS * GDN_HEAD_V
GDN_CONV_DIM = 2 * GDN_KDIM + GDN_DIM

FF = ((8 * D_MODEL // 3 + 255) // 256) * 256

IN_SIZES = (ATT_DIM, ATT_KV_DIM, ATT_KV_DIM,
            SSD_DIM, SSD_CONV_DIM, SSD_HEADS,
            GDN_CONV_DIM, GDN_DIM, GDN_HEADS, GDN_HEADS)
IN_COLS = sum(IN_SIZES)

kernel_name = "hybrid_ssd_swa_gdn_sandwich_block"


def _offsets(sizes):
    out, acc = [], 0
    for s in sizes[:-1]:
        acc += s
        out.append(acc)
    return out


def rmsnorm(x, w):
    xf = x.astype(jnp.float32)
    y = xf * lax.rsqrt(jnp.mean(xf * xf, axis=-1, keepdims=True) + EPS)
    return (y * w.astype(jnp.float32)).astype(x.dtype)


def l2norm(x):
    return x * lax.rsqrt(jnp.sum(x * x, axis=-1, keepdims=True) + EPS)


def causal_dwconv(x, w, b=None):
    k = w.shape[0]
    t = x.shape[1]
    xp = jnp.pad(x, ((0, 0), (k - 1, 0), (0, 0)))
    y = xp[:, 0:t] * w[0]
    for i in range(1, k):
        y = y + xp[:, i:i + t] * w[i]
    if b is not None:
        y = y + b
    return y


def alibi_slopes(n):
    return 2.0 ** (-8.0 * jnp.arange(1, n + 1, dtype=jnp.float32) / n)


def lower_exp_diff(cs):
    n = cs.shape[-1]
    tril = jnp.tril(jnp.ones((n, n), dtype=bool))
    diff = cs[..., :, None] - cs[..., None, :]
    return jnp.where(tril, jnp.exp(jnp.where(tril, diff, 0.0)), 0.0)


def swa_sink_alibi(q, k, v, sinks):
    bsz, t = q.shape[0], q.shape[1]
    nc = t // CHUNK
    grp = ATT_HEADS // ATT_KV_HEADS
    band = (WIN_CHUNKS + 1) * CHUNK
    qc = q.reshape(bsz, nc, CHUNK, ATT_KV_HEADS, grp, ATT_HEAD_DIM)
    padw = ((0, 0), (WIN_CHUNKS * CHUNK, 0), (0, 0))
    kp = jnp.pad(k, padw).reshape(bsz, nc + WIN_CHUNKS, CHUNK, ATT_KV_HEADS, ATT_HEAD_DIM)
    vp = jnp.pad(v, padw).reshape(bsz, nc + WIN_CHUNKS, CHUNK, ATT_KV_HEADS, ATT_HEAD_DIM)
    kb = jnp.concatenate([kp[:, j:j + nc] for j in range(WIN_CHUNKS + 1)], axis=2)
    vb = jnp.concatenate([vp[:, j:j + nc] for j in range(WIN_CHUNKS + 1)], axis=2)
    s = jnp.einsum('bcikgd,bcjkd->bckgij', qc, kb).astype(jnp.float32) * (ATT_HEAD_DIM ** -0.5)
    qi = jnp.arange(CHUNK)[:, None]
    kj = jnp.arange(band)[None, :]
    dist = jnp.abs(qi + WIN_CHUNKS * CHUNK - kj).astype(jnp.float32)
    slopes = alibi_slopes(ATT_HEADS).reshape(ATT_KV_HEADS, grp)
    s = s - slopes[:, :, None, None] * dist
    key_chunk = jnp.arange(nc)[:, None] - WIN_CHUNKS + (jnp.arange(band) // CHUNK)[None, :]
    valid = key_chunk >= 0
    s = jnp.where(valid[None, :, None, None, None, :], s, -jnp.inf)
    sink = jnp.broadcast_to(sinks.astype(jnp.float32).reshape(1, 1, ATT_KV_HEADS, grp, 1, 1),
                            s.shape[:-1] + (1,))
    p = jax.nn.softmax(jnp.concatenate([s, sink], axis=-1), axis=-1)[..., :band]
    o = jnp.einsum('bckgij,bcjkd->bcikgd', p.astype(v.dtype), vb)
    return o.reshape(bsz, t, ATT_DIM)


def ssd_mixer(z, xbc, dt, conv_w, conv_b, dt_bias, a_log, d_skip, norm_w):
    bsz, t = xbc.shape[0], xbc.shape[1]
    nc = t // CHUNK
    hpg = SSD_HEADS // SSD_GROUPS
    xbc = jax.nn.silu(causal_dwconv(xbc, conv_w, conv_b)).astype(jnp.float32)
    xs, bm, cm = jnp.split(xbc, [SSD_DIM, SSD_DIM + SSD_GROUPS * SSD_STATE], axis=-1)
    xs = xs.reshape(bsz, t, SSD_HEADS, SSD_HEAD_DIM)
    bm = jnp.repeat(bm.reshape(bsz, t, SSD_GROUPS, SSD_STATE), hpg, axis=2)
    cm = jnp.repeat(cm.reshape(bsz, t, SSD_GROUPS, SSD_STATE), hpg, axis=2)
    dt = jax.nn.softplus(dt.astype(jnp.float32) + dt_bias.astype(jnp.float32))
    a = -jnp.exp(a_log.astype(jnp.float32))
    xc = (xs * dt[..., None]).reshape(bsz, nc, CHUNK, SSD_HEADS, SSD_HEAD_DIM)
    bc = bm.reshape(bsz, nc, CHUNK, SSD_HEADS, SSD_STATE)
    cc = cm.reshape(bsz, nc, CHUNK, SSD_HEADS, SSD_STATE)
    da = (dt * a).reshape(bsz, nc, CHUNK, SSD_HEADS).transpose(0, 3, 1, 2)
    a_cs = jnp.cumsum(da, axis=-1)
    lmat = lower_exp_diff(a_cs)
    scores = jnp.einsum('bclhn,bcshn->bhcls', cc, bc) * lmat
    y_diag = jnp.einsum('bhcls,bcshp->bclhp', scores, xc)
    decay_states = jnp.exp(a_cs[..., -1:] - a_cs)
    chunk_states = jnp.einsum('bclhn,bhcl,bclhp->bchpn', bc, decay_states, xc)
    chunk_decay = jnp.exp(a_cs[..., -1])

    def step(state, inp):
        new, dec = inp
        return state * dec[..., None, None] + new, state

    init = jnp.zeros((bsz, SSD_HEADS, SSD_HEAD_DIM, SSD_STATE), jnp.float32)
    _, prev = lax.scan(step, init, (chunk_states.transpose(1, 0, 2, 3, 4),
                                    chunk_decay.transpose(2, 0, 1)))
    prev = prev.transpose(1, 0, 2, 3, 4)
    y_off = jnp.einsum('bclhn,bchpn,bhcl->bclhp', cc, prev, jnp.exp(a_cs))
    y = (y_diag + y_off).reshape(bsz, t, SSD_HEADS, SSD_HEAD_DIM) + xs * d_skip.astype(jnp.float32)[:, None]
    g = y.reshape(bsz, t, SSD_DIM) * jax.nn.silu(z.astype(jnp.float32))
    g = g.reshape(bsz, t, SSD_GROUPS, SSD_DIM // SSD_GROUPS)
    g = g * lax.rsqrt(jnp.mean(g * g, axis=-1, keepdims=True) + EPS)
    return (g.reshape(bsz, t, SSD_DIM) * norm_w.astype(jnp.float32)).astype(z.dtype)


def gdn_mixer(qkv, z, b, a, conv_w, dt_bias, a_log, norm_w):
    bsz, t = qkv.shape[0], qkv.shape[1]
    nc = t // CHUNK
    qkv = jax.nn.silu(causal_dwconv(qkv, conv_w)).astype(jnp.float32)
    q, k, v = jnp.split(qkv, [GDN_KDIM, 2 * GDN_KDIM], axis=-1)
    q = l2norm(q.reshape(bsz, t, GDN_HEADS, GDN_HEAD_K)) * (GDN_HEAD_K ** -0.5)
    k = l2norm(k.reshape(bsz, t, GDN_HEADS, GDN_HEAD_K))
    v = v.reshape(bsz, t, GDN_HEADS, GDN_HEAD_V)
    beta = jax.nn.sigmoid(b.astype(jnp.float32))
    g = -jnp.exp(a_log.astype(jnp.float32)) * jax.nn.softplus(a.astype(jnp.float32) + dt_bias.astype(jnp.float32))

    def chunks(u):
        return u.reshape(bsz, nc, CHUNK, GDN_HEADS, u.shape[-1]).transpose(0, 3, 1, 2, 4)

    qc, kc, vc = chunks(q), chunks(k), chunks(v)
    betac = beta.reshape(bsz, nc, CHUNK, GDN_HEADS).transpose(0, 3, 1, 2)
    gc = jnp.cumsum(g.reshape(bsz, nc, CHUNK, GDN_HEADS).transpose(0, 3, 1, 2), axis=-1)
    decay = lower_exp_diff(gc)
    kbeta = kc * betac[..., None]
    strict = jnp.tril(jnp.einsum('bhcid,bhcjd->bhcij', kbeta, kc) * decay, -1)
    rhs = jnp.concatenate([vc * betac[..., None], kbeta * jnp.exp(gc)[..., None]], axis=-1)
    sol = lax.linalg.triangular_solve(strict, rhs, left_side=True, lower=True, unit_diagonal=True)
    u, w = jnp.split(sol, [GDN_HEAD_V], axis=-1)
    qk = jnp.einsum('bhcid,bhcjd->bhcij', qc, kc) * decay

    def step(state, inp):
        q_i, k_i, u_i, w_i, qk_i, g_i = inp
        v_new = u_i - jnp.einsum('bhld,bhde->bhle', w_i, state)
        o = (jnp.einsum('bhld,bhde->bhle', q_i * jnp.exp(g_i)[..., None], state)
             + jnp.einsum('bhls,bhse->bhle', qk_i, v_new))
        g_last = g_i[..., -1]
        state = (state * jnp.exp(g_last)[..., None, None]
                 + jnp.einsum('bhld,bhle->bhde', k_i * jnp.exp(g_last[..., None] - g_i)[..., None], v_new))
        return state, o

    xs = tuple(jnp.moveaxis(u_, 2, 0) for u_ in (qc, kc, u, w, qk, gc))
    init = jnp.zeros((bsz, GDN_HEADS, GDN_HEAD_K, GDN_HEAD_V), jnp.float32)
    _, o = lax.scan(step, init, xs)
    o = o.transpose(1, 0, 3, 2, 4).reshape(bsz, t, GDN_HEADS, GDN_HEAD_V)
    o = o * lax.rsqrt(jnp.mean(o * o, axis=-1, keepdims=True) + EPS) * norm_w.astype(jnp.float32)
    o = o * jax.nn.silu(z.astype(jnp.float32).reshape(bsz, t, GDN_HEADS, GDN_HEAD_V))
    return o.reshape(bsz, t, GDN_DIM).astype(z.dtype)


def setup_inputs(seed: int = 0) -> dict:
    key = jax.random.key(seed)
    ks = jax.random.split(key, 24)
    L = DEPTH

    def nrm(k, shape, scale):
        return jax.random.normal(k, shape, jnp.float32) * scale

    def gain(k, shape):
        return 1.0 + 0.05 * jax.random.normal(k, shape, jnp.float32)

    def dt_bias_init(k, shape):
        u = jax.random.uniform(k, shape, jnp.float32, math.log(1e-3), math.log(1e-1))
        dtv = jnp.exp(u)
        return dtv + jnp.log(-jnp.expm1(-dtv))

    def a_log_init(k, shape):
        return jnp.log(jax.random.uniform(k, shape, jnp.float32, 1.0, 16.0))

    return {
        "x": nrm(ks[0], (BATCH, SEQ, D_MODEL), 1.0),
        "pre_mix_norm": gain(ks[1], (L, D_MODEL)),
        "post_mix_norm": gain(ks[2], (L, D_MODEL)),
        "pre_ffn_norm": gain(ks[3], (L, D_MODEL)),
        "post_ffn_norm": gain(ks[4], (L, D_MODEL)),
        "w_in": nrm(ks[5], (L, D_MODEL, IN_COLS), D_MODEL ** -0.5),
        "w_out": nrm(ks[6], (L, D_MIX, D_MODEL), D_MIX ** -0.5),
        "attn_sinks": nrm(ks[7], (L, ATT_HEADS), 0.5),
        "ssd_conv_w": nrm(ks[8], (L, CONV_K, SSD_CONV_DIM), CONV_K ** -0.5),
        "ssd_conv_b": nrm(ks[9], (L, SSD_CONV_DIM), 0.01),
        "ssd_dt_bias": dt_bias_init(ks[10], (L, SSD_HEADS)),
        "ssd_A_log": a_log_init(ks[11], (L, SSD_HEADS)),
        "ssd_D": 1.0 + 0.1 * jax.random.normal(ks[12], (L, SSD_HEADS), jnp.float32),
        "ssd_norm_w": gain(ks[13], (L, SSD_DIM)),
        "gdn_conv_w": nrm(ks[14], (L, CONV_K, GDN_CONV_DIM), CONV_K ** -0.5),
        "gdn_dt_bias": dt_bias_init(ks[15], (L, GDN_HEADS)),
        "gdn_A_log": a_log_init(ks[16], (L, GDN_HEADS)),
        "gdn_norm_w": gain(ks[17], (L, GDN_HEAD_V)),
        "ffn_w_gate": nrm(ks[18], (L, D_MODEL, FF), D_MODEL ** -0.5),
        "ffn_w_up": nrm(ks[19], (L, D_MODEL, FF), D_MODEL ** -0.5),
        "ffn_w_down": nrm(ks[20], (L, FF, D_MODEL), FF ** -0.5),
    }


def reference(x, pre_mix_norm, post_mix_norm, pre_ffn_norm, post_ffn_norm, w_in, w_out,
              attn_sinks, ssd_conv_w, ssd_conv_b, ssd_dt_bias, ssd_A_log, ssd_D, ssd_norm_w,
              gdn_conv_w, gdn_dt_bias, gdn_A_log, gdn_norm_w, ffn_w_gate, ffn_w_up, ffn_w_down):
    offs = _offsets(IN_SIZES)
    for l in range(DEPTH):
        h = rmsnorm(x, pre_mix_norm[l])
        proj = h @ w_in[l]
        (a_q, a_k, a_v, s_z, s_xbc, s_dt, g_qkv, g_z, g_b, g_a) = jnp.split(proj, offs, axis=-1)
        att = swa_sink_alibi(a_q, a_k, a_v, attn_sinks[l])
        ssd = ssd_mixer(s_z, s_xbc, s_dt, ssd_conv_w[l], ssd_conv_b[l], ssd_dt_bias[l],
                        ssd_A_log[l], ssd_D[l], ssd_norm_w[l])
        gdn = gdn_mixer(g_qkv, g_z, g_b, g_a, gdn_conv_w[l], gdn_dt_bias[l], gdn_A_log[l], gdn_norm_w[l])
        mix = jnp.concatenate([att, ssd, gdn], axis=-1) @ w_out[l]
        x = x + rmsnorm(mix, post_mix_norm[l])
        h = rmsnorm(x, pre_ffn_norm[l])
        f = (jax.nn.silu(h @ ffn_w_gate[l]) * (h @ ffn_w_up[l])) @ ffn_w_down[l]
        x = x + rmsnorm(f, post_ffn_norm[l])
    return x
```

```python
import functools

import jax
import jax.numpy as jnp
from jax import lax
from jax.experimental import pallas as pl
from jax.experimental.pallas import tpu as pltpu

F32 = jnp.float32
BF16 = jnp.bfloat16

D_MODEL = 1024
CHUNK = 64
EPS = 1e-6
CONV_K = 4

SSD_HEADS = 8
SSD_HEAD_DIM = 64
SSD_DIM = SSD_HEADS * SSD_HEAD_DIM
SSD_GROUPS = 2
SSD_STATE = 128
SSD_HPG = SSD_HEADS // SSD_GROUPS
SSD_CONV_DIM = SSD_DIM + 2 * SSD_GROUPS * SSD_STATE

ATT_HEADS = 4
ATT_KV_HEADS = 2
ATT_HEAD_DIM = 64
ATT_DIM = ATT_HEADS * ATT_HEAD_DIM
ATT_KV_DIM = ATT_KV_HEADS * ATT_HEAD_DIM
WINDOW = 128
WIN_CHUNKS = WINDOW // CHUNK

GDN_HEADS = 4
GDN_HEAD_K = 64
GDN_HEAD_V = 64
GDN_KDIM = GDN_HEADS * GDN_HEAD_K
GDN_DIM = GDN_HEADS * GDN_HEAD_V
GDN_CONV_DIM = 2 * GDN_KDIM + GDN_DIM

FF = 2816
LANES = 128
SUBLANES = 8

SMALL_DT = 0
SMALL_BETA = SSD_HEADS
SMALL_DECAY = SSD_HEADS + GDN_HEADS
PROJ_WIDTHS = (ATT_DIM, ATT_KV_DIM, ATT_KV_DIM, SSD_DIM, SSD_CONV_DIM, GDN_CONV_DIM, GDN_DIM, LANES)
PROJ_COLS = sum(PROJ_WIDTHS)

TM_PROJ = 512
TM_FFN = 512
FF_CHUNK = 256
TB = 256
VMEM_LIMIT = 56 * 1024 * 1024


def _bf(x):
    return x.astype(BF16)


def _mm(a, b):
    return jnp.dot(_bf(a), _bf(b), preferred_element_type=F32)


def _mm_nt(a, b):
    return lax.dot_general(_bf(a), _bf(b), (((1,), (1,)), ((), ())), preferred_element_type=F32)


def _mm_tn(a, b):
    return lax.dot_general(_bf(a), _bf(b), (((0,), (0,)), ((), ())), preferred_element_type=F32)


def _split3(x):
    hi = x.astype(BF16)
    r1 = x - hi.astype(F32)
    mid = r1.astype(BF16)
    lo = (r1 - mid.astype(F32)).astype(BF16)
    return hi, mid, lo


def _mm_exact_lhs(m_bf16, x):
    hi, mid, lo = _split3(x)
    dot = functools.partial(jnp.dot, preferred_element_type=F32)
    return dot(m_bf16, hi) + dot(m_bf16, mid) + dot(m_bf16, lo)


def _mm_exact_rhs(x, m_bf16):
    hi, mid, lo = _split3(x)
    dot = functools.partial(jnp.dot, preferred_element_type=F32)
    return dot(hi, m_bf16) + dot(mid, m_bf16) + dot(lo, m_bf16)


def _sigmoid(x):
    return 1.0 / (1.0 + jnp.exp(-x))


def _silu(x):
    return x * _sigmoid(x)


def _softplus(x):
    return jnp.maximum(x, 0.0) + jnp.log1p(jnp.exp(-jnp.abs(x)))


def _rmsnorm(x, w):
    return x * lax.rsqrt(jnp.mean(x * x, axis=-1, keepdims=True) + EPS) * w


def _seg_cumsum_matrix(n, seg):
    ri = lax.broadcasted_iota(jnp.int32, (n, n), 0)
    ci = lax.broadcasted_iota(jnp.int32, (n, n), 1)
    same = (ri // seg) == (ci // seg)
    return jnp.where(same, jnp.where(ci <= ri, 1.0, 0.0), 0.0).astype(BF16)


def _head_expand_matrix(src_off, n_heads, width):
    ri = lax.broadcasted_iota(jnp.int32, (LANES, n_heads * width), 0)
    ci = lax.broadcasted_iota(jnp.int32, (LANES, n_heads * width), 1)
    return jnp.where(ri - src_off == ci // width, 1.0, 0.0).astype(BF16)


def _block_ones(n, seg):
    ri = lax.broadcasted_iota(jnp.int32, (n, n), 0)
    ci = lax.broadcasted_iota(jnp.int32, (n, n), 1)
    return jnp.where((ri // seg) == (ci // seg), 1.0, 0.0).astype(BF16)


def _causal_conv(x, tail_ref, w_ref):
    n = x.shape[0]
    tail = tail_ref[...]
    row = lax.broadcasted_iota(jnp.int32, (SUBLANES, x.shape[1]), 0)
    acc = x * w_ref[CONV_K - 1:CONV_K, :]
    for j in range(1, CONV_K):
        xs = pltpu.roll(x, j, axis=0)
        ts = pltpu.roll(tail, j, axis=0)
        first = jnp.where(row < j, ts, xs[0:SUBLANES])
        shifted = jnp.concatenate([first, xs[SUBLANES:]], axis=0)
        acc = acc + shifted * w_ref[CONV_K - 1 - j:CONV_K - j, :]
    tail_ref[...] = x[n - SUBLANES:n]
    return acc


def _unit_lower_solve(a, rhs):
    p = -a
    sol = rhs
    steps = CHUNK.bit_length() - 1
    for j in range(steps):
        sol = sol + _mm(p, sol)
        if j + 1 < steps:
            p = _mm(p, p)
    return sol


def _proj_kernel(x_ref, nw_ref, w_ref, *out_refs):
    h = _bf(_rmsnorm(x_ref[...], nw_ref[...]))
    off = 0
    for ref, width in zip(out_refs, PROJ_WIDTHS):
        ref[...] = jnp.dot(h, w_ref[:, off:off + width], preferred_element_type=F32)
        off += width


def _proj_call(x2d, pre_norm, w_in_r, layer):
    m = x2d.shape[0]
    grid = (m // TM_PROJ,)
    out_shape = tuple(jax.ShapeDtypeStruct((m, w), F32) for w in PROJ_WIDTHS)
    out_specs = tuple(pl.BlockSpec((TM_PROJ, w), lambda i: (i, 0)) for w in PROJ_WIDTHS)
    return pl.pallas_call(
        _proj_kernel,
        grid=grid,
        in_specs=[
            pl.BlockSpec((TM_PROJ, D_MODEL), lambda i: (i, 0)),
            pl.BlockSpec((None, 1, D_MODEL), lambda i: (layer, 0, 0)),
            pl.BlockSpec((None, D_MODEL, PROJ_COLS), lambda i: (layer, 0, 0),
                         pipeline_mode=pl.Buffered(1)),
        ],
        out_specs=out_specs,
        out_shape=out_shape,
        compiler_params=pltpu.CompilerParams(
            dimension_semantics=("arbitrary",), vmem_limit_bytes=VMEM_LIMIT),
        name="proj",
    )(x2d, pre_norm, w_in_r)


def _attention(q, k, v, kprev_ref, vprev_ref, sinks_ref, layer, t, mix_ref):
    nk = TB + WINDOW
    kext = jnp.concatenate([kprev_ref[...], k], axis=0)
    vext = jnp.concatenate([vprev_ref[...], v], axis=0)
    kprev_ref[...] = k[TB - WINDOW:TB]
    vprev_ref[...] = v[TB - WINDOW:TB]
    qi = lax.broadcasted_iota(jnp.int32, (TB, nk), 0)
    kj = lax.broadcasted_iota(jnp.int32, (TB, nk), 1)
    dist = jnp.abs(qi + WINDOW - kj).astype(F32)
    qc = qi // CHUNK
    kc = kj // CHUNK - WIN_CHUNKS
    in_band = jnp.where(kc <= qc, jnp.where(kc >= qc - WIN_CHUNKS, 1, 0), 0)
    exists = jnp.where(kc + t * (TB // CHUNK) >= 0, 1, 0)
    valid = (in_band * exists) > 0
    grp = ATT_HEADS // ATT_KV_HEADS
    for kh in range(ATT_KV_HEADS):
        kk = kext[:, kh * ATT_HEAD_DIM:(kh + 1) * ATT_HEAD_DIM]
        vv = vext[:, kh * ATT_HEAD_DIM:(kh + 1) * ATT_HEAD_DIM]
        for gi in range(grp):
            h = kh * grp + gi
            slope = 2.0 ** (-8.0 * (h + 1) / ATT_HEADS)
            sink = sinks_ref[layer * ATT_HEADS + h]
            qh = q[:, h * ATT_HEAD_DIM:(h + 1) * ATT_HEAD_DIM]
            s = _mm_nt(qh, kk) * (ATT_HEAD_DIM ** -0.5) - slope * dist
            s = jnp.where(valid, s, -1e30)
            m = jnp.maximum(jnp.max(s, axis=-1, keepdims=True), sink)
            p = jnp.exp(s - m)
            den = jnp.sum(p, axis=-1, keepdims=True) + jnp.exp(sink - m)
            mix_ref[:, h * ATT_HEAD_DIM:(h + 1) * ATT_HEAD_DIM] = _mm(p, vv) / den


def _ssd(xbc, z, sp, cs, cs_t, dvec, norm_w, state_ref, mix_ref):
    xs = xbc[:, :SSD_DIM]
    bm = xbc[:, SSD_DIM:SSD_DIM + SSD_GROUPS * SSD_STATE]
    cm = xbc[:, SSD_DIM + SSD_GROUPS * SSD_STATE:]
    expand = _head_expand_matrix(SMALL_DT, SSD_HEADS, SSD_HEAD_DIM)
    cs_last = cs[TB - 1:TB, :]
    xc = xs * _mm_exact_rhs(sp, expand)
    xcd = xc * _mm_exact_rhs(jnp.exp(cs_last - cs), expand)
    ecs = _mm_exact_rhs(jnp.exp(cs), expand)
    elast = _mm_exact_rhs(jnp.broadcast_to(jnp.exp(cs_last), (SUBLANES, LANES)), expand)[0:1]
    ri = lax.broadcasted_iota(jnp.int32, (TB, TB), 0)
    ci = lax.broadcasted_iota(jnp.int32, (TB, TB), 1)
    tri = ci <= ri
    gw = SSD_HPG * SSD_HEAD_DIM
    ys = []
    for g in range(SSD_GROUPS):
        bg = bm[:, g * SSD_STATE:(g + 1) * SSD_STATE]
        cg = cm[:, g * SSD_STATE:(g + 1) * SSD_STATE]
        cb = _mm_nt(cg, bg)
        prev = state_ref[g]
        y_off = _mm(cg, prev) * ecs[:, g * gw:(g + 1) * gw]
        state_ref[g] = prev * elast[:, g * gw:(g + 1) * gw] + _mm_tn(bg, xcd[:, g * gw:(g + 1) * gw])
        yd = []
        for hh in range(SSD_HPG):
            h = g * SSD_HPG + hh
            diff = cs[:, h:h + 1] - cs_t[h:h + 1, :]
            lmat = jnp.where(tri, jnp.exp(jnp.where(tri, diff, 0.0)), 0.0)
            yd.append(_mm(cb * lmat, xc[:, h * SSD_HEAD_DIM:(h + 1) * SSD_HEAD_DIM]))
        ys.append(jnp.concatenate(yd, axis=1) + y_off)
    y = jnp.concatenate(ys, axis=1) + xs * dvec
    gated = y * _silu(z)
    for g in range(SSD_GROUPS):
        gg = gated[:, g * gw:(g + 1) * gw]
        gg = gg * lax.rsqrt(jnp.mean(gg * gg, axis=-1, keepdims=True) + EPS)
        mix_ref[:, ATT_DIM + g * gw:ATT_DIM + (g + 1) * gw] = gg * norm_w[:, g * gw:(g + 1) * gw]


def _gdn(gq, gz, beta, gc, gc_t, norm_w, state_ref, go_ref, mix_ref):
    q = gq[:, :GDN_KDIM]
    k = gq[:, GDN_KDIM:2 * GDN_KDIM]
    v = gq[:, 2 * GDN_KDIM:]
    ones = _block_ones(GDN_KDIM, GDN_HEAD_K)

    def seg_sum(x2):
        hi = x2.astype(BF16)
        lo = (x2 - hi.astype(F32)).astype(BF16)
        return (jnp.dot(hi, ones, preferred_element_type=F32)
                + jnp.dot(lo, ones, preferred_element_type=F32))

    qn = q * lax.rsqrt(seg_sum(q * q) + EPS) * (GDN_HEAD_K ** -0.5)
    kn = k * lax.rsqrt(seg_sum(k * k) + EPS)
    exp_beta = _head_expand_matrix(SMALL_BETA, GDN_HEADS, GDN_HEAD_K)
    exp_dec = _head_expand_matrix(SMALL_DECAY, GDN_HEADS, GDN_HEAD_K)
    beta_b = _mm_exact_rhs(beta, exp_beta)
    eg_b = _mm_exact_rhs(jnp.exp(gc), exp_dec)
    kbeta = kn * beta_b
    vbeta = v * beta_b
    kbg = kbeta * eg_b
    qeg = qn * eg_b
    ri = lax.broadcasted_iota(jnp.int32, (CHUNK, CHUNK), 0)
    ci = lax.broadcasted_iota(jnp.int32, (CHUNK, CHUNK), 1)
    tri = ci <= ri
    strict = ci < ri
    for c in range(TB // CHUNK):
        r0, r1 = c * CHUNK, (c + 1) * CHUNK
        g_last = gc[r1 - 1:r1, :]
        kd = kn[r0:r1] * _mm_exact_rhs(jnp.exp(g_last - gc[r0:r1]), exp_dec)
        e_last = jnp.exp(g_last)
        for h in range(GDN_HEADS):
            c0, c1 = h * GDN_HEAD_K, (h + 1) * GDN_HEAD_K
            lane = SMALL_DECAY + h
            diff = gc[r0:r1, lane:lane + 1] - gc_t[lane:lane + 1, r0:r1]
            decay = jnp.where(tri, jnp.exp(jnp.where(tri, diff, 0.0)), 0.0)
            kh = kn[r0:r1, c0:c1]
            a = jnp.where(strict, _mm_nt(kbeta[r0:r1, c0:c1], kh) * decay, 0.0)
            rhs = jnp.concatenate([vbeta[r0:r1, c0:c1], kbg[r0:r1, c0:c1]], axis=1)
            sol = _unit_lower_solve(a, rhs)
            u = sol[:, :GDN_HEAD_V]
            w = sol[:, GDN_HEAD_V:]
            qk = _mm_nt(qn[r0:r1, c0:c1], kh) * decay
            state = state_ref[h]
            v_new = u - _mm(w, state)
            go_ref[r0:r1, c0:c1] = _mm(qeg[r0:r1, c0:c1], state) + _mm(qk, v_new)
            state_ref[h] = state * e_last[:, lane:lane + 1] + _mm_tn(kd[:, c0:c1], v_new)
    o = go_ref[...]
    ms = seg_sum(o * o) * (1.0 / GDN_HEAD_V)
    mix_ref[:, ATT_DIM + SSD_DIM:] = o * lax.rsqrt(ms + EPS) * norm_w * _silu(gz)


def _mixer_kernel(layer,
                  sinks_ref, q_ref, k_ref, v_ref, z_ref, xbc_ref, gqkv_ref, gz_ref, sm_ref, x_ref,
                  wout_ref, postw_ref, sconvw_ref, sconvb_ref, gconvw_ref, bias_ref, alog_ref,
                  dvec_ref, snormw_ref, gnormw_ref,
                  o_ref,
                  stail_ref, gtail_ref, kprev_ref, vprev_ref, sstate_ref, gstate_ref, go_ref, mix_ref):
    t = pl.program_id(1)

    @pl.when(t == 0)
    def _():
        stail_ref[...] = jnp.zeros_like(stail_ref)
        gtail_ref[...] = jnp.zeros_like(gtail_ref)
        kprev_ref[...] = jnp.zeros_like(kprev_ref)
        vprev_ref[...] = jnp.zeros_like(vprev_ref)
        sstate_ref[...] = jnp.zeros_like(sstate_ref)
        gstate_ref[...] = jnp.zeros_like(gstate_ref)

    _attention(q_ref[...], k_ref[...], v_ref[...], kprev_ref, vprev_ref, sinks_ref, layer, t, mix_ref)

    sm = sm_ref[...]
    sp = _softplus(sm + bias_ref[...])
    lane = lax.broadcasted_iota(jnp.int32, (1, LANES), 1)
    is_decay = jnp.where(lane < SSD_HEADS, 1.0, jnp.where(lane >= SMALL_DECAY, jnp.where(lane < SMALL_DECAY + GDN_HEADS, 1.0, 0.0), 0.0))
    da = sp * (-jnp.exp(alog_ref[...]) * is_decay)
    cs = _mm_exact_lhs(_seg_cumsum_matrix(TB, TB), da)
    gc = _mm_exact_lhs(_seg_cumsum_matrix(TB, CHUNK), da)

    xbc = _silu(_causal_conv(xbc_ref[...], stail_ref, sconvw_ref) + sconvb_ref[...])
    _ssd(xbc, z_ref[...], sp, cs, cs.T, dvec_ref[...], snormw_ref[...], sstate_ref, mix_ref)

    gq = _silu(_causal_conv(gqkv_ref[...], gtail_ref, gconvw_ref))
    _gdn(gq, gz_ref[...], _sigmoid(sm), gc, gc.T, gnormw_ref[...], gstate_ref, go_ref, mix_ref)

    mixed = jnp.dot(_bf(mix_ref[...]), wout_ref[...], preferred_element_type=F32)
    o_ref[...] = x_ref[...] + _rmsnorm(mixed, postw_ref[...])


def _mixer_call(projs, x3d, params, layer):
    bsz, seq, _ = x3d.shape
    grid = (bsz, seq // TB)

    def act_spec(width):
        return pl.BlockSpec((None, TB, width), lambda b, t: (b, t, 0))

    def par_spec(arr):
        shape = arr.shape[1:]
        zeros = (0,) * len(shape)
        return pl.BlockSpec((None,) + shape, lambda b, t: (layer,) + zeros)

    (sinks, wout, postw, sconvw, sconvb, gconvw, bias, alog, dvec, snormw, gnormw) = params
    vmem_params = (wout, postw, sconvw, sconvb, gconvw, bias, alog, dvec, snormw, gnormw)
    in_specs = ([pl.BlockSpec(memory_space=pltpu.SMEM)]
                + [act_spec(w) for w in PROJ_WIDTHS]
                + [act_spec(D_MODEL)]
                + [par_spec(p) for p in vmem_params])
    scratch = [
        pltpu.VMEM((SUBLANES, SSD_CONV_DIM), F32),
        pltpu.VMEM((SUBLANES, GDN_CONV_DIM), F32),
        pltpu.VMEM((WINDOW, ATT_KV_DIM), F32),
        pltpu.VMEM((WINDOW, ATT_KV_DIM), F32),
        pltpu.VMEM((SSD_GROUPS, SSD_STATE, SSD_HPG * SSD_HEAD_DIM), F32),
        pltpu.VMEM((GDN_HEADS, GDN_HEAD_K, GDN_HEAD_V), F32),
        pltpu.VMEM((TB, GDN_DIM), F32),
        pltpu.VMEM((TB, D_MODEL), F32),
    ]
    acts = [p.reshape(bsz, seq, p.shape[-1]) for p in projs]
    return pl.pallas_call(
        functools.partial(_mixer_kernel, layer),
        grid=grid,
        in_specs=in_specs,
        out_specs=act_spec(D_MODEL),
        out_shape=jax.ShapeDtypeStruct(x3d.shape, F32),
        scratch_shapes=scratch,
        compiler_params=pltpu.CompilerParams(
            dimension_semantics=("arbitrary", "arbitrary"), vmem_limit_bytes=VMEM_LIMIT),
        name="mixer",
    )(sinks, *acts, x3d, *vmem_params)


def _ffn_kernel(x_ref, prew_ref, postw_ref, wg_ref, wu_ref, wd_ref, o_ref):
    x = x_ref[...]
    h = _bf(_rmsnorm(x, prew_ref[...]))
    acc = jnp.zeros((TM_FFN, D_MODEL), F32)
    for c in range(FF // FF_CHUNK):
        c0, c1 = c * FF_CHUNK, (c + 1) * FF_CHUNK
        gate = jnp.dot(h, wg_ref[:, c0:c1], preferred_element_type=F32)
        up = jnp.dot(h, wu_ref[:, c0:c1], preferred_element_type=F32)
        acc = acc + jnp.dot(_bf(_silu(gate) * up), wd_ref[c0:c1, :], preferred_element_type=F32)
    o_ref[...] = x + _rmsnorm(acc, postw_ref[...])


def _ffn_call(x2d, pre_norm, post_norm, wg, wu, wd, layer):
    m = x2d.shape[0]

    def wspec(shape):
        return pl.BlockSpec((None,) + shape, lambda i: (layer, 0, 0), pipeline_mode=pl.Buffered(1))

    return pl.pallas_call(
        _ffn_kernel,
        grid=(m // TM_FFN,),
        in_specs=[
            pl.BlockSpec((TM_FFN, D_MODEL), lambda i: (i, 0)),
            pl.BlockSpec((None, 1, D_MODEL), lambda i: (layer, 0, 0)),
            pl.BlockSpec((None, 1, D_MODEL), lambda i: (layer, 0, 0)),
            wspec((D_MODEL, FF)),
            wspec((D_MODEL, FF)),
            wspec((FF, D_MODEL)),
        ],
        out_specs=pl.BlockSpec((TM_FFN, D_MODEL), lambda i: (i, 0)),
        out_shape=jax.ShapeDtypeStruct(x2d.shape, F32),
        compiler_params=pltpu.CompilerParams(
            dimension_semantics=("arbitrary",), vmem_limit_bytes=VMEM_LIMIT),
        name="ffn",
    )(x2d, pre_norm, post_norm, wg, wu, wd)


def _regroup_w_in(w_in):
    n_wide_a = ATT_DIM + 2 * ATT_KV_DIM + SSD_DIM + SSD_CONV_DIM
    dt0 = n_wide_a
    g0 = dt0 + SSD_HEADS
    b0 = g0 + GDN_CONV_DIM + GDN_DIM
    depth = w_in.shape[0]
    pad = jnp.zeros((depth, D_MODEL, LANES - SSD_HEADS - 2 * GDN_HEADS), w_in.dtype)
    return jnp.concatenate(
        [w_in[:, :, :n_wide_a], w_in[:, :, g0:b0], w_in[:, :, dt0:g0], w_in[:, :, b0:], pad],
        axis=-1).astype(BF16)


def _pack_small(depth, ssd_vals, gdn_vals):
    vec = jnp.zeros((depth, 1, LANES), F32)
    vec = vec.at[:, 0, SMALL_DT:SMALL_DT + SSD_HEADS].set(ssd_vals.astype(F32))
    vec = vec.at[:, 0, SMALL_DECAY:SMALL_DECAY + GDN_HEADS].set(gdn_vals.astype(F32))
    return vec


def kernel(x, pre_mix_norm, post_mix_norm, pre_ffn_norm, post_ffn_norm, w_in, w_out, attn_sinks, ssd_conv_w, ssd_conv_b, ssd_dt_bias, ssd_A_log, ssd_D, ssd_norm_w, gdn_conv_w, gdn_dt_bias, gdn_A_log, gdn_norm_w, ffn_w_gate, ffn_w_up, ffn_w_down):
    bsz, seq, _ = x.shape
    depth = w_in.shape[0]
    assert seq % TB == 0 and (bsz * seq) % TM_PROJ == 0 and (bsz * seq) % TM_FFN == 0

    w_in_r = _regroup_w_in(w_in)
    vec3 = lambda a: a.reshape(depth, 1, a.shape[-1]).astype(F32)
    params = (
        attn_sinks.reshape(-1).astype(F32),
        w_out.astype(BF16),
        vec3(post_mix_norm),
        ssd_conv_w.astype(F32),
        vec3(ssd_conv_b),
        gdn_conv_w.astype(F32),
        _pack_small(depth, ssd_dt_bias, gdn_dt_bias),
        _pack_small(depth, ssd_A_log, gdn_A_log),
        vec3(jnp.repeat(ssd_D, SSD_HEAD_DIM, axis=-1)),
        vec3(ssd_norm_w),
        vec3(jnp.tile(gdn_norm_w, (1, GDN_HEADS))),
    )
    pre_mix = vec3(pre_mix_norm)
    pre_ffn = vec3(pre_ffn_norm)
    post_ffn = vec3(post_ffn_norm)
    wg = ffn_w_gate.astype(BF16)
    wu = ffn_w_up.astype(BF16)
    wd = ffn_w_down.astype(BF16)

    for layer in range(depth):
        projs = _proj_call(x.reshape(bsz * seq, D_MODEL), pre_mix, w_in_r, layer)
        x = _mixer_call(projs, x, params, layer)
        x = _ffn_call(x.reshape(bsz * seq, D_MODEL), pre_ffn, post_ffn, wg, wu, wd, layer).reshape(bsz, seq, D_MODEL)
    return x
```

```python
import functools

import jax
import jax.numpy as jnp
from jax import lax
from jax.experimental import pallas as pl
from jax.experimental.pallas import tpu as pltpu

F32 = jnp.float32
BF16 = jnp.bfloat16

D_MODEL = 1024
CHUNK = 64
EPS = 1e-6
CONV_K = 4

SSD_HEADS = 8
SSD_HEAD_DIM = 64
SSD_DIM = SSD_HEADS * SSD_HEAD_DIM
SSD_GROUPS = 2
SSD_STATE = 128
SSD_HPG = SSD_HEADS // SSD_GROUPS
SSD_CONV_DIM = SSD_DIM + 2 * SSD_GROUPS * SSD_STATE

ATT_HEADS = 4
ATT_KV_HEADS = 2
ATT_HEAD_DIM = 64
ATT_DIM = ATT_HEADS * ATT_HEAD_DIM
ATT_KV_DIM = ATT_KV_HEADS * ATT_HEAD_DIM
WINDOW = 128
WIN_CHUNKS = WINDOW // CHUNK

GDN_HEADS = 4
GDN_HEAD_K = 64
GDN_HEAD_V = 64
GDN_KDIM = GDN_HEADS * GDN_HEAD_K
GDN_DIM = GDN_HEADS * GDN_HEAD_V
GDN_CONV_DIM = 2 * GDN_KDIM + GDN_DIM

FF = 2816
LANES = 128
SUBLANES = 8

SMALL_DT = 0
SMALL_BETA = SSD_HEADS
SMALL_DECAY = SSD_HEADS + GDN_HEADS
PROJ_WIDTHS = (ATT_DIM, ATT_KV_DIM, ATT_KV_DIM, SSD_DIM, SSD_CONV_DIM, GDN_CONV_DIM, GDN_DIM, LANES)
PROJ_COLS = sum(PROJ_WIDTHS)

TM_PROJ = 512
TM_FFN = 512
FF_CHUNK = 256
TB = 256
VMEM_LIMIT = 56 * 1024 * 1024


def _bf(x):
    return x.astype(BF16)


def _mm(a, b):
    return jnp.dot(_bf(a), _bf(b), preferred_element_type=F32)


def _mm_nt(a, b):
    return lax.dot_general(_bf(a), _bf(b), (((1,), (1,)), ((), ())), preferred_element_type=F32)


def _mm_tn(a, b):
    return lax.dot_general(_bf(a), _bf(b), (((0,), (0,)), ((), ())), preferred_element_type=F32)


def _split3(x):
    hi = x.astype(BF16)
    r1 = x - hi.astype(F32)
    mid = r1.astype(BF16)
    lo = (r1 - mid.astype(F32)).astype(BF16)
    return hi, mid, lo


def _mm_exact_lhs(m_bf16, x):
    hi, mid, lo = _split3(x)
    dot = functools.partial(jnp.dot, preferred_element_type=F32)
    return dot(m_bf16, hi) + dot(m_bf16, mid) + dot(m_bf16, lo)


def _mm_exact_rhs(x, m_bf16):
    hi, mid, lo = _split3(x)
    dot = functools.partial(jnp.dot, preferred_element_type=F32)
    return dot(hi, m_bf16) + dot(mid, m_bf16) + dot(lo, m_bf16)


def _sigmoid(x):
    return 1.0 / (1.0 + jnp.exp(-x))


def _silu(x):
    return x * _sigmoid(x)


def _softplus(x):
    return jnp.maximum(x, 0.0) + jnp.log1p(jnp.exp(-jnp.abs(x)))


def _rmsnorm(x, w):
    return x * lax.rsqrt(jnp.mean(x * x, axis=-1, keepdims=True) + EPS) * w


def _seg_cumsum_matrix(n, seg):
    ri = lax.broadcasted_iota(jnp.int32, (n, n), 0)
    ci = lax.broadcasted_iota(jnp.int32, (n, n), 1)
    same = (ri // seg) == (ci // seg)
    return jnp.where(same, jnp.where(ci <= ri, 1.0, 0.0), 0.0).astype(BF16)


def _head_expand_matrix(src_off, n_heads, width):
    ri = lax.broadcasted_iota(jnp.int32, (LANES, n_heads * width), 0)
    ci = lax.broadcasted_iota(jnp.int32, (LANES, n_heads * width), 1)
    return jnp.where(ri - src_off == ci // width, 1.0, 0.0).astype(BF16)


def _block_ones(n, seg):
    ri = lax.broadcasted_iota(jnp.int32, (n, n), 0)
    ci = lax.broadcasted_iota(jnp.int32, (n, n), 1)
    return jnp.where((ri // seg) == (ci // seg), 1.0, 0.0).astype(BF16)


def _causal_conv(x, tail_ref, w_ref):
    n = x.shape[0]
    tail = tail_ref[...]
    row = lax.broadcasted_iota(jnp.int32, (SUBLANES, x.shape[1]), 0)
    acc = x * w_ref[CONV_K - 1:CONV_K, :]
    for j in range(1, CONV_K):
        xs = pltpu.roll(x, j, axis=0)
        ts = pltpu.roll(tail, j, axis=0)
        first = jnp.where(row < j, ts, xs[0:SUBLANES])
        shifted = jnp.concatenate([first, xs[SUBLANES:]], axis=0)
        acc = acc + shifted * w_ref[CONV_K - 1 - j:CONV_K - j, :]
    tail_ref[...] = x[n - SUBLANES:n]
    return acc


def _unit_lower_solve(a, rhs):
    p = -a
    sol = rhs
    steps = CHUNK.bit_length() - 1
    for j in range(steps):
        sol = sol + _mm(p, sol)
        if j + 1 < steps:
            p = _mm(p, p)
    return sol


def _proj_kernel(x_ref, nw_ref, w_ref, *out_refs):
    h = _bf(_rmsnorm(x_ref[...], nw_ref[...]))
    off = 0
    for ref, width in zip(out_refs, PROJ_WIDTHS):
        ref[...] = jnp.dot(h, w_ref[:, off:off + width], preferred_element_type=F32)
        off += width


def _proj_call(x2d, pre_norm, w_in_r, layer):
    m = x2d.shape[0]
    grid = (m // TM_PROJ,)
    out_shape = tuple(jax.ShapeDtypeStruct((m, w), F32) for w in PROJ_WIDTHS)
    out_specs = tuple(pl.BlockSpec((TM_PROJ, w), lambda i: (i, 0)) for w in PROJ_WIDTHS)
    return pl.pallas_call(
        _proj_kernel,
        grid=grid,
        in_specs=[
            pl.BlockSpec((TM_PROJ, D_MODEL), lambda i: (i, 0)),
            pl.BlockSpec((None, 1, D_MODEL), lambda i: (layer, 0, 0)),
            pl.BlockSpec((None, D_MODEL, PROJ_COLS), lambda i: (layer, 0, 0),
                         pipeline_mode=pl.Buffered(1)),
        ],
        out_specs=out_specs,
        out_shape=out_shape,
        compiler_params=pltpu.CompilerParams(
            dimension_semantics=("arbitrary",), vmem_limit_bytes=VMEM_LIMIT),
        name="proj",
    )(x2d, pre_norm, w_in_r)


def _attention(q, k, v, kprev_ref, vprev_ref, sinks_ref, layer, t, mix_ref):
    nk = TB + WINDOW
    kext = jnp.concatenate([kprev_ref[...], k], axis=0)
    vext = jnp.concatenate([vprev_ref[...], v], axis=0)
    kprev_ref[...] = k[TB - WINDOW:TB]
    vprev_ref[...] = v[TB - WINDOW:TB]
    qi = lax.broadcasted_iota(jnp.int32, (TB, nk), 0)
    kj = lax.broadcasted_iota(jnp.int32, (TB, nk), 1)
    dist = jnp.abs(qi + WINDOW - kj).astype(F32)
    qc = qi // CHUNK
    kc = kj // CHUNK - WIN_CHUNKS
    in_band = jnp.where(kc <= qc, jnp.where(kc >= qc - WIN_CHUNKS, 1, 0), 0)
    exists = jnp.where(kc + t * (TB // CHUNK) >= 0, 1, 0)
    valid = (in_band * exists) > 0
    grp = ATT_HEADS // ATT_KV_HEADS
    for kh in range(ATT_KV_HEADS):
        kk = kext[:, kh * ATT_HEAD_DIM:(kh + 1) * ATT_HEAD_DIM]
        vv = vext[:, kh * ATT_HEAD_DIM:(kh + 1) * ATT_HEAD_DIM]
        for gi in range(grp):
            h = kh * grp + gi
            slope = 2.0 ** (-8.0 * (h + 1) / ATT_HEADS)
            sink = sinks_ref[layer * ATT_HEADS + h]
            qh = q[:, h * ATT_HEAD_DIM:(h + 1) * ATT_HEAD_DIM]
            s = _mm_nt(qh, kk) * (ATT_HEAD_DIM ** -0.5) - slope * dist
            s = jnp.where(valid, s, -1e30)
            m = jnp.maximum(jnp.max(s, axis=-1, keepdims=True), sink)
            p = jnp.exp(s - m)
            den = jnp.sum(p, axis=-1, keepdims=True) + jnp.exp(sink - m)
            mix_ref[:, h * ATT_HEAD_DIM:(h + 1) * ATT_HEAD_DIM] = _mm(p, vv) / den


def _ssd(xbc, z, sp, cs, cs_t, dvec, norm_w, state_ref, mix_ref):
    xs = xbc[:, :SSD_DIM]
    bm = xbc[:, SSD_DIM:SSD_DIM + SSD_GROUPS * SSD_STATE]
    cm = xbc[:, SSD_DIM + SSD_GROUPS * SSD_STATE:]
    expand = _head_expand_matrix(SMALL_DT, SSD_HEADS, SSD_HEAD_DIM)
    cs_last = cs[TB - 1:TB, :]
    xc = xs * _mm_exact_rhs(sp, expand)
    xcd = xc * _mm_exact_rhs(jnp.exp(cs_last - cs), expand)
    ecs = _mm_exact_rhs(jnp.exp(cs), expand)
    elast = _mm_exact_rhs(jnp.broadcast_to(jnp.exp(cs_last), (SUBLANES, LANES)), expand)[0:1]
    ri = lax.broadcasted_iota(jnp.int32, (TB, TB), 0)
    ci = lax.broadcasted_iota(jnp.int32, (TB, TB), 1)
    tri = ci <= ri
    gw = SSD_HPG * SSD_HEAD_DIM
    ys = []
    for g in range(SSD_GROUPS):
        bg = bm[:, g * SSD_STATE:(g + 1) * SSD_STATE]
        cg = cm[:, g * SSD_STATE:(g + 1) * SSD_STATE]
        cb = _mm_nt(cg, bg)
        prev = state_ref[g]
        y_off = _mm(cg, prev) * ecs[:, g * gw:(g + 1) * gw]
        state_ref[g] = prev * elast[:, g * gw:(g + 1) * gw] + _mm_tn(bg, xcd[:, g * gw:(g + 1) * gw])
        yd = []
        for hh in range(SSD_HPG):
            h = g * SSD_HPG + hh
            diff = cs[:, h:h + 1] - cs_t[h:h + 1, :]
            lmat = jnp.where(tri, jnp.exp(jnp.where(tri, diff, 0.0)), 0.0)
            yd.append(_mm(cb * lmat, xc[:, h * SSD_HEAD_DIM:(h + 1) * SSD_HEAD_DIM]))
        ys.append(jnp.concatenate(yd, axis=1) + y_off)
    y = jnp.concatenate(ys, axis=1) + xs * dvec
    gated = y * _silu(z)
    for g in range(SSD_GROUPS):
        gg = gated[:, g * gw:(g + 1) * gw]
        gg = gg * lax.rsqrt(jnp.mean(gg * gg, axis=-1, keepdims=True) + EPS)
        mix_ref[:, ATT_DIM + g * gw:ATT_DIM + (g + 1) * gw] = gg * norm_w[:, g * gw:(g + 1) * gw]


def _gdn(gq, gz, beta, gc, gc_t, g_end, norm_w, state_ref, mix_ref):
    q = gq[:, :GDN_KDIM]
    k = gq[:, GDN_KDIM:2 * GDN_KDIM]
    v = gq[:, 2 * GDN_KDIM:]
    ones = _block_ones(GDN_KDIM, GDN_HEAD_K)

    def seg_sum(x2):
        hi = x2.astype(BF16)
        lo = (x2 - hi.astype(F32)).astype(BF16)
        return (jnp.dot(hi, ones, preferred_element_type=F32)
                + jnp.dot(lo, ones, preferred_element_type=F32))

    qn = q * lax.rsqrt(seg_sum(q * q) + EPS) * (GDN_HEAD_K ** -0.5)
    kn = k * lax.rsqrt(seg_sum(k * k) + EPS)
    exp_beta = _head_expand_matrix(SMALL_BETA, GDN_HEADS, GDN_HEAD_K)
    exp_dec = _head_expand_matrix(SMALL_DECAY, GDN_HEADS, GDN_HEAD_K)
    beta_b = _mm_exact_rhs(beta, exp_beta)
    eg_b = _mm_exact_rhs(jnp.exp(gc), exp_dec)
    kbeta = kn * beta_b
    vbeta = v * beta_b
    kbg = kbeta * eg_b
    qeg = qn * eg_b
    kd = kn * _mm_exact_rhs(jnp.exp(g_end - gc), exp_dec)
    ri = lax.broadcasted_iota(jnp.int32, (TB, TB), 0)
    ci = lax.broadcasted_iota(jnp.int32, (TB, TB), 1)
    same = (ri // CHUNK) == (ci // CHUNK)
    tri = jnp.logical_and(same, ci <= ri)
    strict = jnp.logical_and(same, ci < ri)
    heads = range(GDN_HEADS)
    n_chunks = TB // CHUNK
    hs = [slice(h * GDN_HEAD_K, (h + 1) * GDN_HEAD_K) for h in heads]
    rs = [slice(c * CHUNK, (c + 1) * CHUNK) for c in range(n_chunks)]

    p, sol, qk = [], [], []
    for h in heads:
        lane = SMALL_DECAY + h
        diff = gc[:, lane:lane + 1] - gc_t[lane:lane + 1, :]
        decay = jnp.where(tri, jnp.exp(jnp.where(tri, diff, 0.0)), 0.0)
        p.append(jnp.where(strict, _mm_nt(kbeta[:, hs[h]], kn[:, hs[h]]) * (-decay), 0.0))
        qk.append(_mm_nt(qn[:, hs[h]], kn[:, hs[h]]) * decay)
        sol.append(jnp.concatenate([vbeta[:, hs[h]], kbg[:, hs[h]]], axis=1))
    steps = CHUNK.bit_length() - 1
    for j in range(steps):
        sol = [sol[h] + _mm(p[h], sol[h]) for h in heads]
        if j + 1 < steps:
            p = [_mm(p[h], p[h]) for h in heads]
    qks = [_mm(qk[h], sol[h]) for h in heads]
    qeff = [qeg[:, hs[h]] - qks[h][:, GDN_HEAD_V:] for h in heads]
    kuw = [[_mm_tn(kd[rs[c], hs[h]], sol[h][rs[c]]) for h in heads] for c in range(n_chunks)]
    state = [state_ref[h] for h in heads]
    states = []
    for c in range(n_chunks):
        states.append(state)
        e_end = jnp.exp(gc[(c + 1) * CHUNK - 1:(c + 1) * CHUNK, :])
        state = [state[h] * e_end[:, SMALL_DECAY + h:SMALL_DECAY + h + 1]
                 - _mm(kuw[c][h][:, GDN_HEAD_V:], state[h]) + kuw[c][h][:, :GDN_HEAD_V]
                 for h in heads]
    for h in heads:
        state_ref[h] = state[h]
    o = jnp.concatenate(
        [jnp.concatenate([_mm(qeff[h][rs[c]], states[c][h]) + qks[h][rs[c], :GDN_HEAD_V] for h in heads], axis=1)
         for c in range(n_chunks)], axis=0)
    ms = seg_sum(o * o) * (1.0 / GDN_HEAD_V)
    mix_ref[:, ATT_DIM + SSD_DIM:] = o * lax.rsqrt(ms + EPS) * norm_w * _silu(gz)


def _mixer_kernel(layer,
                  sinks_ref, q_ref, k_ref, v_ref, z_ref, xbc_ref, gqkv_ref, gz_ref, sm_ref, x_ref,
                  wout_ref, postw_ref, sconvw_ref, sconvb_ref, gconvw_ref, bias_ref, alog_ref,
                  dvec_ref, snormw_ref, gnormw_ref,
                  o_ref,
                  stail_ref, gtail_ref, kprev_ref, vprev_ref, sstate_ref, gstate_ref, mix_ref):
    t = pl.program_id(1)

    @pl.when(t == 0)
    def _():
        stail_ref[...] = jnp.zeros_like(stail_ref)
        gtail_ref[...] = jnp.zeros_like(gtail_ref)
        kprev_ref[...] = jnp.zeros_like(kprev_ref)
        vprev_ref[...] = jnp.zeros_like(vprev_ref)
        sstate_ref[...] = jnp.zeros_like(sstate_ref)
        gstate_ref[...] = jnp.zeros_like(gstate_ref)

    _attention(q_ref[...], k_ref[...], v_ref[...], kprev_ref, vprev_ref, sinks_ref, layer, t, mix_ref)

    sm = sm_ref[...]
    sp = _softplus(sm + bias_ref[...])
    lane = lax.broadcasted_iota(jnp.int32, (1, LANES), 1)
    is_decay = jnp.where(lane < SSD_HEADS, 1.0, jnp.where(lane >= SMALL_DECAY, jnp.where(lane < SMALL_DECAY + GDN_HEADS, 1.0, 0.0), 0.0))
    da = sp * (-jnp.exp(alog_ref[...]) * is_decay)
    cs = _mm_exact_lhs(_seg_cumsum_matrix(TB, TB), da)
    gc = _mm_exact_lhs(_seg_cumsum_matrix(TB, CHUNK), da)

    xbc = _silu(_causal_conv(xbc_ref[...], stail_ref, sconvw_ref) + sconvb_ref[...])
    _ssd(xbc, z_ref[...], sp, cs, cs.T, dvec_ref[...], snormw_ref[...], sstate_ref, mix_ref)

    gq = _silu(_causal_conv(gqkv_ref[...], gtail_ref, gconvw_ref))
    g_end = _mm_exact_lhs(_block_ones(TB, CHUNK), da)
    _gdn(gq, gz_ref[...], _sigmoid(sm), gc, gc.T, g_end, gnormw_ref[...], gstate_ref, mix_ref)

    mixed = jnp.dot(_bf(mix_ref[...]), wout_ref[...], preferred_element_type=F32)
    o_ref[...] = x_ref[...] + _rmsnorm(mixed, postw_ref[...])


def _mixer_call(projs, x3d, params, layer):
    bsz, seq, _ = x3d.shape
    grid = (bsz, seq // TB)

    def act_spec(width):
        return pl.BlockSpec((None, TB, width), lambda b, t: (b, t, 0))

    def par_spec(arr):
        shape = arr.shape[1:]
        zeros = (0,) * len(shape)
        return pl.BlockSpec((None,) + shape, lambda b, t: (layer,) + zeros)

    (sinks, wout, postw, sconvw, sconvb, gconvw, bias, alog, dvec, snormw, gnormw) = params
    vmem_params = (wout, postw, sconvw, sconvb, gconvw, bias, alog, dvec, snormw, gnormw)
    in_specs = ([pl.BlockSpec(memory_space=pltpu.SMEM)]
                + [act_spec(w) for w in PROJ_WIDTHS]
                + [act_spec(D_MODEL)]
                + [par_spec(p) for p in vmem_params])
    scratch = [
        pltpu.VMEM((SUBLANES, SSD_CONV_DIM), F32),
        pltpu.VMEM((SUBLANES, GDN_CONV_DIM), F32),
        pltpu.VMEM((WINDOW, ATT_KV_DIM), F32),
        pltpu.VMEM((WINDOW, ATT_KV_DIM), F32),
        pltpu.VMEM((SSD_GROUPS, SSD_STATE, SSD_HPG * SSD_HEAD_DIM), F32),
        pltpu.VMEM((GDN_HEADS, GDN_HEAD_K, GDN_HEAD_V), F32),
        pltpu.VMEM((TB, D_MODEL), F32),
    ]
    acts = [p.reshape(bsz, seq, p.shape[-1]) for p in projs]
    return pl.pallas_call(
        functools.partial(_mixer_kernel, layer),
        grid=grid,
        in_specs=in_specs,
        out_specs=act_spec(D_MODEL),
        out_shape=jax.ShapeDtypeStruct(x3d.shape, F32),
        scratch_shapes=scratch,
        compiler_params=pltpu.CompilerParams(
            dimension_semantics=("arbitrary", "arbitrary"), vmem_limit_bytes=VMEM_LIMIT),
        name="mixer",
    )(sinks, *acts, x3d, *vmem_params)


def _ffn_kernel(x_ref, prew_ref, postw_ref, wg_ref, wu_ref, wd_ref, o_ref):
    x = x_ref[...]
    h = _bf(_rmsnorm(x, prew_ref[...]))
    acc = jnp.zeros((TM_FFN, D_MODEL), F32)
    for c in range(FF // FF_CHUNK):
        c0, c1 = c * FF_CHUNK, (c + 1) * FF_CHUNK
        gate = jnp.dot(h, wg_ref[:, c0:c1], preferred_element_type=F32)
        up = jnp.dot(h, wu_ref[:, c0:c1], preferred_element_type=F32)
        acc = acc + jnp.dot(_bf(_silu(gate) * up), wd_ref[c0:c1, :], preferred_element_type=F32)
    o_ref[...] = x + _rmsnorm(acc, postw_ref[...])


def _ffn_call(x2d, pre_norm, post_norm, wg, wu, wd, layer):
    m = x2d.shape[0]

    def wspec(shape):
        return pl.BlockSpec((None,) + shape, lambda i: (layer, 0, 0), pipeline_mode=pl.Buffered(1))

    return pl.pallas_call(
        _ffn_kernel,
        grid=(m // TM_FFN,),
        in_specs=[
            pl.BlockSpec((TM_FFN, D_MODEL), lambda i: (i, 0)),
            pl.BlockSpec((None, 1, D_MODEL), lambda i: (layer, 0, 0)),
            pl.BlockSpec((None, 1, D_MODEL), lambda i: (layer, 0, 0)),
            wspec((D_MODEL, FF)),
            wspec((D_MODEL, FF)),
            wspec((FF, D_MODEL)),
        ],
        out_specs=pl.BlockSpec((TM_FFN, D_MODEL), lambda i: (i, 0)),
        out_shape=jax.ShapeDtypeStruct(x2d.shape, F32),
        compiler_params=pltpu.CompilerParams(
            dimension_semantics=("arbitrary",), vmem_limit_bytes=VMEM_LIMIT),
        name="ffn",
    )(x2d, pre_norm, post_norm, wg, wu, wd)


def _regroup_w_in(w_in):
    n_wide_a = ATT_DIM + 2 * ATT_KV_DIM + SSD_DIM + SSD_CONV_DIM
    dt0 = n_wide_a
    g0 = dt0 + SSD_HEADS
    b0 = g0 + GDN_CONV_DIM + GDN_DIM
    depth = w_in.shape[0]
    pad = jnp.zeros((depth, D_MODEL, LANES - SSD_HEADS - 2 * GDN_HEADS), w_in.dtype)
    return jnp.concatenate(
        [w_in[:, :, :n_wide_a], w_in[:, :, g0:b0], w_in[:, :, dt0:g0], w_in[:, :, b0:], pad],
        axis=-1).astype(BF16)


def _pack_small(depth, ssd_vals, gdn_vals):
    vec = jnp.zeros((depth, 1, LANES), F32)
    vec = vec.at[:, 0, SMALL_DT:SMALL_DT + SSD_HEADS].set(ssd_vals.astype(F32))
    vec = vec.at[:, 0, SMALL_DECAY:SMALL_DECAY + GDN_HEADS].set(gdn_vals.astype(F32))
    return vec


def kernel(x, pre_mix_norm, post_mix_norm, pre_ffn_norm, post_ffn_norm, w_in, w_out, attn_sinks, ssd_conv_w, ssd_conv_b, ssd_dt_bias, ssd_A_log, ssd_D, ssd_norm_w, gdn_conv_w, gdn_dt_bias, gdn_A_log, gdn_norm_w, ffn_w_gate, ffn_w_up, ffn_w_down):
    bsz, seq, _ = x.shape
    depth = w_in.shape[0]
    assert seq % TB == 0 and (bsz * seq) % TM_PROJ == 0 and (bsz * seq) % TM_FFN == 0

    w_in_r = _regroup_w_in(w_in)
    vec3 = lambda a: a.reshape(depth, 1, a.shape[-1]).astype(F32)
    params = (
        attn_sinks.reshape(-1).astype(F32),
        w_out.astype(BF16),
        vec3(post_mix_norm),
        ssd_conv_w.astype(F32),
        vec3(ssd_conv_b),
        gdn_conv_w.astype(F32),
        _pack_small(depth, ssd_dt_bias, gdn_dt_bias),
        _pack_small(depth, ssd_A_log, gdn_A_log),
        vec3(jnp.repeat(ssd_D, SSD_HEAD_DIM, axis=-1)),
        vec3(ssd_norm_w),
        vec3(jnp.tile(gdn_norm_w, (1, GDN_HEADS))),
    )
    pre_mix = vec3(pre_mix_norm)
    pre_ffn = vec3(pre_ffn_norm)
    post_ffn = vec3(post_ffn_norm)
    wg = ffn_w_gate.astype(BF16)
    wu = ffn_w_up.astype(BF16)
    wd = ffn_w_down.astype(BF16)

    for layer in range(depth):
        projs = _proj_call(x.reshape(bsz * seq, D_MODEL), pre_mix, w_in_r, layer)
        x = _mixer_call(projs, x, params, layer)
        x = _ffn_call(x.reshape(bsz * seq, D_MODEL), pre_ffn, post_ffn, wg, wu, wd, layer).reshape(bsz, seq, D_MODEL)
    return x
```

```python
import functools

import jax
import jax.numpy as jnp
from jax import lax
from jax.experimental import pallas as pl
from jax.experimental.pallas import tpu as pltpu

F32 = jnp.float32
BF16 = jnp.bfloat16

D_MODEL = 1024
CHUNK = 64
EPS = 1e-6
CONV_K = 4

SSD_HEADS = 8
SSD_HEAD_DIM = 64
SSD_DIM = SSD_HEADS * SSD_HEAD_DIM
SSD_GROUPS = 2
SSD_STATE = 128
SSD_HPG = SSD_HEADS // SSD_GROUPS
SSD_CONV_DIM = SSD_DIM + 2 * SSD_GROUPS * SSD_STATE

ATT_HEADS = 4
ATT_KV_HEADS = 2
ATT_HEAD_DIM = 64
ATT_DIM = ATT_HEADS * ATT_HEAD_DIM
ATT_KV_DIM = ATT_KV_HEADS * ATT_HEAD_DIM
WINDOW = 128
WIN_CHUNKS = WINDOW // CHUNK

GDN_HEADS = 4
GDN_HEAD_K = 64
GDN_HEAD_V = 64
GDN_KDIM = GDN_HEADS * GDN_HEAD_K
GDN_DIM = GDN_HEADS * GDN_HEAD_V
GDN_CONV_DIM = 2 * GDN_KDIM + GDN_DIM

FF = 2816
LANES = 128
SUBLANES = 8

PROJ_WIDTHS = (ATT_DIM, ATT_KV_DIM, ATT_KV_DIM, SSD_DIM, SSD_CONV_DIM, GDN_CONV_DIM, GDN_DIM)
PROJ_COLS = sum(PROJ_WIDTHS)
XBC_SLOT = 4
GQKV_SLOT = 5
SMALL_ROWS = SSD_HEADS + 2 * GDN_HEADS
SMALL_DT = 0
SMALL_BETA = SSD_HEADS
SMALL_DECAY = SSD_HEADS + GDN_HEADS

TM_PROJ = 512
TM_FFN = 512
FF_CHUNK = 256
TB = 256
ATT_QB = 128
GDN_SPAN = 128
VMEM_LIMIT = 56 * 1024 * 1024


def _bf(x):
    return x.astype(BF16)


def _mm(a, b):
    return jnp.dot(_bf(a), _bf(b), preferred_element_type=F32)


def _mm_nt(a, b):
    return lax.dot_general(_bf(a), _bf(b), (((1,), (1,)), ((), ())), preferred_element_type=F32)


def _mm_tn(a, b):
    return lax.dot_general(_bf(a), _bf(b), (((0,), (0,)), ((), ())), preferred_element_type=F32)


def _split(x, terms):
    out = []
    for _ in range(terms - 1):
        t = x.astype(BF16)
        out.append(t)
        x = x - t.astype(F32)
    out.append(x.astype(BF16))
    return out


def _cumsum_lanes(x_t, m_bf16):
    return sum(jnp.dot(t, m_bf16, preferred_element_type=F32) for t in _split(x_t, 3))


def _expand_heads(x_t, r_bf16):
    dn = (((0,), (0,)), ((), ()))
    return sum(lax.dot_general(t, r_bf16, dn, preferred_element_type=F32) for t in _split(x_t, 2))


def _columns(x_t):
    pad = jnp.zeros((LANES - x_t.shape[0], x_t.shape[1]), F32)
    return jnp.concatenate([x_t, pad], axis=0).T


def _sigmoid(x):
    return 1.0 / (1.0 + jnp.exp(-x))


def _silu(x):
    return x * _sigmoid(x)


def _softplus(x):
    return jnp.maximum(x, 0.0) + jnp.log1p(jnp.exp(-jnp.abs(x)))


def _rmsnorm(x, w):
    return x * lax.rsqrt(jnp.mean(x * x, axis=-1, keepdims=True) + EPS) * w


def _iota2(shape):
    return (lax.broadcasted_iota(jnp.int32, shape, 0), lax.broadcasted_iota(jnp.int32, shape, 1))


def _head_expand_matrix(src_off, n_heads, width):
    ri, ci = _iota2((SMALL_ROWS, n_heads * width))
    return jnp.where(ri - src_off == ci // width, 1.0, 0.0).astype(BF16)


def _block_ones(n, seg):
    ri, ci = _iota2((n, n))
    return jnp.where((ri // seg) == (ci // seg), 1.0, 0.0).astype(BF16)


def _causal_conv(x, tail_ref, w_ref):
    n = x.shape[0]
    tail = tail_ref[...]
    row = lax.broadcasted_iota(jnp.int32, (SUBLANES, x.shape[1]), 0)
    acc = None
    for j in reversed(range(CONV_K)):
        if j == 0:
            shifted = x
        else:
            xs = pltpu.roll(x, j, axis=0)
            first = jnp.where(row < j, pltpu.roll(tail, j, axis=0), xs[0:SUBLANES])
            shifted = jnp.concatenate([first, xs[SUBLANES:]], axis=0)
        term = shifted * w_ref[CONV_K - 1 - j:CONV_K - j, :]
        acc = term if acc is None else acc + term
    tail_ref[...] = x[n - SUBLANES:n]
    return acc


def _proj_kernel(tiles_per_seq, x_ref, nw_ref, w_ref, wsmall_ref, sconvw_ref, sconvb_ref, gconvw_ref,
                 *refs):
    out_refs = refs[:len(PROJ_WIDTHS)]
    small_ref, stail_ref, gtail_ref = refs[len(PROJ_WIDTHS):]

    @pl.when(pl.program_id(0) % tiles_per_seq == 0)
    def _():
        stail_ref[...] = jnp.zeros_like(stail_ref)
        gtail_ref[...] = jnp.zeros_like(gtail_ref)

    h = _bf(_rmsnorm(x_ref[...], nw_ref[...]))
    off = 0
    for slot, (ref, width) in enumerate(zip(out_refs, PROJ_WIDTHS)):
        y = jnp.dot(h, w_ref[:, off:off + width], preferred_element_type=F32)
        if slot == XBC_SLOT:
            y = _silu(_causal_conv(y, stail_ref, sconvw_ref) + sconvb_ref[...])
        elif slot == GQKV_SLOT:
            y = _silu(_causal_conv(y, gtail_ref, gconvw_ref))
        ref[...] = y
        off += width
    small_ref[...] = lax.dot_general(wsmall_ref[...], h, (((1,), (1,)), ((), ())),
                                     preferred_element_type=F32)


def _proj_call(x2d, seq, pre_norm, w_in_r, w_small_t, sconvw, sconvb, gconvw, layer):
    m = x2d.shape[0]
    grid = (m // TM_PROJ,)

    def par_spec(arr, **kw):
        shape = arr.shape[1:]
        return pl.BlockSpec((None,) + shape, lambda i: (layer,) + (0,) * len(shape), **kw)

    out_shape = (tuple(jax.ShapeDtypeStruct((m, w), F32) for w in PROJ_WIDTHS)
                 + (jax.ShapeDtypeStruct((SMALL_ROWS, m), F32),))
    out_specs = (tuple(pl.BlockSpec((TM_PROJ, w), lambda i: (i, 0)) for w in PROJ_WIDTHS)
                 + (pl.BlockSpec((SMALL_ROWS, TM_PROJ), lambda i: (0, i)),))
    return pl.pallas_call(
        functools.partial(_proj_kernel, seq // TM_PROJ),
        grid=grid,
        in_specs=[
            pl.BlockSpec((TM_PROJ, D_MODEL), lambda i: (i, 0)),
            par_spec(pre_norm),
            par_spec(w_in_r, pipeline_mode=pl.Buffered(1)),
            par_spec(w_small_t),
            par_spec(sconvw),
            par_spec(sconvb),
            par_spec(gconvw),
        ],
        out_specs=out_specs,
        out_shape=out_shape,
        scratch_shapes=[pltpu.VMEM((SUBLANES, SSD_CONV_DIM), F32),
                        pltpu.VMEM((SUBLANES, GDN_CONV_DIM), F32)],
        compiler_params=pltpu.CompilerParams(
            dimension_semantics=("arbitrary",), vmem_limit_bytes=VMEM_LIMIT),
        name="proj",
    )(x2d, pre_norm, w_in_r, w_small_t, sconvw, sconvb, gconvw)


def _attention(q, k, v, kprev_ref, vprev_ref, sinks_ref, layer, t, mix_ref):
    nk = ATT_QB + WINDOW
    kext = jnp.concatenate([kprev_ref[...], k], axis=0)
    vext = jnp.concatenate([vprev_ref[...], v], axis=0)
    kprev_ref[...] = k[TB - WINDOW:TB]
    vprev_ref[...] = v[TB - WINDOW:TB]
    qi, kj = _iota2((ATT_QB, nk))
    dist = jnp.abs(qi + WINDOW - kj).astype(F32)
    qc = qi // CHUNK
    kc = kj // CHUNK - WIN_CHUNKS
    in_band = jnp.where(kc <= qc, jnp.where(kc >= qc - WIN_CHUNKS, 1, 0), 0)
    grp = ATT_HEADS // ATT_KV_HEADS
    for a in range(TB // ATT_QB):
        first_chunk = t * (TB // CHUNK) + a * (ATT_QB // CHUNK)
        valid = (in_band * jnp.where(kc + first_chunk >= 0, 1, 0)) > 0
        rows = slice(a * ATT_QB, (a + 1) * ATT_QB)
        krows = slice(a * ATT_QB, a * ATT_QB + nk)
        for kh in range(ATT_KV_HEADS):
            kk = kext[krows, kh * ATT_HEAD_DIM:(kh + 1) * ATT_HEAD_DIM]
            vv = vext[krows, kh * ATT_HEAD_DIM:(kh + 1) * ATT_HEAD_DIM]
            for gi in range(grp):
                h = kh * grp + gi
                slope = 2.0 ** (-8.0 * (h + 1) / ATT_HEADS)
                sink = sinks_ref[layer * ATT_HEADS + h]
                qh = q[rows, h * ATT_HEAD_DIM:(h + 1) * ATT_HEAD_DIM]
                s = _mm_nt(qh, kk) * (ATT_HEAD_DIM ** -0.5) - slope * dist
                s = jnp.where(valid, s, -1e30)
                m = jnp.maximum(jnp.max(s, axis=-1, keepdims=True), sink)
                p = jnp.exp(s - m)
                den = jnp.sum(p, axis=-1, keepdims=True) + jnp.exp(sink - m)
                mix_ref[rows, h * ATT_HEAD_DIM:(h + 1) * ATT_HEAD_DIM] = _mm(p, vv) / den


def _ssd(xbc, z, sp_t, cs_t, dvec, norm_w, state_ref, mix_ref):
    xs = xbc[:, :SSD_DIM]
    bm = xbc[:, SSD_DIM:SSD_DIM + SSD_GROUPS * SSD_STATE]
    cm = xbc[:, SSD_DIM + SSD_GROUPS * SSD_STATE:]
    expand = _head_expand_matrix(SMALL_DT, SSD_HEADS, SSD_HEAD_DIM)
    cs = _columns(cs_t)
    cs_last_t = cs_t[:, TB - 1:TB]
    xc = xs * _expand_heads(sp_t, expand)
    xcd = xc * _expand_heads(jnp.exp(cs_last_t - cs_t), expand)
    ecs = _expand_heads(jnp.exp(cs_t), expand)
    elast = _expand_heads(jnp.broadcast_to(jnp.exp(cs_last_t), (SMALL_ROWS, SMALL_ROWS)), expand)[0:1]
    ri, ci = _iota2((TB, TB))
    tri = ci <= ri
    gw = SSD_HPG * SSD_HEAD_DIM
    ys = []
    for g in range(SSD_GROUPS):
        bg = bm[:, g * SSD_STATE:(g + 1) * SSD_STATE]
        cg = cm[:, g * SSD_STATE:(g + 1) * SSD_STATE]
        cb = _mm_nt(cg, bg)
        prev = state_ref[g]
        y_off = _mm(cg, prev) * ecs[:, g * gw:(g + 1) * gw]
        state_ref[g] = prev * elast[:, g * gw:(g + 1) * gw] + _mm_tn(bg, xcd[:, g * gw:(g + 1) * gw])
        yd = []
        for hh in range(SSD_HPG):
            h = g * SSD_HPG + hh
            diff = cs[:, h:h + 1] - cs_t[h:h + 1, :]
            lmat = jnp.where(tri, jnp.exp(jnp.where(tri, diff, 0.0)), 0.0)
            yd.append(_mm(cb * lmat, xc[:, h * SSD_HEAD_DIM:(h + 1) * SSD_HEAD_DIM]))
        ys.append(jnp.concatenate(yd, axis=1) + y_off)
    y = jnp.concatenate(ys, axis=1) + xs * dvec
    gated = y * _silu(z)
    for g in range(SSD_GROUPS):
        gg = gated[:, g * gw:(g + 1) * gw]
        gg = gg * lax.rsqrt(jnp.mean(gg * gg, axis=-1, keepdims=True) + EPS)
        mix_ref[:, ATT_DIM + g * gw:ATT_DIM + (g + 1) * gw] = gg * norm_w[:, g * gw:(g + 1) * gw]


def _gdn(gq, gz, beta_t, gc_t, g_end_t, norm_w, state_ref, mix_ref):
    q = gq[:, :GDN_KDIM]
    k = gq[:, GDN_KDIM:2 * GDN_KDIM]
    v = gq[:, 2 * GDN_KDIM:]
    ones = _block_ones(GDN_KDIM, GDN_HEAD_K)

    def seg_sum(x2):
        return jnp.dot(_bf(x2), ones, preferred_element_type=F32)

    qn = q * lax.rsqrt(seg_sum(q * q) + EPS) * (GDN_HEAD_K ** -0.5)
    kn = k * lax.rsqrt(seg_sum(k * k) + EPS)
    exp_beta = _head_expand_matrix(SMALL_BETA, GDN_HEADS, GDN_HEAD_K)
    exp_dec = _head_expand_matrix(SMALL_DECAY, GDN_HEADS, GDN_HEAD_K)
    beta_b = _expand_heads(beta_t, exp_beta)
    eg_b = _expand_heads(jnp.exp(gc_t), exp_dec)
    kbeta = kn * beta_b
    vbeta = v * beta_b
    kbg = kbeta * eg_b
    qeg = qn * eg_b
    kd = kn * _expand_heads(jnp.exp(g_end_t - gc_t), exp_dec)
    gc = _columns(gc_t)
    ri, ci = _iota2((GDN_SPAN, GDN_SPAN))
    same = (ri // CHUNK) == (ci // CHUNK)
    tri = jnp.logical_and(same, ci <= ri)
    strict = jnp.logical_and(same, ci < ri)
    heads = range(GDN_HEADS)
    n_chunks = TB // CHUNK
    hs = [slice(h * GDN_HEAD_K, (h + 1) * GDN_HEAD_K) for h in heads]
    rs = [slice(c * CHUNK, (c + 1) * CHUNK) for c in range(n_chunks)]
    spans = [slice(a * GDN_SPAN, (a + 1) * GDN_SPAN) for a in range(TB // GDN_SPAN)]
    units = [(sp, h) for sp in spans for h in heads]

    p, sol, qk = [], [], []
    for sp, h in units:
        row = SMALL_DECAY + h
        diff = gc[sp, row:row + 1] - gc_t[row:row + 1, sp]
        decay = jnp.where(tri, jnp.exp(jnp.where(tri, diff, 0.0)), 0.0)
        p.append(jnp.where(strict, _mm_nt(kbeta[sp, hs[h]], kn[sp, hs[h]]) * (-decay), 0.0))
        qk.append(_mm_nt(qn[sp, hs[h]], kn[sp, hs[h]]) * decay)
        sol.append(jnp.concatenate([vbeta[sp, hs[h]], kbg[sp, hs[h]]], axis=1))
    steps = CHUNK.bit_length() - 1
    for j in range(steps):
        pb = [_bf(x) for x in p]
        sb = [_bf(x) for x in sol]
        if j + 1 < steps:
            res = [jnp.dot(pb[i], jnp.concatenate([pb[i], sb[i]], axis=1), preferred_element_type=F32)
                   for i in range(len(units))]
            p = [r[:, :GDN_SPAN] for r in res]
            sol = [sol[i] + res[i][:, GDN_SPAN:] for i in range(len(units))]
        else:
            sol = [sol[i] + jnp.dot(pb[i], sb[i], preferred_element_type=F32) for i in range(len(units))]
    qks_u = [_mm(qk[i], sol[i]) for i in range(len(units))]
    n_sp = len(spans)
    sol = [jnp.concatenate([sol[a * GDN_HEADS + h] for a in range(n_sp)], axis=0) for h in heads]
    qks = [jnp.concatenate([qks_u[a * GDN_HEADS + h] for a in range(n_sp)], axis=0) for h in heads]
    qeff = [qeg[:, hs[h]] - qks[h][:, GDN_HEAD_V:] for h in heads]
    kuw = [[_mm_tn(kd[rs[c], hs[h]], sol[h][rs[c]]) for h in heads] for c in range(n_chunks)]
    state = [state_ref[h] for h in heads]
    states = []
    for c in range(n_chunks):
        states.append(state)
        e_end = jnp.exp(gc_t[:, (c + 1) * CHUNK - 1:(c + 1) * CHUNK])
        state = [state[h] * e_end[SMALL_DECAY + h:SMALL_DECAY + h + 1, :]
                 - _mm(kuw[c][h][:, GDN_HEAD_V:], state[h]) + kuw[c][h][:, :GDN_HEAD_V]
                 for h in heads]
    for h in heads:
        state_ref[h] = state[h]
    o = jnp.concatenate(
        [jnp.concatenate([_mm(qeff[h][rs[c]], states[c][h]) + qks[h][rs[c], :GDN_HEAD_V] for h in heads], axis=1)
         for c in range(n_chunks)], axis=0)
    ms = seg_sum(o * o) * (1.0 / GDN_HEAD_V)
    mix_ref[:, ATT_DIM + SSD_DIM:] = o * lax.rsqrt(ms + EPS) * norm_w * _silu(gz)


def _mixer_kernel(layer,
                  sinks_ref, q_ref, k_ref, v_ref, z_ref, xbc_ref, gqkv_ref, gz_ref, small_ref, x_ref,
                  wout_ref, postw_ref, bias_ref, alog_ref, dvec_ref, snormw_ref, gnormw_ref,
                  o_ref,
                  kprev_ref, vprev_ref, sstate_ref, gstate_ref, mix_ref):
    t = pl.program_id(1)

    @pl.when(t == 0)
    def _():
        kprev_ref[...] = jnp.zeros_like(kprev_ref)
        vprev_ref[...] = jnp.zeros_like(vprev_ref)
        sstate_ref[...] = jnp.zeros_like(sstate_ref)
        gstate_ref[...] = jnp.zeros_like(gstate_ref)

    _attention(q_ref[...], k_ref[...], v_ref[...], kprev_ref, vprev_ref, sinks_ref, layer, t, mix_ref)

    small_t = small_ref[...]
    sp_t = _softplus(small_t + bias_ref[...])
    row = lax.broadcasted_iota(jnp.int32, (SMALL_ROWS, 1), 0)
    is_decay = jnp.where(row < SMALL_BETA, 1.0, jnp.where(row >= SMALL_DECAY, 1.0, 0.0))
    da_t = sp_t * (-jnp.exp(alog_ref[...]) * is_decay)
    ri, ci = _iota2((TB, TB))
    same = (ri // CHUNK) == (ci // CHUNK)
    before = jnp.where(ri <= ci, 1.0, 0.0)
    cs_t = _cumsum_lanes(da_t, before.astype(BF16))
    gc_t = _cumsum_lanes(da_t, jnp.where(same, before, 0.0).astype(BF16))
    g_end_t = _cumsum_lanes(da_t, jnp.where(same, 1.0, 0.0).astype(BF16))

    _ssd(xbc_ref[...], z_ref[...], sp_t, cs_t, dvec_ref[...], snormw_ref[...], sstate_ref, mix_ref)
    _gdn(gqkv_ref[...], gz_ref[...], _sigmoid(small_t), gc_t, g_end_t, gnormw_ref[...], gstate_ref, mix_ref)

    mixed = jnp.dot(_bf(mix_ref[...]), wout_ref[...], preferred_element_type=F32)
    o_ref[...] = x_ref[...] + _rmsnorm(mixed, postw_ref[...])


def _mixer_call(projs, small_t, x3d, params, layer):
    bsz, seq, _ = x3d.shape
    n_blk = seq // TB
    grid = (bsz, n_blk)

    def act_spec(width):
        return pl.BlockSpec((None, TB, width), lambda b, t: (b, t, 0))

    def par_spec(arr):
        shape = arr.shape[1:]
        return pl.BlockSpec((None,) + shape, lambda b, t: (layer,) + (0,) * len(shape))

    sinks, vmem_params = params[0], params[1:]
    in_specs = ([pl.BlockSpec(memory_space=pltpu.SMEM)]
                + [act_spec(w) for w in PROJ_WIDTHS]
                + [pl.BlockSpec((SMALL_ROWS, TB), lambda b, t: (0, b * n_blk + t))]
                + [act_spec(D_MODEL)]
                + [par_spec(p) for p in vmem_params])
    scratch = [
        pltpu.VMEM((WINDOW, ATT_KV_DIM), F32),
        pltpu.VMEM((WINDOW, ATT_KV_DIM), F32),
        pltpu.VMEM((SSD_GROUPS, SSD_STATE, SSD_HPG * SSD_HEAD_DIM), F32),
        pltpu.VMEM((GDN_HEADS, GDN_HEAD_K, GDN_HEAD_V), F32),
        pltpu.VMEM((TB, D_MODEL), F32),
    ]
    acts = [p.reshape(bsz, seq, p.shape[-1]) for p in projs]
    return pl.pallas_call(
        functools.partial(_mixer_kernel, layer),
        grid=grid,
        in_specs=in_specs,
        out_specs=act_spec(D_MODEL),
        out_shape=jax.ShapeDtypeStruct(x3d.shape, F32),
        scratch_shapes=scratch,
        compiler_params=pltpu.CompilerParams(
            dimension_semantics=("arbitrary", "arbitrary"), vmem_limit_bytes=VMEM_LIMIT),
        name="mixer",
    )(sinks, *acts, small_t, x3d, *vmem_params)


def _ffn_kernel(x_ref, prew_ref, postw_ref, wg_ref, wu_ref, wd_ref, o_ref):
    x = x_ref[...]
    h = _bf(_rmsnorm(x, prew_ref[...]))
    acc = jnp.zeros((TM_FFN, D_MODEL), F32)
    for c in range(FF // FF_CHUNK):
        c0, c1 = c * FF_CHUNK, (c + 1) * FF_CHUNK
        gate = jnp.dot(h, wg_ref[:, c0:c1], preferred_element_type=F32)
        up = jnp.dot(h, wu_ref[:, c0:c1], preferred_element_type=F32)
        acc = acc + jnp.dot(_bf(_silu(gate) * up), wd_ref[c0:c1, :], preferred_element_type=F32)
    o_ref[...] = x + _rmsnorm(acc, postw_ref[...])


def _ffn_call(x2d, pre_norm, post_norm, wg, wu, wd, layer):
    m = x2d.shape[0]

    def wspec(shape):
        return pl.BlockSpec((None,) + shape, lambda i: (layer, 0, 0), pipeline_mode=pl.Buffered(1))

    return pl.pallas_call(
        _ffn_kernel,
        grid=(m // TM_FFN,),
        in_specs=[
            pl.BlockSpec((TM_FFN, D_MODEL), lambda i: (i, 0)),
            pl.BlockSpec((None, 1, D_MODEL), lambda i: (layer, 0, 0)),
            pl.BlockSpec((None, 1, D_MODEL), lambda i: (layer, 0, 0)),
            wspec((D_MODEL, FF)),
            wspec((D_MODEL, FF)),
            wspec((FF, D_MODEL)),
        ],
        out_specs=pl.BlockSpec((TM_FFN, D_MODEL), lambda i: (i, 0)),
        out_shape=jax.ShapeDtypeStruct(x2d.shape, F32),
        compiler_params=pltpu.CompilerParams(
            dimension_semantics=("arbitrary",), vmem_limit_bytes=VMEM_LIMIT),
        name="ffn",
    )(x2d, pre_norm, post_norm, wg, wu, wd)


def _regroup_w_in(w_in):
    n_wide_a = ATT_DIM + 2 * ATT_KV_DIM + SSD_DIM + SSD_CONV_DIM
    dt0 = n_wide_a
    g0 = dt0 + SSD_HEADS
    b0 = g0 + GDN_CONV_DIM + GDN_DIM
    wide = jnp.concatenate([w_in[:, :, :n_wide_a], w_in[:, :, g0:b0]], axis=-1).astype(BF16)
    small = jnp.concatenate([w_in[:, :, dt0:g0], w_in[:, :, b0:]], axis=-1)
    return wide, jnp.swapaxes(small, 1, 2).astype(BF16)


def _pack_small(ssd_vals, gdn_vals):
    depth = ssd_vals.shape[0]
    mid = jnp.zeros((depth, GDN_HEADS), F32)
    return jnp.concatenate([ssd_vals.astype(F32), mid, gdn_vals.astype(F32)], axis=-1)[:, :, None]


def kernel(x, pre_mix_norm, post_mix_norm, pre_ffn_norm, post_ffn_norm, w_in, w_out, attn_sinks, ssd_conv_w, ssd_conv_b, ssd_dt_bias, ssd_A_log, ssd_D, ssd_norm_w, gdn_conv_w, gdn_dt_bias, gdn_A_log, gdn_norm_w, ffn_w_gate, ffn_w_up, ffn_w_down):
    bsz, seq, _ = x.shape
    depth = w_in.shape[0]
    assert seq % TB == 0 and seq % TM_PROJ == 0 and (bsz * seq) % TM_FFN == 0

    w_in_r, w_small_t = _regroup_w_in(w_in)
    vec3 = lambda a: a.reshape(depth, 1, a.shape[-1]).astype(F32)
    params = (
        attn_sinks.reshape(-1).astype(F32),
        w_out.astype(BF16),
        vec3(post_mix_norm),
        _pack_small(ssd_dt_bias, gdn_dt_bias),
        _pack_small(ssd_A_log, gdn_A_log),
        vec3(jnp.repeat(ssd_D, SSD_HEAD_DIM, axis=-1)),
        vec3(ssd_norm_w),
        vec3(jnp.tile(gdn_norm_w, (1, GDN_HEADS))),
    )
    pre_mix = vec3(pre_mix_norm)
    pre_ffn = vec3(pre_ffn_norm)
    post_ffn = vec3(post_ffn_norm)
    sconvw = ssd_conv_w.astype(F32)
    sconvb = vec3(ssd_conv_b)
    gconvw = gdn_conv_w.astype(F32)
    wg = ffn_w_gate.astype(BF16)
    wu = ffn_w_up.astype(BF16)
    wd = ffn_w_down.astype(BF16)

    for layer in range(depth):
        outs = _proj_call(x.reshape(bsz * seq, D_MODEL), seq, pre_mix, w_in_r, w_small_t,
                          sconvw, sconvb, gconvw, layer)
        x = _mixer_call(outs[:-1], outs[-1], x, params, layer)
        x = _ffn_call(x.reshape(bsz * seq, D_MODEL), pre_ffn, post_ffn, wg, wu, wd, layer).reshape(bsz, seq, D_MODEL)
    return x
```

```python
import functools

import jax
import jax.numpy as jnp
from jax import lax
from jax.experimental import pallas as pl
from jax.experimental.pallas import tpu as pltpu

F32 = jnp.float32
BF16 = jnp.bfloat16

D_MODEL = 1024
CHUNK = 64
EPS = 1e-6
CONV_K = 4

SSD_HEADS = 8
SSD_HEAD_DIM = 64
SSD_DIM = SSD_HEADS * SSD_HEAD_DIM
SSD_GROUPS = 2
SSD_STATE = 128
SSD_HPG = SSD_HEADS // SSD_GROUPS
SSD_CONV_DIM = SSD_DIM + 2 * SSD_GROUPS * SSD_STATE

ATT_HEADS = 4
ATT_KV_HEADS = 2
ATT_HEAD_DIM = 64
ATT_DIM = ATT_HEADS * ATT_HEAD_DIM
ATT_KV_DIM = ATT_KV_HEADS * ATT_HEAD_DIM
WINDOW = 128
WIN_CHUNKS = WINDOW // CHUNK

GDN_HEADS = 4
GDN_HEAD_K = 64
GDN_HEAD_V = 64
GDN_KDIM = GDN_HEADS * GDN_HEAD_K
GDN_DIM = GDN_HEADS * GDN_HEAD_V
GDN_CONV_DIM = 2 * GDN_KDIM + GDN_DIM

FF = 2816
LANES = 128
SUBLANES = 8

PROJ_WIDTHS = (ATT_DIM, ATT_KV_DIM, ATT_KV_DIM, SSD_DIM, SSD_CONV_DIM, GDN_CONV_DIM, GDN_DIM)
PROJ_COLS = sum(PROJ_WIDTHS)
XBC_SLOT = 4
GQKV_SLOT = 5
SMALL_ROWS = SSD_HEADS + 2 * GDN_HEADS
SMALL_DT = 0
SMALL_BETA = SSD_HEADS
SMALL_DECAY = SSD_HEADS + GDN_HEADS

TM_PROJ = 512
TM_FFN = 512
FF_CHUNK = 256
TB = 256
NB = 2
ATT_QB = 128
GDN_SPAN = 128
VMEM_LIMIT = 56 * 1024 * 1024


def _bf(x):
    return x.astype(BF16)


def _mm(a, b):
    return jnp.dot(_bf(a), _bf(b), preferred_element_type=F32)


def _mm_nt(a, b):
    return lax.dot_general(_bf(a), _bf(b), (((1,), (1,)), ((), ())), preferred_element_type=F32)


def _mm_tn(a, b):
    return lax.dot_general(_bf(a), _bf(b), (((0,), (0,)), ((), ())), preferred_element_type=F32)


def _split(x, terms):
    out = []
    for _ in range(terms - 1):
        t = x.astype(BF16)
        out.append(t)
        x = x - t.astype(F32)
    out.append(x.astype(BF16))
    return out


def _cumsum_lanes(x_t, m_bf16):
    return sum(jnp.dot(t, m_bf16, preferred_element_type=F32) for t in _split(x_t, 3))


def _expand_heads(x_t, r_bf16):
    terms = jnp.concatenate(_split(x_t, 2), axis=0)
    return lax.dot_general(terms, jnp.concatenate([r_bf16, r_bf16], axis=0), (((0,), (0,)), ((), ())),
                           preferred_element_type=F32)


def _columns(x_t):
    pad = jnp.zeros((LANES - x_t.shape[0], x_t.shape[1]), F32)
    return jnp.concatenate([x_t, pad], axis=0).T


def _sigmoid(x):
    return 1.0 / (1.0 + jnp.exp(-x))


def _silu(x):
    return x * _sigmoid(x)


def _softplus(x):
    return jnp.maximum(x, 0.0) + jnp.log1p(jnp.exp(-jnp.abs(x)))


def _rmsnorm(x, w):
    return x * lax.rsqrt(jnp.mean(x * x, axis=-1, keepdims=True) + EPS) * w


def _iota2(shape):
    return (lax.broadcasted_iota(jnp.int32, shape, 0), lax.broadcasted_iota(jnp.int32, shape, 1))


def _head_expand_matrix(src_off, n_heads, width):
    ri, ci = _iota2((SMALL_ROWS, n_heads * width))
    return jnp.where(ri - src_off == ci // width, 1.0, 0.0).astype(BF16)


def _block_ones(n, seg):
    ri, ci = _iota2((n, n))
    return jnp.where((ri // seg) == (ci // seg), 1.0, 0.0).astype(BF16)


def _causal_conv(x, tail_ref, w_ref):
    n = x.shape[0]
    tail = tail_ref[...]
    row = lax.broadcasted_iota(jnp.int32, (SUBLANES, x.shape[1]), 0)
    acc = None
    for j in reversed(range(CONV_K)):
        if j == 0:
            shifted = x
        else:
            xs = pltpu.roll(x, j, axis=0)
            first = jnp.where(row < j, pltpu.roll(tail, j, axis=0), xs[0:SUBLANES])
            shifted = jnp.concatenate([first, xs[SUBLANES:]], axis=0)
        term = shifted * w_ref[CONV_K - 1 - j:CONV_K - j, :]
        acc = term if acc is None else acc + term
    tail_ref[...] = x[n - SUBLANES:n]
    return acc


def _proj_kernel(tiles_per_seq, x_ref, nw_ref, w_ref, wsmall_ref, sconvw_ref, sconvb_ref, gconvw_ref,
                 *refs):
    out_refs = refs[:len(PROJ_WIDTHS)]
    small_ref, stail_ref, gtail_ref = refs[len(PROJ_WIDTHS):]

    @pl.when(pl.program_id(0) % tiles_per_seq == 0)
    def _():
        stail_ref[...] = jnp.zeros_like(stail_ref)
        gtail_ref[...] = jnp.zeros_like(gtail_ref)

    h = _bf(_rmsnorm(x_ref[...], nw_ref[...]))
    off = 0
    for slot, (ref, width) in enumerate(zip(out_refs, PROJ_WIDTHS)):
        y = jnp.dot(h, w_ref[:, off:off + width], preferred_element_type=F32)
        if slot == XBC_SLOT:
            y = _silu(_causal_conv(y, stail_ref, sconvw_ref) + sconvb_ref[...])
        elif slot == GQKV_SLOT:
            y = _silu(_causal_conv(y, gtail_ref, gconvw_ref))
        ref[...] = y
        off += width
    small_ref[...] = lax.dot_general(wsmall_ref[...], h, (((1,), (1,)), ((), ())),
                                     preferred_element_type=F32)


def _proj_call(x2d, seq, pre_norm, w_in_r, w_small_t, sconvw, sconvb, gconvw, layer):
    m = x2d.shape[0]
    grid = (m // TM_PROJ,)
    tiles_per_seq = seq // TM_PROJ

    def par_spec(arr, **kw):
        shape = arr.shape[1:]
        return pl.BlockSpec((None,) + shape, lambda i: (layer,) + (0,) * len(shape), **kw)

    out_shape = (tuple(jax.ShapeDtypeStruct((m, w), F32) for w in PROJ_WIDTHS)
                 + (jax.ShapeDtypeStruct((m // seq, SMALL_ROWS, seq), F32),))
    out_specs = (tuple(pl.BlockSpec((TM_PROJ, w), lambda i: (i, 0)) for w in PROJ_WIDTHS)
                 + (pl.BlockSpec((None, SMALL_ROWS, TM_PROJ),
                                 lambda i: (i // tiles_per_seq, 0, i % tiles_per_seq)),))
    return pl.pallas_call(
        functools.partial(_proj_kernel, tiles_per_seq),
        grid=grid,
        in_specs=[
            pl.BlockSpec((TM_PROJ, D_MODEL), lambda i: (i, 0)),
            par_spec(pre_norm),
            par_spec(w_in_r, pipeline_mode=pl.Buffered(1)),
            par_spec(w_small_t),
            par_spec(sconvw),
            par_spec(sconvb),
            par_spec(gconvw),
        ],
        out_specs=out_specs,
        out_shape=out_shape,
        scratch_shapes=[pltpu.VMEM((SUBLANES, SSD_CONV_DIM), F32),
                        pltpu.VMEM((SUBLANES, GDN_CONV_DIM), F32)],
        compiler_params=pltpu.CompilerParams(
            dimension_semantics=("arbitrary",), vmem_limit_bytes=VMEM_LIMIT),
        name="proj",
    )(x2d, pre_norm, w_in_r, w_small_t, sconvw, sconvb, gconvw)


def _attention(q, k, v, b, kprev_ref, vprev_ref, sinks_ref, layer, t, mix_ref):
    nk = ATT_QB + WINDOW
    kext = jnp.concatenate([kprev_ref[b], k], axis=0)
    vext = jnp.concatenate([vprev_ref[b], v], axis=0)
    kprev_ref[b] = k[TB - WINDOW:TB]
    vprev_ref[b] = v[TB - WINDOW:TB]
    qi, kj = _iota2((ATT_QB, nk))
    dist = jnp.abs(qi + WINDOW - kj).astype(F32)
    qc = qi // CHUNK
    kc = kj // CHUNK - WIN_CHUNKS
    in_band = jnp.where(kc <= qc, jnp.where(kc >= qc - WIN_CHUNKS, 1, 0), 0)
    grp = ATT_HEADS // ATT_KV_HEADS
    for a in range(TB // ATT_QB):
        first_chunk = t * (TB // CHUNK) + a * (ATT_QB // CHUNK)
        valid = (in_band * jnp.where(kc + first_chunk >= 0, 1, 0)) > 0
        rows = slice(a * ATT_QB, (a + 1) * ATT_QB)
        out_rows = slice(b * TB + a * ATT_QB, b * TB + (a + 1) * ATT_QB)
        krows = slice(a * ATT_QB, a * ATT_QB + nk)
        for kh in range(ATT_KV_HEADS):
            kk = kext[krows, kh * ATT_HEAD_DIM:(kh + 1) * ATT_HEAD_DIM]
            vv = vext[krows, kh * ATT_HEAD_DIM:(kh + 1) * ATT_HEAD_DIM]
            for gi in range(grp):
                h = kh * grp + gi
                slope = 2.0 ** (-8.0 * (h + 1) / ATT_HEADS)
                sink = sinks_ref[layer * ATT_HEADS + h]
                qh = q[rows, h * ATT_HEAD_DIM:(h + 1) * ATT_HEAD_DIM]
                s = _mm_nt(qh, kk) * (ATT_HEAD_DIM ** -0.5) - slope * dist
                s = jnp.where(valid, s, -1e30)
                m = jnp.maximum(jnp.max(s, axis=-1, keepdims=True), sink)
                p = jnp.exp(s - m)
                den = jnp.sum(p, axis=-1, keepdims=True) + jnp.exp(sink - m)
                mix_ref[out_rows, h * ATT_HEAD_DIM:(h + 1) * ATT_HEAD_DIM] = _mm(p, vv) / den


def _ssd(xbc, z, sp_t, cs_t, dvec, norm_w, b, state_ref, mix_ref):
    xs = xbc[:, :SSD_DIM]
    bm = xbc[:, SSD_DIM:SSD_DIM + SSD_GROUPS * SSD_STATE]
    cm = xbc[:, SSD_DIM + SSD_GROUPS * SSD_STATE:]
    expand = _head_expand_matrix(SMALL_DT, SSD_HEADS, SSD_HEAD_DIM)
    cs = _columns(cs_t)
    cs_last_t = cs_t[:, TB - 1:TB]
    xc = xs * _expand_heads(sp_t, expand)
    xcd = xc * _expand_heads(jnp.exp(cs_last_t - cs_t), expand)
    ecs = _expand_heads(jnp.exp(cs_t), expand)
    elast = _expand_heads(jnp.broadcast_to(jnp.exp(cs_last_t), (SMALL_ROWS, SMALL_ROWS)), expand)[0:1]
    ri, ci = _iota2((TB, TB))
    tri = ci <= ri
    gw = SSD_HPG * SSD_HEAD_DIM
    ys = []
    for g in range(SSD_GROUPS):
        bg = bm[:, g * SSD_STATE:(g + 1) * SSD_STATE]
        cg = cm[:, g * SSD_STATE:(g + 1) * SSD_STATE]
        cb = _mm_nt(cg, bg)
        prev = state_ref[b, g]
        y_off = _mm(cg, prev) * ecs[:, g * gw:(g + 1) * gw]
        state_ref[b, g] = prev * elast[:, g * gw:(g + 1) * gw] + _mm_tn(bg, xcd[:, g * gw:(g + 1) * gw])
        yd = []
        for hh in range(SSD_HPG):
            h = g * SSD_HPG + hh
            diff = cs[:, h:h + 1] - cs_t[h:h + 1, :]
            lmat = jnp.where(tri, jnp.exp(jnp.where(tri, diff, 0.0)), 0.0)
            yd.append(_mm(cb * lmat, xc[:, h * SSD_HEAD_DIM:(h + 1) * SSD_HEAD_DIM]))
        ys.append(jnp.concatenate(yd, axis=1) + y_off)
    y = jnp.concatenate(ys, axis=1) + xs * dvec
    gated = y * _silu(z)
    rows = slice(b * TB, (b + 1) * TB)
    for g in range(SSD_GROUPS):
        gg = gated[:, g * gw:(g + 1) * gw]
        gg = gg * lax.rsqrt(jnp.mean(gg * gg, axis=-1, keepdims=True) + EPS)
        mix_ref[rows, ATT_DIM + g * gw:ATT_DIM + (g + 1) * gw] = gg * norm_w[:, g * gw:(g + 1) * gw]


def _gdn(per_b, norm_w, state_ref, mix_ref):
    nb = len(per_b)
    ones = _block_ones(GDN_KDIM, GDN_HEAD_K)

    def seg_sum(x2):
        return jnp.dot(_bf(x2), ones, preferred_element_type=F32)

    exp_beta = _head_expand_matrix(SMALL_BETA, GDN_HEADS, GDN_HEAD_K)
    exp_dec = _head_expand_matrix(SMALL_DECAY, GDN_HEADS, GDN_HEAD_K)
    ri, ci = _iota2((GDN_SPAN, GDN_SPAN))
    same = (ri // CHUNK) == (ci // CHUNK)
    tri = jnp.logical_and(same, ci <= ri)
    strict = jnp.logical_and(same, ci < ri)
    heads = range(GDN_HEADS)
    n_chunks = TB // CHUNK
    n_sp = TB // GDN_SPAN
    hs = [slice(h * GDN_HEAD_K, (h + 1) * GDN_HEAD_K) for h in heads]
    rs = [slice(c * CHUNK, (c + 1) * CHUNK) for c in range(n_chunks)]
    spans = [slice(a * GDN_SPAN, (a + 1) * GDN_SPAN) for a in range(n_sp)]
    bh = [(b, h) for b in range(nb) for h in heads]

    p, sol, qk, qeg, kd = {}, {}, {}, [], []
    for b, (gq, _, beta_t, gc_t, g_end_t) in enumerate(per_b):
        q = gq[:, :GDN_KDIM]
        k = gq[:, GDN_KDIM:2 * GDN_KDIM]
        v = gq[:, 2 * GDN_KDIM:]
        qn = q * lax.rsqrt(seg_sum(q * q) + EPS) * (GDN_HEAD_K ** -0.5)
        kn = k * lax.rsqrt(seg_sum(k * k) + EPS)
        beta_b = _expand_heads(beta_t, exp_beta)
        eg_b = _expand_heads(jnp.exp(gc_t), exp_dec)
        kbeta = kn * beta_b
        vbeta = v * beta_b
        kbg = kbeta * eg_b
        qeg.append(qn * eg_b)
        kd.append(kn * _expand_heads(jnp.exp(g_end_t - gc_t), exp_dec))
        gc = _columns(gc_t)
        for a, sp in enumerate(spans):
            for h in heads:
                row = SMALL_DECAY + h
                diff = gc[sp, row:row + 1] - gc_t[row:row + 1, sp]
                decay = jnp.where(tri, jnp.exp(jnp.where(tri, diff, 0.0)), 0.0)
                u = (b, a, h)
                p[u] = jnp.where(strict, _mm_nt(kbeta[sp, hs[h]], kn[sp, hs[h]]) * (-decay), 0.0)
                qk[u] = _mm_nt(qn[sp, hs[h]], kn[sp, hs[h]]) * decay
                sol[u] = jnp.concatenate([vbeta[sp, hs[h]], kbg[sp, hs[h]]], axis=1)
    units = list(p.keys())
    steps = CHUNK.bit_length() - 1
    for j in range(steps):
        pb = {u: _bf(p[u]) for u in units}
        sb = {u: _bf(sol[u]) for u in units}
        if j + 1 < steps:
            res = {u: jnp.dot(pb[u], jnp.concatenate([pb[u], sb[u]], axis=1), preferred_element_type=F32)
                   for u in units}
            p = {u: res[u][:, :GDN_SPAN] for u in units}
            sol = {u: sol[u] + res[u][:, GDN_SPAN:] for u in units}
        else:
            sol = {u: sol[u] + jnp.dot(pb[u], sb[u], preferred_element_type=F32) for u in units}
    qks_u = {u: _mm(qk[u], sol[u]) for u in units}
    solh = {(b, h): jnp.concatenate([sol[(b, a, h)] for a in range(n_sp)], axis=0) for b, h in bh}
    qks = {(b, h): jnp.concatenate([qks_u[(b, a, h)] for a in range(n_sp)], axis=0) for b, h in bh}
    qeff = {(b, h): qeg[b][:, hs[h]] - qks[(b, h)][:, GDN_HEAD_V:] for b, h in bh}
    kuw = [{(b, h): _mm_tn(kd[b][rs[c], hs[h]], solh[(b, h)][rs[c]]) for b, h in bh} for c in range(n_chunks)]
    state = {(b, h): state_ref[b, h] for b, h in bh}
    states = []
    for c in range(n_chunks):
        states.append(state)
        col = (c + 1) * CHUNK - 1
        e_end = [jnp.exp(per_b[b][3][:, col:col + 1]) for b in range(nb)]
        state = {(b, h): state[(b, h)] * e_end[b][SMALL_DECAY + h:SMALL_DECAY + h + 1, :]
                 - _mm(kuw[c][(b, h)][:, GDN_HEAD_V:], state[(b, h)]) + kuw[c][(b, h)][:, :GDN_HEAD_V]
                 for b, h in bh}
    for b, h in bh:
        state_ref[b, h] = state[(b, h)]
    outs = [{(b, h): _mm(qeff[(b, h)][rs[c]], states[c][(b, h)]) + qks[(b, h)][rs[c], :GDN_HEAD_V]
             for b, h in bh} for c in range(n_chunks)]
    for b in range(nb):
        o = jnp.concatenate([jnp.concatenate([outs[c][(b, h)] for h in heads], axis=1)
                             for c in range(n_chunks)], axis=0)
        ms = seg_sum(o * o) * (1.0 / GDN_HEAD_V)
        mix_ref[b * TB:(b + 1) * TB, ATT_DIM + SSD_DIM:] = o * lax.rsqrt(ms + EPS) * norm_w * _silu(per_b[b][1])


def _mixer_kernel(layer,
                  sinks_ref, q_ref, k_ref, v_ref, z_ref, xbc_ref, gqkv_ref, gz_ref, small_ref, x_ref,
                  wout_ref, postw_ref, bias_ref, alog_ref, dvec_ref, snormw_ref, gnormw_ref,
                  o_ref,
                  kprev_ref, vprev_ref, sstate_ref, gstate_ref, mix_ref):
    t = pl.program_id(1)

    @pl.when(t == 0)
    def _():
        kprev_ref[...] = jnp.zeros_like(kprev_ref)
        vprev_ref[...] = jnp.zeros_like(vprev_ref)
        sstate_ref[...] = jnp.zeros_like(sstate_ref)
        gstate_ref[...] = jnp.zeros_like(gstate_ref)

    row = lax.broadcasted_iota(jnp.int32, (SMALL_ROWS, 1), 0)
    is_decay = jnp.where(row < SMALL_BETA, 1.0, jnp.where(row >= SMALL_DECAY, 1.0, 0.0))
    neg_a = -jnp.exp(alog_ref[...]) * is_decay
    ri, ci = _iota2((TB, TB))
    same = (ri // CHUNK) == (ci // CHUNK)
    before = jnp.where(ri <= ci, 1.0, 0.0)
    m_block = before.astype(BF16)
    m_chunk = jnp.where(same, before, 0.0).astype(BF16)
    m_whole_chunk = jnp.where(same, 1.0, 0.0).astype(BF16)
    gdn_in, ssd_in = [], []
    for b in range(NB):
        small_t = small_ref[b]
        sp_t = _softplus(small_t + bias_ref[...])
        da_t = sp_t * neg_a
        ssd_in.append((sp_t, _cumsum_lanes(da_t, m_block)))
        gc_t = _cumsum_lanes(da_t, m_chunk)
        g_end_t = _cumsum_lanes(da_t, m_whole_chunk)
        gdn_in.append((gqkv_ref[b], gz_ref[b], _sigmoid(small_t), gc_t, g_end_t))
    _gdn(gdn_in, gnormw_ref[...], gstate_ref, mix_ref)
    for b in range(NB):
        _attention(q_ref[b], k_ref[b], v_ref[b], b, kprev_ref, vprev_ref, sinks_ref, layer, t, mix_ref)
    for b in range(NB):
        sp_t, cs_t = ssd_in[b]
        _ssd(xbc_ref[b], z_ref[b], sp_t, cs_t, dvec_ref[...], snormw_ref[...], b, sstate_ref, mix_ref)

    mixed = jnp.dot(_bf(mix_ref[...]), wout_ref[...], preferred_element_type=F32)
    y = _rmsnorm(mixed, postw_ref[...])
    for b in range(NB):
        o_ref[b] = x_ref[b] + y[b * TB:(b + 1) * TB]


def _mixer_call(projs, small_t, x3d, params, layer):
    bsz, seq, _ = x3d.shape
    grid = (bsz // NB, seq // TB)

    def act_spec(width):
        return pl.BlockSpec((NB, TB, width), lambda b, t: (b, t, 0))

    def par_spec(arr):
        shape = arr.shape[1:]
        return pl.BlockSpec((None,) + shape, lambda b, t: (layer,) + (0,) * len(shape))

    sinks, vmem_params = params[0], params[1:]
    in_specs = ([pl.BlockSpec(memory_space=pltpu.SMEM)]
                + [act_spec(w) for w in PROJ_WIDTHS]
                + [pl.BlockSpec((NB, SMALL_ROWS, TB), lambda b, t: (b, 0, t))]
                + [act_spec(D_MODEL)]
                + [par_spec(p) for p in vmem_params])
    scratch = [
        pltpu.VMEM((NB, WINDOW, ATT_KV_DIM), F32),
        pltpu.VMEM((NB, WINDOW, ATT_KV_DIM), F32),
        pltpu.VMEM((NB, SSD_GROUPS, SSD_STATE, SSD_HPG * SSD_HEAD_DIM), F32),
        pltpu.VMEM((NB, GDN_HEADS, GDN_HEAD_K, GDN_HEAD_V), F32),
        pltpu.VMEM((NB * TB, D_MODEL), F32),
    ]
    acts = [p.reshape(bsz, seq, p.shape[-1]) for p in projs]
    return pl.pallas_call(
        functools.partial(_mixer_kernel, layer),
        grid=grid,
        in_specs=in_specs,
        out_specs=act_spec(D_MODEL),
        out_shape=jax.ShapeDtypeStruct(x3d.shape, F32),
        scratch_shapes=scratch,
        compiler_params=pltpu.CompilerParams(
            dimension_semantics=("arbitrary", "arbitrary"), vmem_limit_bytes=VMEM_LIMIT),
        name="mixer",
    )(sinks, *acts, small_t, x3d, *vmem_params)


def _ffn_kernel(x_ref, prew_ref, postw_ref, wg_ref, wu_ref, wd_ref, o_ref):
    x = x_ref[...]
    h = _bf(_rmsnorm(x, prew_ref[...]))
    acc = jnp.zeros((TM_FFN, D_MODEL), F32)
    for c in range(FF // FF_CHUNK):
        c0, c1 = c * FF_CHUNK, (c + 1) * FF_CHUNK
        gate = jnp.dot(h, wg_ref[:, c0:c1], preferred_element_type=F32)
        up = jnp.dot(h, wu_ref[:, c0:c1], preferred_element_type=F32)
        acc = acc + jnp.dot(_bf(_silu(gate) * up), wd_ref[c0:c1, :], preferred_element_type=F32)
    o_ref[...] = x + _rmsnorm(acc, postw_ref[...])


def _ffn_call(x2d, pre_norm, post_norm, wg, wu, wd, layer):
    m = x2d.shape[0]

    def wspec(shape):
        return pl.BlockSpec((None,) + shape, lambda i: (layer, 0, 0), pipeline_mode=pl.Buffered(1))

    return pl.pallas_call(
        _ffn_kernel,
        grid=(m // TM_FFN,),
        in_specs=[
            pl.BlockSpec((TM_FFN, D_MODEL), lambda i: (i, 0)),
            pl.BlockSpec((None, 1, D_MODEL), lambda i: (layer, 0, 0)),
            pl.BlockSpec((None, 1, D_MODEL), lambda i: (layer, 0, 0)),
            wspec((D_MODEL, FF)),
            wspec((D_MODEL, FF)),
            wspec((FF, D_MODEL)),
        ],
        out_specs=pl.BlockSpec((TM_FFN, D_MODEL), lambda i: (i, 0)),
        out_shape=jax.ShapeDtypeStruct(x2d.shape, F32),
        compiler_params=pltpu.CompilerParams(
            dimension_semantics=("arbitrary",), vmem_limit_bytes=VMEM_LIMIT),
        name="ffn",
    )(x2d, pre_norm, post_norm, wg, wu, wd)


def _regroup_w_in(w_in):
    n_wide_a = ATT_DIM + 2 * ATT_KV_DIM + SSD_DIM + SSD_CONV_DIM
    dt0 = n_wide_a
    g0 = dt0 + SSD_HEADS
    b0 = g0 + GDN_CONV_DIM + GDN_DIM
    wide = jnp.concatenate([w_in[:, :, :n_wide_a], w_in[:, :, g0:b0]], axis=-1).astype(BF16)
    small = jnp.concatenate([w_in[:, :, dt0:g0], w_in[:, :, b0:]], axis=-1)
    return wide, jnp.swapaxes(small, 1, 2).astype(BF16)


def _pack_small(ssd_vals, gdn_vals):
    depth = ssd_vals.shape[0]
    mid = jnp.zeros((depth, GDN_HEADS), F32)
    return jnp.concatenate([ssd_vals.astype(F32), mid, gdn_vals.astype(F32)], axis=-1)[:, :, None]


def kernel(x, pre_mix_norm, post_mix_norm, pre_ffn_norm, post_ffn_norm, w_in, w_out, attn_sinks, ssd_conv_w, ssd_conv_b, ssd_dt_bias, ssd_A_log, ssd_D, ssd_norm_w, gdn_conv_w, gdn_dt_bias, gdn_A_log, gdn_norm_w, ffn_w_gate, ffn_w_up, ffn_w_down):
    bsz, seq, _ = x.shape
    depth = w_in.shape[0]
    assert seq % TB == 0 and seq % TM_PROJ == 0 and (bsz * seq) % TM_FFN == 0 and bsz % NB == 0

    w_in_r, w_small_t = _regroup_w_in(w_in)
    vec3 = lambda a: a.reshape(depth, 1, a.shape[-1]).astype(F32)
    params = (
        attn_sinks.reshape(-1).astype(F32),
        w_out.astype(BF16),
        vec3(post_mix_norm),
        _pack_small(ssd_dt_bias, gdn_dt_bias),
        _pack_small(ssd_A_log, gdn_A_log),
        vec3(jnp.repeat(ssd_D, SSD_HEAD_DIM, axis=-1)),
        vec3(ssd_norm_w),
        vec3(jnp.tile(gdn_norm_w, (1, GDN_HEADS))),
    )
    pre_mix = vec3(pre_mix_norm)
    pre_ffn = vec3(pre_ffn_norm)
    post_ffn = vec3(post_ffn_norm)
    sconvw = ssd_conv_w.astype(F32)
    sconvb = vec3(ssd_conv_b)
    gconvw = gdn_conv_w.astype(F32)
    wg = ffn_w_gate.astype(BF16)
    wu = ffn_w_up.astype(BF16)
    wd = ffn_w_down.astype(BF16)

    for layer in range(depth):
        outs = _proj_call(x.reshape(bsz * seq, D_MODEL), seq, pre_mix, w_in_r, w_small_t,
                          sconvw, sconvb, gconvw, layer)
        x = _mixer_call(outs[:-1], outs[-1], x, params, layer)
        x = _ffn_call(x.reshape(bsz * seq, D_MODEL), pre_ffn, post_ffn, wg, wu, wd, layer).reshape(bsz, seq, D_MODEL)
    return x
```

```python
import functools

import jax
import jax.numpy as jnp
from jax import lax
from jax.experimental import pallas as pl
from jax.experimental.pallas import tpu as pltpu

F32 = jnp.float32
BF16 = jnp.bfloat16

D_MODEL = 1024
CHUNK = 64
EPS = 1e-6
CONV_K = 4

SSD_HEADS = 8
SSD_HEAD_DIM = 64
SSD_DIM = SSD_HEADS * SSD_HEAD_DIM
SSD_GROUPS = 2
SSD_STATE = 128
SSD_HPG = SSD_HEADS // SSD_GROUPS
SSD_CONV_DIM = SSD_DIM + 2 * SSD_GROUPS * SSD_STATE

ATT_HEADS = 4
ATT_KV_HEADS = 2
ATT_HEAD_DIM = 64
ATT_DIM = ATT_HEADS * ATT_HEAD_DIM
ATT_KV_DIM = ATT_KV_HEADS * ATT_HEAD_DIM
WINDOW = 128
WIN_CHUNKS = WINDOW // CHUNK

GDN_HEADS = 4
GDN_HEAD_K = 64
GDN_HEAD_V = 64
GDN_KDIM = GDN_HEADS * GDN_HEAD_K
GDN_DIM = GDN_HEADS * GDN_HEAD_V
GDN_CONV_DIM = 2 * GDN_KDIM + GDN_DIM

FF = 2816
LANES = 128
SUBLANES = 8

PROJ_WIDTHS = (ATT_DIM, ATT_KV_DIM, ATT_KV_DIM, SSD_DIM, SSD_CONV_DIM, GDN_CONV_DIM, GDN_DIM)
PROJ_OFFS = tuple(sum(PROJ_WIDTHS[:i]) for i in range(len(PROJ_WIDTHS)))
PROJ_COLS = sum(PROJ_WIDTHS)
Q_OFF, K_OFF, V_OFF, Z_OFF, XBC_OFF, GQKV_OFF, GZ_OFF = PROJ_OFFS
CONV_LO = XBC_OFF
CONV_HI = GZ_OFF
CONV_COLS = CONV_HI - CONV_LO
SMALL_ROWS = SSD_HEADS + 2 * GDN_HEADS
SMALL_DT = 0
SMALL_BETA = SSD_HEADS
SMALL_DECAY = SSD_HEADS + GDN_HEADS

TM_FFN = 512
FF_CHUNK = 256
TB = 256
NB = 2
PROJ_CHUNK = 256
ATT_QB = 128
GDN_SPAN = 128
VMEM_LIMIT = 56 * 1024 * 1024


def _bf(x):
    return x.astype(BF16)


def _mm(a, b):
    return jnp.dot(_bf(a), _bf(b), preferred_element_type=F32)


def _mm_nt(a, b):
    return lax.dot_general(_bf(a), _bf(b), (((1,), (1,)), ((), ())), preferred_element_type=F32)


def _mm_tn(a, b):
    return lax.dot_general(_bf(a), _bf(b), (((0,), (0,)), ((), ())), preferred_element_type=F32)


def _split(x, terms):
    out = []
    for _ in range(terms - 1):
        t = x.astype(BF16)
        out.append(t)
        x = x - t.astype(F32)
    out.append(x.astype(BF16))
    return out


def _cumsum_lanes(x_t, m_bf16):
    return sum(jnp.dot(t, m_bf16, preferred_element_type=F32) for t in _split(x_t, 3))


def _expand_heads(x_t, r_bf16):
    terms = jnp.concatenate(_split(x_t, 2), axis=0)
    return lax.dot_general(terms, jnp.concatenate([r_bf16, r_bf16], axis=0), (((0,), (0,)), ((), ())),
                           preferred_element_type=F32)


def _columns(x_t):
    pad = jnp.zeros((LANES - x_t.shape[0], x_t.shape[1]), F32)
    return jnp.concatenate([x_t, pad], axis=0).T


def _sigmoid(x):
    return 1.0 / (1.0 + jnp.exp(-x))


def _silu(x):
    return x * _sigmoid(x)


def _softplus(x):
    return jnp.maximum(x, 0.0) + jnp.log1p(jnp.exp(-jnp.abs(x)))


def _rmsnorm(x, w):
    return x * lax.rsqrt(jnp.mean(x * x, axis=-1, keepdims=True) + EPS) * w


def _iota2(shape):
    return (lax.broadcasted_iota(jnp.int32, shape, 0), lax.broadcasted_iota(jnp.int32, shape, 1))


def _head_expand_matrix(src_off, n_heads, width):
    ri, ci = _iota2((SMALL_ROWS, n_heads * width))
    return jnp.where(ri - src_off == ci // width, 1.0, 0.0).astype(BF16)


def _block_ones(n, seg):
    ri, ci = _iota2((n, n))
    return jnp.where((ri // seg) == (ci // seg), 1.0, 0.0).astype(BF16)


def _causal_conv(x, tail, w):
    row = lax.broadcasted_iota(jnp.int32, (SUBLANES, x.shape[1]), 0)
    acc = None
    for j in reversed(range(CONV_K)):
        if j == 0:
            shifted = x
        else:
            xs = pltpu.roll(x, j, axis=0)
            first = jnp.where(row < j, pltpu.roll(tail, j, axis=0), xs[0:SUBLANES])
            shifted = jnp.concatenate([first, xs[SUBLANES:]], axis=0)
        term = shifted * w[CONV_K - 1 - j:CONV_K - j, :]
        acc = term if acc is None else acc + term
    return acc


def _proj_pieces(x_ref, nw_ref, w_ref, wsmall_ref, convw_ref, convb_ref, tail_ref, proj_ref, small_ref):
    cell = {}

    def norm():
        x = x_ref[...].reshape(NB * TB, D_MODEL)
        cell["h"] = _bf(_rmsnorm(x, nw_ref[...]))

    def dot(key):
        c0, b = key

        def run():
            cell[key] = jnp.dot(cell["h"][b * TB:(b + 1) * TB], w_ref[:, c0:c0 + PROJ_CHUNK],
                                preferred_element_type=F32)
        return run

    def post(key):
        c0, b = key

        def run():
            y = cell.pop(key)
            if CONV_LO <= c0 < CONV_HI:
                cols = slice(c0 - CONV_LO, c0 - CONV_LO + PROJ_CHUNK)
                conv = _causal_conv(y, tail_ref[b, :, cols], convw_ref[:, cols])
                tail_ref[b, :, cols] = y[TB - SUBLANES:TB]
                y = _silu(conv + convb_ref[:, cols])
            proj_ref[b * TB:(b + 1) * TB, c0:c0 + PROJ_CHUNK] = y
        return run

    def small():
        small_ref[...] = lax.dot_general(wsmall_ref[...], cell["h"], (((1,), (1,)), ((), ())),
                                          preferred_element_type=F32)

    keys = [(c0, b) for c0 in range(0, PROJ_COLS, PROJ_CHUNK) for b in range(NB)]
    seq = [norm, dot(keys[0])]
    for prev, key in zip(keys[:-1], keys[1:]):
        seq += [dot(key), post(prev)]
    return seq + [post(keys[-1]), small]


def _attention(q, k, v, b, kprev_ref, vprev_ref, sinks_ref, layer, t, mix_ref):
    nk = ATT_QB + WINDOW
    kext = jnp.concatenate([kprev_ref[b], k], axis=0)
    vext = jnp.concatenate([vprev_ref[b], v], axis=0)
    kprev_ref[b] = k[TB - WINDOW:TB]
    vprev_ref[b] = v[TB - WINDOW:TB]
    qi, kj = _iota2((ATT_QB, nk))
    dist = jnp.abs(qi + WINDOW - kj).astype(F32)
    qc = qi // CHUNK
    kc = kj // CHUNK - WIN_CHUNKS
    in_band = jnp.where(kc <= qc, jnp.where(kc >= qc - WIN_CHUNKS, 1, 0), 0)
    grp = ATT_HEADS // ATT_KV_HEADS
    for a in range(TB // ATT_QB):
        first_chunk = t * (TB // CHUNK) + a * (ATT_QB // CHUNK)
        valid = (in_band * jnp.where(kc + first_chunk >= 0, 1, 0)) > 0
        rows = slice(a * ATT_QB, (a + 1) * ATT_QB)
        out_rows = slice(b * TB + a * ATT_QB, b * TB + (a + 1) * ATT_QB)
        krows = slice(a * ATT_QB, a * ATT_QB + nk)
        for kh in range(ATT_KV_HEADS):
            kk = kext[krows, kh * ATT_HEAD_DIM:(kh + 1) * ATT_HEAD_DIM]
            vv = vext[krows, kh * ATT_HEAD_DIM:(kh + 1) * ATT_HEAD_DIM]
            for gi in range(grp):
                h = kh * grp + gi
                slope = 2.0 ** (-8.0 * (h + 1) / ATT_HEADS)
                sink = sinks_ref[layer * ATT_HEADS + h]
                qh = q[rows, h * ATT_HEAD_DIM:(h + 1) * ATT_HEAD_DIM]
                s = _mm_nt(qh, kk) * (ATT_HEAD_DIM ** -0.5) - slope * dist
                s = jnp.where(valid, s, -1e30)
                m = jnp.maximum(jnp.max(s, axis=-1, keepdims=True), sink)
                p = jnp.exp(s - m)
                den = jnp.sum(p, axis=-1, keepdims=True) + jnp.exp(sink - m)
                mix_ref[out_rows, h * ATT_HEAD_DIM:(h + 1) * ATT_HEAD_DIM] = _mm(p, vv) / den


def _ssd(xbc, z, sp_t, cs_t, dvec, norm_w, b, state_ref, mix_ref):
    xs = xbc[:, :SSD_DIM]
    bm = xbc[:, SSD_DIM:SSD_DIM + SSD_GROUPS * SSD_STATE]
    cm = xbc[:, SSD_DIM + SSD_GROUPS * SSD_STATE:]
    expand = _head_expand_matrix(SMALL_DT, SSD_HEADS, SSD_HEAD_DIM)
    cs = _columns(cs_t)
    cs_last_t = cs_t[:, TB - 1:TB]
    xc = xs * _expand_heads(sp_t, expand)
    xcd = xc * _expand_heads(jnp.exp(cs_last_t - cs_t), expand)
    ecs = _expand_heads(jnp.exp(cs_t), expand)
    elast = _expand_heads(jnp.broadcast_to(jnp.exp(cs_last_t), (SMALL_ROWS, SMALL_ROWS)), expand)[0:1]
    ri, ci = _iota2((TB, TB))
    tri = ci <= ri
    gw = SSD_HPG * SSD_HEAD_DIM
    ys = []
    for g in range(SSD_GROUPS):
        bg = bm[:, g * SSD_STATE:(g + 1) * SSD_STATE]
        cg = cm[:, g * SSD_STATE:(g + 1) * SSD_STATE]
        cb = _mm_nt(cg, bg)
        prev = state_ref[b, g]
        y_off = _mm(cg, prev) * ecs[:, g * gw:(g + 1) * gw]
        state_ref[b, g] = prev * elast[:, g * gw:(g + 1) * gw] + _mm_tn(bg, xcd[:, g * gw:(g + 1) * gw])
        yd = []
        for hh in range(SSD_HPG):
            h = g * SSD_HPG + hh
            diff = cs[:, h:h + 1] - cs_t[h:h + 1, :]
            lmat = jnp.where(tri, jnp.exp(jnp.where(tri, diff, 0.0)), 0.0)
            yd.append(_mm(cb * lmat, xc[:, h * SSD_HEAD_DIM:(h + 1) * SSD_HEAD_DIM]))
        ys.append(jnp.concatenate(yd, axis=1) + y_off)
    y = jnp.concatenate(ys, axis=1) + xs * dvec
    gated = y * _silu(z)
    rows = slice(b * TB, (b + 1) * TB)
    for g in range(SSD_GROUPS):
        gg = gated[:, g * gw:(g + 1) * gw]
        gg = gg * lax.rsqrt(jnp.mean(gg * gg, axis=-1, keepdims=True) + EPS)
        mix_ref[rows, ATT_DIM + g * gw:ATT_DIM + (g + 1) * gw] = gg * norm_w[:, g * gw:(g + 1) * gw]


def _gdn(per_b, norm_w, state_ref, mix_ref, fill):
    nb = len(per_b)
    ones = _block_ones(GDN_KDIM, GDN_HEAD_K)

    def seg_sum(x2):
        return jnp.dot(_bf(x2), ones, preferred_element_type=F32)

    exp_beta = _head_expand_matrix(SMALL_BETA, GDN_HEADS, GDN_HEAD_K)
    exp_dec = _head_expand_matrix(SMALL_DECAY, GDN_HEADS, GDN_HEAD_K)
    ri, ci = _iota2((GDN_SPAN, GDN_SPAN))
    same = (ri // CHUNK) == (ci // CHUNK)
    tri = jnp.logical_and(same, ci <= ri)
    strict = jnp.logical_and(same, ci < ri)
    heads = range(GDN_HEADS)
    n_chunks = TB // CHUNK
    n_sp = TB // GDN_SPAN
    hs = [slice(h * GDN_HEAD_K, (h + 1) * GDN_HEAD_K) for h in heads]
    rs = [slice(c * CHUNK, (c + 1) * CHUNK) for c in range(n_chunks)]
    spans = [slice(a * GDN_SPAN, (a + 1) * GDN_SPAN) for a in range(n_sp)]
    bh = [(b, h) for b in range(nb) for h in heads]

    p, sol, qk, qeg, kd = {}, {}, {}, [], []
    for b, (gq, _, beta_t, gc_t, g_end_t) in enumerate(per_b):
        q = gq[:, :GDN_KDIM]
        k = gq[:, GDN_KDIM:2 * GDN_KDIM]
        v = gq[:, 2 * GDN_KDIM:]
        qn = q * lax.rsqrt(seg_sum(q * q) + EPS) * (GDN_HEAD_K ** -0.5)
        kn = k * lax.rsqrt(seg_sum(k * k) + EPS)
        beta_b = _expand_heads(beta_t, exp_beta)
        eg_b = _expand_heads(jnp.exp(gc_t), exp_dec)
        kbeta = kn * beta_b
        vbeta = v * beta_b
        kbg = kbeta * eg_b
        qeg.append(qn * eg_b)
        kd.append(kn * _expand_heads(jnp.exp(g_end_t - gc_t), exp_dec))
        gc = _columns(gc_t)
        for a, sp in enumerate(spans):
            for h in heads:
                row = SMALL_DECAY + h
                diff = gc[sp, row:row + 1] - gc_t[row:row + 1, sp]
                decay = jnp.where(tri, jnp.exp(jnp.where(tri, diff, 0.0)), 0.0)
                u = (b, a, h)
                p[u] = jnp.where(strict, _mm_nt(kbeta[sp, hs[h]], kn[sp, hs[h]]) * (-decay), 0.0)
                qk[u] = _mm_nt(qn[sp, hs[h]], kn[sp, hs[h]]) * decay
                sol[u] = jnp.concatenate([vbeta[sp, hs[h]], kbg[sp, hs[h]]], axis=1)
    units = list(p.keys())
    fill(6)
    steps = CHUNK.bit_length() - 1
    for j in range(steps):
        pb = {u: _bf(p[u]) for u in units}
        sb = {u: _bf(sol[u]) for u in units}
        if j + 1 < steps:
            res = {u: jnp.dot(pb[u], jnp.concatenate([pb[u], sb[u]], axis=1), preferred_element_type=F32)
                   for u in units}
            p = {u: res[u][:, :GDN_SPAN] for u in units}
            sol = {u: sol[u] + res[u][:, GDN_SPAN:] for u in units}
        else:
            sol = {u: sol[u] + jnp.dot(pb[u], sb[u], preferred_element_type=F32) for u in units}
        fill(4)
    qks_u = {u: _mm(qk[u], sol[u]) for u in units}
    solh = {(b, h): jnp.concatenate([sol[(b, a, h)] for a in range(n_sp)], axis=0) for b, h in bh}
    qks = {(b, h): jnp.concatenate([qks_u[(b, a, h)] for a in range(n_sp)], axis=0) for b, h in bh}
    qeff = {(b, h): qeg[b][:, hs[h]] - qks[(b, h)][:, GDN_HEAD_V:] for b, h in bh}
    kuw = [{(b, h): _mm_tn(kd[b][rs[c], hs[h]], solh[(b, h)][rs[c]]) for b, h in bh} for c in range(n_chunks)]
    state = {(b, h): state_ref[b, h] for b, h in bh}
    states = []
    for c in range(n_chunks):
        states.append(state)
        col = (c + 1) * CHUNK - 1
        e_end = [jnp.exp(per_b[b][3][:, col:col + 1]) for b in range(nb)]
        state = {(b, h): state[(b, h)] * e_end[b][SMALL_DECAY + h:SMALL_DECAY + h + 1, :]
                 - _mm(kuw[c][(b, h)][:, GDN_HEAD_V:], state[(b, h)]) + kuw[c][(b, h)][:, :GDN_HEAD_V]
                 for b, h in bh}
        fill(3)
    for b, h in bh:
        state_ref[b, h] = state[(b, h)]
    outs = [{(b, h): _mm(qeff[(b, h)][rs[c]], states[c][(b, h)]) + qks[(b, h)][rs[c], :GDN_HEAD_V]
             for b, h in bh} for c in range(n_chunks)]
    for b in range(nb):
        o = jnp.concatenate([jnp.concatenate([outs[c][(b, h)] for h in heads], axis=1)
                             for c in range(n_chunks)], axis=0)
        ms = seg_sum(o * o) * (1.0 / GDN_HEAD_V)
        mix_ref[b * TB:(b + 1) * TB, ATT_DIM + SSD_DIM:] = o * lax.rsqrt(ms + EPS) * norm_w * _silu(per_b[b][1])


def _mixer_kernel(layer, n_blk,
                  sinks_ref, x_ref, xnext_ref, prew_ref, win_ref, wsmall_ref, convw_ref, convb_ref,
                  wout_ref, postw_ref, bias_ref, alog_ref, dvec_ref, snormw_ref, gnormw_ref,
                  o_ref,
                  proj_ref, small_ref, pnext_ref, snext_ref, tail_ref, kprev_ref, vprev_ref, sstate_ref, gstate_ref, mix_ref):
    t = pl.program_id(1)
    proj_args = (prew_ref, win_ref, wsmall_ref, convw_ref, convb_ref, tail_ref, pnext_ref, snext_ref)

    @pl.when(t == 0)
    def _():
        tail_ref[...] = jnp.zeros_like(tail_ref)
        kprev_ref[...] = jnp.zeros_like(kprev_ref)
        vprev_ref[...] = jnp.zeros_like(vprev_ref)
        sstate_ref[...] = jnp.zeros_like(sstate_ref)
        gstate_ref[...] = jnp.zeros_like(gstate_ref)
        for piece in _proj_pieces(x_ref, *proj_args):
            piece()

    proj_ref[...] = pnext_ref[...]
    small_ref[...] = snext_ref[...]
    pieces = iter(_proj_pieces(xnext_ref, *proj_args))

    def fill(n):
        for _ in range(n):
            piece = next(pieces, None)
            if piece is not None:
                piece()

    def group(b, off, width):
        return proj_ref[b * TB:(b + 1) * TB, off:off + width]

    row = lax.broadcasted_iota(jnp.int32, (SMALL_ROWS, 1), 0)
    is_decay = jnp.where(row < SMALL_BETA, 1.0, jnp.where(row >= SMALL_DECAY, 1.0, 0.0))
    neg_a = -jnp.exp(alog_ref[...]) * is_decay
    ri, ci = _iota2((TB, TB))
    same = (ri // CHUNK) == (ci // CHUNK)
    before = jnp.where(ri <= ci, 1.0, 0.0)
    m_block = before.astype(BF16)
    m_chunk = jnp.where(same, before, 0.0).astype(BF16)
    m_whole_chunk = jnp.where(same, 1.0, 0.0).astype(BF16)
    gdn_in, ssd_in = [], []
    for b in range(NB):
        small_t = small_ref[:, b * TB:(b + 1) * TB]
        sp_t = _softplus(small_t + bias_ref[...])
        da_t = sp_t * neg_a
        ssd_in.append((sp_t, _cumsum_lanes(da_t, m_block)))
        gc_t = _cumsum_lanes(da_t, m_chunk)
        g_end_t = _cumsum_lanes(da_t, m_whole_chunk)
        gdn_in.append((group(b, GQKV_OFF, GDN_CONV_DIM), group(b, GZ_OFF, GDN_DIM), _sigmoid(small_t),
                       gc_t, g_end_t))
    _gdn(gdn_in, gnormw_ref[...], gstate_ref, mix_ref, fill)
    for b in range(NB):
        _attention(group(b, Q_OFF, ATT_DIM), group(b, K_OFF, ATT_KV_DIM), group(b, V_OFF, ATT_KV_DIM),
                   b, kprev_ref, vprev_ref, sinks_ref, layer, t, mix_ref)
        fill(2)
    for b in range(NB):
        sp_t, cs_t = ssd_in[b]
        _ssd(group(b, XBC_OFF, SSD_CONV_DIM), group(b, Z_OFF, SSD_DIM), sp_t, cs_t, dvec_ref[...],
             snormw_ref[...], b, sstate_ref, mix_ref)
        fill(2)
    fill(PROJ_COLS)

    mixed = jnp.dot(_bf(mix_ref[...]), wout_ref[...], preferred_element_type=F32)
    y = _rmsnorm(mixed, postw_ref[...])
    for b in range(NB):
        o_ref[b] = x_ref[b] + y[b * TB:(b + 1) * TB]


def _mixer_call(x3d, params, layer):
    bsz, seq, _ = x3d.shape
    n_blk = seq // TB
    grid = (bsz // NB, n_blk)

    def par_spec(arr, **kw):
        shape = arr.shape[1:]
        return pl.BlockSpec((None,) + shape, lambda b, t: (layer,) + (0,) * len(shape), **kw)

    sinks, pre_norm, w_in_r, w_small_t, convw, convb, wout = params[:7]
    rest = params[7:]
    in_specs = ([pl.BlockSpec(memory_space=pltpu.SMEM),
                 pl.BlockSpec((NB, TB, D_MODEL), lambda b, t: (b, t, 0)),
                 pl.BlockSpec((NB, TB, D_MODEL), lambda b, t: (b, jnp.minimum(t + 1, n_blk - 1), 0)),
                 par_spec(pre_norm),
                 par_spec(w_in_r, pipeline_mode=pl.Buffered(1)),
                 par_spec(w_small_t),
                 par_spec(convw),
                 par_spec(convb),
                 par_spec(wout, pipeline_mode=pl.Buffered(1))]
                + [par_spec(p) for p in rest])
    scratch = [
        pltpu.VMEM((NB * TB, PROJ_COLS), F32),
        pltpu.VMEM((SMALL_ROWS, NB * TB), F32),
        pltpu.VMEM((NB * TB, PROJ_COLS), F32),
        pltpu.VMEM((SMALL_ROWS, NB * TB), F32),
        pltpu.VMEM((NB, SUBLANES, CONV_COLS), F32),
        pltpu.VMEM((NB, WINDOW, ATT_KV_DIM), F32),
        pltpu.VMEM((NB, WINDOW, ATT_KV_DIM), F32),
        pltpu.VMEM((NB, SSD_GROUPS, SSD_STATE, SSD_HPG * SSD_HEAD_DIM), F32),
        pltpu.VMEM((NB, GDN_HEADS, GDN_HEAD_K, GDN_HEAD_V), F32),
        pltpu.VMEM((NB * TB, D_MODEL), F32),
    ]
    return pl.pallas_call(
        functools.partial(_mixer_kernel, layer, n_blk),
        grid=grid,
        in_specs=in_specs,
        out_specs=pl.BlockSpec((NB, TB, D_MODEL), lambda b, t: (b, t, 0)),
        out_shape=jax.ShapeDtypeStruct(x3d.shape, F32),
        scratch_shapes=scratch,
        compiler_params=pltpu.CompilerParams(
            dimension_semantics=("arbitrary", "arbitrary"), vmem_limit_bytes=VMEM_LIMIT),
        name="mixer",
    )(sinks, x3d, x3d, pre_norm, w_in_r, w_small_t, convw, convb, wout, *rest)


def _ffn_kernel(x_ref, prew_ref, postw_ref, wg_ref, wu_ref, wd_ref, o_ref):
    x = x_ref[...]
    h = _bf(_rmsnorm(x, prew_ref[...]))
    acc = jnp.zeros((TM_FFN, D_MODEL), F32)
    for c in range(FF // FF_CHUNK):
        c0, c1 = c * FF_CHUNK, (c + 1) * FF_CHUNK
        gate = jnp.dot(h, wg_ref[:, c0:c1], preferred_element_type=F32)
        up = jnp.dot(h, wu_ref[:, c0:c1], preferred_element_type=F32)
        acc = acc + jnp.dot(_bf(_silu(gate) * up), wd_ref[c0:c1, :], preferred_element_type=F32)
    o_ref[...] = x + _rmsnorm(acc, postw_ref[...])


def _ffn_call(x2d, pre_norm, post_norm, wg, wu, wd, layer):
    m = x2d.shape[0]

    def wspec(shape):
        return pl.BlockSpec((None,) + shape, lambda i: (layer, 0, 0), pipeline_mode=pl.Buffered(1))

    return pl.pallas_call(
        _ffn_kernel,
        grid=(m // TM_FFN,),
        in_specs=[
            pl.BlockSpec((TM_FFN, D_MODEL), lambda i: (i, 0)),
            pl.BlockSpec((None, 1, D_MODEL), lambda i: (layer, 0, 0)),
            pl.BlockSpec((None, 1, D_MODEL), lambda i: (layer, 0, 0)),
            wspec((D_MODEL, FF)),
            wspec((D_MODEL, FF)),
            wspec((FF, D_MODEL)),
        ],
        out_specs=pl.BlockSpec((TM_FFN, D_MODEL), lambda i: (i, 0)),
        out_shape=jax.ShapeDtypeStruct(x2d.shape, F32),
        compiler_params=pltpu.CompilerParams(
            dimension_semantics=("arbitrary",), vmem_limit_bytes=VMEM_LIMIT),
        name="ffn",
    )(x2d, pre_norm, post_norm, wg, wu, wd)


def _regroup_w_in(w_in):
    n_wide_a = ATT_DIM + 2 * ATT_KV_DIM + SSD_DIM + SSD_CONV_DIM
    dt0 = n_wide_a
    g0 = dt0 + SSD_HEADS
    b0 = g0 + GDN_CONV_DIM + GDN_DIM
    wide = jnp.concatenate([w_in[:, :, :n_wide_a], w_in[:, :, g0:b0]], axis=-1).astype(BF16)
    small = jnp.concatenate([w_in[:, :, dt0:g0], w_in[:, :, b0:]], axis=-1)
    return wide, jnp.swapaxes(small, 1, 2).astype(BF16)


def _pack_small(ssd_vals, gdn_vals):
    depth = ssd_vals.shape[0]
    mid = jnp.zeros((depth, GDN_HEADS), F32)
    return jnp.concatenate([ssd_vals.astype(F32), mid, gdn_vals.astype(F32)], axis=-1)[:, :, None]


def kernel(x, pre_mix_norm, post_mix_norm, pre_ffn_norm, post_ffn_norm, w_in, w_out, attn_sinks, ssd_conv_w, ssd_conv_b, ssd_dt_bias, ssd_A_log, ssd_D, ssd_norm_w, gdn_conv_w, gdn_dt_bias, gdn_A_log, gdn_norm_w, ffn_w_gate, ffn_w_up, ffn_w_down):
    bsz, seq, _ = x.shape
    depth = w_in.shape[0]
    assert seq % TB == 0 and (bsz * seq) % TM_FFN == 0 and bsz % NB == 0

    w_in_r, w_small_t = _regroup_w_in(w_in)
    vec3 = lambda a: a.reshape(depth, 1, a.shape[-1]).astype(F32)
    convw = jnp.concatenate([ssd_conv_w, gdn_conv_w], axis=-1).astype(F32)
    convb = jnp.concatenate([ssd_conv_b.astype(F32), jnp.zeros((depth, GDN_CONV_DIM), F32)], axis=-1)[:, None, :]
    params = (
        attn_sinks.reshape(-1).astype(F32),
        vec3(pre_mix_norm),
        w_in_r,
        w_small_t,
        convw,
        convb,
        w_out.astype(BF16),
        vec3(post_mix_norm),
        _pack_small(ssd_dt_bias, gdn_dt_bias),
        _pack_small(ssd_A_log, gdn_A_log),
        vec3(jnp.repeat(ssd_D, SSD_HEAD_DIM, axis=-1)),
        vec3(ssd_norm_w),
        vec3(jnp.tile(gdn_norm_w, (1, GDN_HEADS))),
    )
    pre_ffn = vec3(pre_ffn_norm)
    post_ffn = vec3(post_ffn_norm)
    wg = ffn_w_gate.astype(BF16)
    wu = ffn_w_up.astype(BF16)
    wd = ffn_w_down.astype(BF16)

    for layer in range(depth):
        x = _mixer_call(x, params, layer)
        x = _ffn_call(x.reshape(bsz * seq, D_MODEL), pre_ffn, post_ffn, wg, wu, wd, layer).reshape(bsz, seq, D_MODEL)
    return x
```

```python
import functools

import jax
import jax.numpy as jnp
from jax import lax
from jax.experimental import pallas as pl
from jax.experimental.pallas import tpu as pltpu

F32 = jnp.float32
BF16 = jnp.bfloat16

D_MODEL = 1024
CHUNK = 64
EPS = 1e-6
CONV_K = 4
MASKED = -1e30

SSD_HEADS = 8
SSD_HEAD_DIM = 64
SSD_DIM = SSD_HEADS * SSD_HEAD_DIM
SSD_GROUPS = 2
SSD_STATE = 128
SSD_HPG = SSD_HEADS // SSD_GROUPS
SSD_CONV_DIM = SSD_DIM + 2 * SSD_GROUPS * SSD_STATE

ATT_HEADS = 4
ATT_KV_HEADS = 2
ATT_HEAD_DIM = 64
ATT_DIM = ATT_HEADS * ATT_HEAD_DIM
ATT_KV_DIM = ATT_KV_HEADS * ATT_HEAD_DIM
WINDOW = 128
WIN_CHUNKS = WINDOW // CHUNK

GDN_HEADS = 4
GDN_HEAD_K = 64
GDN_HEAD_V = 64
GDN_KDIM = GDN_HEADS * GDN_HEAD_K
GDN_DIM = GDN_HEADS * GDN_HEAD_V
GDN_CONV_DIM = 2 * GDN_KDIM + GDN_DIM

FF = 2816
LANES = 128
SUBLANES = 8
BF16_ROWS = 16

PROJ_WIDTHS = (ATT_DIM, ATT_KV_DIM, ATT_KV_DIM, SSD_DIM, SSD_CONV_DIM, GDN_CONV_DIM, GDN_DIM)
PROJ_OFFS = tuple(sum(PROJ_WIDTHS[:i]) for i in range(len(PROJ_WIDTHS)))
PROJ_COLS = sum(PROJ_WIDTHS)
Q_OFF, K_OFF, V_OFF, Z_OFF, XBC_OFF, GQKV_OFF, GZ_OFF = PROJ_OFFS
CONV_LO = XBC_OFF
CONV_HI = GZ_OFF
CONV_COLS = CONV_HI - CONV_LO
W_SPLIT = GQKV_OFF
SMALL_ROWS = SSD_HEADS + 2 * GDN_HEADS
SMALL_DT = 0
SMALL_BETA = SSD_HEADS
SMALL_DECAY = SSD_HEADS + GDN_HEADS

TM_FFN = 512
FF_CHUNK = 256
TB = 256
NB = 2
PROJ_CHUNK = 256
ATT_QB = 128
GDN_SPAN = 128
VMEM_LIMIT = 56 * 1024 * 1024


def _bf(x):
    return x.astype(BF16)


def _mm(a, b):
    return jnp.dot(_bf(a), _bf(b), preferred_element_type=F32)


def _mm_nt(a, b):
    return lax.dot_general(_bf(a), _bf(b), (((1,), (1,)), ((), ())), preferred_element_type=F32)


def _mm_tn(a, b):
    return lax.dot_general(_bf(a), _bf(b), (((0,), (0,)), ((), ())), preferred_element_type=F32)


def _split(x, terms):
    out = []
    for _ in range(terms - 1):
        t = x.astype(BF16)
        out.append(t)
        x = x - t.astype(F32)
    out.append(x.astype(BF16))
    return out


def _cumsum_lanes(x_t, m_bf16):
    return sum(jnp.dot(t, m_bf16, preferred_element_type=F32) for t in _split(x_t, 3))


def _expand_heads(x_t, r_bf16):
    terms = jnp.concatenate(_split(x_t, 2), axis=0)
    return lax.dot_general(terms, jnp.concatenate([r_bf16, r_bf16], axis=0), (((0,), (0,)), ((), ())),
                           preferred_element_type=F32)


def _columns(x_t):
    pad = jnp.zeros((LANES - x_t.shape[0], x_t.shape[1]), F32)
    return jnp.concatenate([x_t, pad], axis=0).T


def _sigmoid(x):
    return 0.5 + 0.5 * jnp.tanh(0.5 * x)


def _silu(x):
    hx = 0.5 * x
    return hx + hx * jnp.tanh(hx)


def _softplus(x):
    return jnp.maximum(x, 0.0) + jnp.log1p(jnp.exp(-jnp.abs(x)))


def _rmsnorm(x, w):
    return x * lax.rsqrt(jnp.mean(x * x, axis=-1, keepdims=True) + EPS) * w


def _iota2(shape):
    return (lax.broadcasted_iota(jnp.int32, shape, 0), lax.broadcasted_iota(jnp.int32, shape, 1))


def _head_expand_matrix(src_off, n_heads, width):
    ri, ci = _iota2((SMALL_ROWS, n_heads * width))
    return jnp.where(ri - src_off == ci // width, 1.0, 0.0).astype(BF16)


def _block_ones(n, seg):
    ri, ci = _iota2((n, n))
    return jnp.where((ri // seg) == (ci // seg), 1.0, 0.0).astype(BF16)


def _shift_rows(x, tail, j):
    row = lax.broadcasted_iota(jnp.int32, (SUBLANES, x.shape[1]), 0)
    xs = pltpu.roll(x, j, axis=0)
    first = jnp.where(row < j, pltpu.roll(tail, j, axis=0), xs[0:SUBLANES])
    return jnp.concatenate([first, xs[SUBLANES:]], axis=0)


def _causal_conv(x, tail, w):
    assert CONV_K == 4
    x1 = _shift_rows(x, tail, 1)
    u = x * w[3:4, :] + x1 * w[2:3, :]
    v = x * w[1:2, :] + x1 * w[0:1, :]
    v_tail = tail * w[1:2, :] + pltpu.roll(tail, 1, axis=0) * w[0:1, :]
    return u + _shift_rows(v, v_tail, 2)


def _proj_pieces(x_ref, nw_ref, w_refs, wsmall_ref, convw_ref, convb_ref, tail_ref, proj_ref, small_ref):
    cell = {}

    def norm():
        x = x_ref[...].reshape(NB * TB, D_MODEL)
        cell["h"] = _bf(_rmsnorm(x, nw_ref[...]))

    def dot(key):
        c0, b = key

        def run():
            w_ref, w0 = (w_refs[0], c0) if c0 < W_SPLIT else (w_refs[1], c0 - W_SPLIT)
            cell[key] = jnp.dot(cell["h"][b * TB:(b + 1) * TB], w_ref[:, w0:w0 + PROJ_CHUNK],
                                preferred_element_type=F32)
        return run

    def post(key):
        c0, b = key

        def run():
            y = cell.pop(key)
            if CONV_LO <= c0 < CONV_HI:
                cols = slice(c0 - CONV_LO, c0 - CONV_LO + PROJ_CHUNK)
                conv = _causal_conv(y, tail_ref[b, :, cols], convw_ref[:, cols])
                tail_ref[b, :, cols] = y[TB - SUBLANES:TB]
                y = _silu(conv + convb_ref[:, cols])
            proj_ref[b * TB:(b + 1) * TB, c0:c0 + PROJ_CHUNK] = y
        return run

    def small():
        small_ref[...] = lax.dot_general(wsmall_ref[...], cell["h"], (((1,), (1,)), ((), ())),
                                          preferred_element_type=F32)

    keys = [(c0, b) for c0 in range(0, PROJ_COLS, PROJ_CHUNK) for b in range(NB)]
    seq = [norm, dot(keys[0])]
    for prev, key in zip(keys[:-1], keys[1:]):
        seq += [dot(key), post(prev)]
    return seq + [post(keys[-1]), small]


def _attention(q, k, v, b, kprev_ref, vprev_ref, sinks_ref, layer, t, mix_ref):
    nk = ATT_QB + WINDOW
    kext = jnp.concatenate([kprev_ref[b], k], axis=0)
    vext = jnp.concatenate([vprev_ref[b], v], axis=0)
    kprev_ref[b] = k[TB - WINDOW:TB]
    vprev_ref[b] = v[TB - WINDOW:TB]
    qi, kj = _iota2((ATT_QB, nk))
    dist = jnp.abs(qi + WINDOW - kj).astype(F32)
    qc = qi // CHUNK
    kc = kj // CHUNK - WIN_CHUNKS
    in_band = jnp.where(kc <= qc, jnp.where(kc >= qc - WIN_CHUNKS, 1, 0), 0)
    grp = ATT_HEADS // ATT_KV_HEADS
    for a in range(TB // ATT_QB):
        first_chunk = t * (TB // CHUNK) + a * (ATT_QB // CHUNK)
        valid = (in_band * jnp.where(kc + first_chunk >= 0, 1, 0)) > 0
        rows = slice(a * ATT_QB, (a + 1) * ATT_QB)
        out_rows = slice(b * TB + a * ATT_QB, b * TB + (a + 1) * ATT_QB)
        krows = slice(a * ATT_QB, a * ATT_QB + nk)
        for kh in range(ATT_KV_HEADS):
            kk = kext[krows, kh * ATT_HEAD_DIM:(kh + 1) * ATT_HEAD_DIM]
            vv = vext[krows, kh * ATT_HEAD_DIM:(kh + 1) * ATT_HEAD_DIM]
            for gi in range(grp):
                h = kh * grp + gi
                slope = 2.0 ** (-8.0 * (h + 1) / ATT_HEADS)
                sink = sinks_ref[layer * ATT_HEADS + h]
                qh = q[rows, h * ATT_HEAD_DIM:(h + 1) * ATT_HEAD_DIM]
                s = _mm_nt(qh, kk) * (ATT_HEAD_DIM ** -0.5) - slope * dist
                s = jnp.where(valid, s, MASKED)
                m = jnp.maximum(jnp.max(s, axis=-1, keepdims=True), sink)
                p = jnp.exp(s - m)
                den = jnp.sum(p, axis=-1, keepdims=True) + jnp.exp(sink - m)
                mix_ref[out_rows, h * ATT_HEAD_DIM:(h + 1) * ATT_HEAD_DIM] = _mm(p, vv) / den


def _ssd(xbc, z, sp_t, cs_t, dvec, norm_w, b, state_ref, mix_ref):
    xs = xbc[:, :SSD_DIM]
    bm = xbc[:, SSD_DIM:SSD_DIM + SSD_GROUPS * SSD_STATE]
    cm = xbc[:, SSD_DIM + SSD_GROUPS * SSD_STATE:]
    expand = _head_expand_matrix(SMALL_DT, SSD_HEADS, SSD_HEAD_DIM)
    cs = _columns(cs_t)
    cs_last_t = cs_t[:, TB - 1:TB]
    xc = xs * _expand_heads(sp_t, expand)
    xcd = xc * _expand_heads(jnp.exp(cs_last_t - cs_t), expand)
    ecs = _expand_heads(jnp.exp(cs_t), expand)
    elast = _expand_heads(jnp.broadcast_to(jnp.exp(cs_last_t), (SMALL_ROWS, SMALL_ROWS)), expand)[0:1]
    ri, ci = _iota2((TB, TB))
    tri = ci <= ri
    gw = SSD_HPG * SSD_HEAD_DIM
    ys = []
    for g in range(SSD_GROUPS):
        bg = bm[:, g * SSD_STATE:(g + 1) * SSD_STATE]
        cg = cm[:, g * SSD_STATE:(g + 1) * SSD_STATE]
        cb = _mm_nt(cg, bg)
        prev = state_ref[b, g]
        y_off = _mm(cg, prev) * ecs[:, g * gw:(g + 1) * gw]
        state_ref[b, g] = prev * elast[:, g * gw:(g + 1) * gw] + _mm_tn(bg, xcd[:, g * gw:(g + 1) * gw])
        yd = []
        for hh in range(SSD_HPG):
            h = g * SSD_HPG + hh
            diff = cs[:, h:h + 1] - cs_t[h:h + 1, :]
            lmat = jnp.exp(jnp.where(tri, diff, MASKED))
            yd.append(_mm(cb * lmat, xc[:, h * SSD_HEAD_DIM:(h + 1) * SSD_HEAD_DIM]))
        ys.append(jnp.concatenate(yd, axis=1) + y_off)
    y = jnp.concatenate(ys, axis=1) + xs * dvec
    gated = y * _silu(z)
    rows = slice(b * TB, (b + 1) * TB)
    for g in range(SSD_GROUPS):
        gg = gated[:, g * gw:(g + 1) * gw]
        gg = gg * lax.rsqrt(jnp.mean(gg * gg, axis=-1, keepdims=True) + EPS)
        mix_ref[rows, ATT_DIM + g * gw:ATT_DIM + (g + 1) * gw] = gg * norm_w[:, g * gw:(g + 1) * gw]


def _gdn(per_b, norm_w, state_ref, mix_ref, fill):
    nb = len(per_b)
    ones = _block_ones(GDN_KDIM, GDN_HEAD_K)

    def seg_sum(x2):
        return jnp.dot(_bf(x2), ones, preferred_element_type=F32)

    exp_beta = _head_expand_matrix(SMALL_BETA, GDN_HEADS, GDN_HEAD_K)
    exp_dec = _head_expand_matrix(SMALL_DECAY, GDN_HEADS, GDN_HEAD_K)
    ri, ci = _iota2((GDN_SPAN, GDN_SPAN))
    same = (ri // CHUNK) == (ci // CHUNK)
    tri = jnp.logical_and(same, ci <= ri)
    strict = jnp.logical_and(same, ci < ri)
    heads = range(GDN_HEADS)
    n_chunks = TB // CHUNK
    n_sp = TB // GDN_SPAN
    hs = [slice(h * GDN_HEAD_K, (h + 1) * GDN_HEAD_K) for h in heads]
    rs = [slice(c * CHUNK, (c + 1) * CHUNK) for c in range(n_chunks)]
    spans = [slice(a * GDN_SPAN, (a + 1) * GDN_SPAN) for a in range(n_sp)]
    bh = [(b, h) for b in range(nb) for h in heads]

    p, sol, qk, qeg, kd = {}, {}, {}, [], []
    for b, (gq, _, beta_t, gc_t, g_end_t) in enumerate(per_b):
        q = gq[:, :GDN_KDIM]
        k = gq[:, GDN_KDIM:2 * GDN_KDIM]
        v = gq[:, 2 * GDN_KDIM:]
        qn = q * lax.rsqrt(seg_sum(q * q) + EPS) * (GDN_HEAD_K ** -0.5)
        kn = k * lax.rsqrt(seg_sum(k * k) + EPS)
        beta_b = _expand_heads(beta_t, exp_beta)
        eg_b = _expand_heads(jnp.exp(gc_t), exp_dec)
        kbeta = kn * beta_b
        vbeta = v * beta_b
        kbg = kbeta * eg_b
        qeg.append(qn * eg_b)
        kd.append(kn * _expand_heads(jnp.exp(g_end_t - gc_t), exp_dec))
        gc = _columns(gc_t)
        for a, sp in enumerate(spans):
            for h in heads:
                row = SMALL_DECAY + h
                diff = gc[sp, row:row + 1] - gc_t[row:row + 1, sp]
                decay = jnp.exp(jnp.where(tri, diff, MASKED))
                u = (b, a, h)
                p[u] = jnp.where(strict, _mm_nt(kbeta[sp, hs[h]], kn[sp, hs[h]]) * (-decay), 0.0)
                qk[u] = _mm_nt(qn[sp, hs[h]], kn[sp, hs[h]]) * decay
                sol[u] = jnp.concatenate([vbeta[sp, hs[h]], kbg[sp, hs[h]]], axis=1)
    units = list(p.keys())
    fill(6)
    steps = CHUNK.bit_length() - 1
    for j in range(steps):
        skip = (1 << j) // BF16_ROWS * BF16_ROWS
        keep = CHUNK - skip

        def live_rows(x):
            if skip == 0:
                return x
            return jnp.concatenate([x[c * CHUNK + skip:(c + 1) * CHUNK] for c in range(GDN_SPAN // CHUNK)], axis=0)

        def all_rows(r):
            if skip == 0:
                return r
            zeros = jnp.zeros((skip, r.shape[1]), r.dtype)
            return jnp.concatenate(
                [piece for c in range(GDN_SPAN // CHUNK) for piece in (zeros, r[c * keep:(c + 1) * keep])], axis=0)

        pb = {u: _bf(p[u]) for u in units}
        sb = {u: _bf(sol[u]) for u in units}
        if j + 1 < steps:
            res = {u: all_rows(jnp.dot(live_rows(pb[u]), jnp.concatenate([pb[u], sb[u]], axis=1),
                                       preferred_element_type=F32)) for u in units}
            p = {u: res[u][:, :GDN_SPAN] for u in units}
            sol = {u: sol[u] + res[u][:, GDN_SPAN:] for u in units}
        else:
            sol = {u: sol[u] + all_rows(jnp.dot(live_rows(pb[u]), sb[u], preferred_element_type=F32))
                   for u in units}
        fill(4)
    qks_u = {u: _mm(qk[u], sol[u]) for u in units}
    solh = {(b, h): jnp.concatenate([sol[(b, a, h)] for a in range(n_sp)], axis=0) for b, h in bh}
    qks = {(b, h): jnp.concatenate([qks_u[(b, a, h)] for a in range(n_sp)], axis=0) for b, h in bh}
    qeff = {(b, h): qeg[b][:, hs[h]] - qks[(b, h)][:, GDN_HEAD_V:] for b, h in bh}
    kuw = [{(b, h): _mm_tn(kd[b][rs[c], hs[h]], solh[(b, h)][rs[c]]) for b, h in bh} for c in range(n_chunks)]
    state = {(b, h): state_ref[b, h] for b, h in bh}
    states = []
    for c in range(n_chunks):
        states.append(state)
        col = (c + 1) * CHUNK - 1
        e_end = [jnp.exp(per_b[b][3][:, col:col + 1]) for b in range(nb)]
        state = {(b, h): state[(b, h)] * e_end[b][SMALL_DECAY + h:SMALL_DECAY + h + 1, :]
                 - _mm(kuw[c][(b, h)][:, GDN_HEAD_V:], state[(b, h)]) + kuw[c][(b, h)][:, :GDN_HEAD_V]
                 for b, h in bh}
        fill(3)
    for b, h in bh:
        state_ref[b, h] = state[(b, h)]
    outs = [{(b, h): _mm(qeff[(b, h)][rs[c]], states[c][(b, h)]) + qks[(b, h)][rs[c], :GDN_HEAD_V]
             for b, h in bh} for c in range(n_chunks)]
    for b in range(nb):
        o = jnp.concatenate([jnp.concatenate([outs[c][(b, h)] for h in heads], axis=1)
                             for c in range(n_chunks)], axis=0)
        ms = seg_sum(o * o) * (1.0 / GDN_HEAD_V)
        mix_ref[b * TB:(b + 1) * TB, ATT_DIM + SSD_DIM:] = o * lax.rsqrt(ms + EPS) * norm_w * _silu(per_b[b][1])


def _mixer_kernel(layer, n_blk,
                  sinks_ref, x_ref, xnext_ref, prew_ref, wina_ref, winb_ref, wsmall_ref, convw_ref, convb_ref,
                  wout_ref, postw_ref, bias_ref, alog_ref, dvec_ref, snormw_ref, gnormw_ref,
                  o_ref,
                  proj_ref, small_ref, pnext_ref, snext_ref, tail_ref, kprev_ref, vprev_ref, sstate_ref, gstate_ref, mix_ref):
    t = pl.program_id(1)
    proj_args = (prew_ref, (wina_ref, winb_ref), wsmall_ref, convw_ref, convb_ref, tail_ref, pnext_ref, snext_ref)

    @pl.when(t == 0)
    def _():
        tail_ref[...] = jnp.zeros_like(tail_ref)
        kprev_ref[...] = jnp.zeros_like(kprev_ref)
        vprev_ref[...] = jnp.zeros_like(vprev_ref)
        sstate_ref[...] = jnp.zeros_like(sstate_ref)
        gstate_ref[...] = jnp.zeros_like(gstate_ref)
        for piece in _proj_pieces(x_ref, *proj_args):
            piece()

    proj_ref[...] = pnext_ref[...]
    small_ref[...] = snext_ref[...]
    pieces = iter(_proj_pieces(xnext_ref, *proj_args))

    def fill(n):
        for _ in range(n):
            piece = next(pieces, None)
            if piece is not None:
                piece()

    def group(b, off, width):
        return proj_ref[b * TB:(b + 1) * TB, off:off + width]

    row = lax.broadcasted_iota(jnp.int32, (SMALL_ROWS, 1), 0)
    is_decay = jnp.where(row < SMALL_BETA, 1.0, jnp.where(row >= SMALL_DECAY, 1.0, 0.0))
    neg_a = -jnp.exp(alog_ref[...]) * is_decay
    ri, ci = _iota2((TB, TB))
    same = (ri // CHUNK) == (ci // CHUNK)
    before = jnp.where(ri <= ci, 1.0, 0.0)
    m_block = before.astype(BF16)
    m_chunk = jnp.where(same, before, 0.0).astype(BF16)
    m_whole_chunk = jnp.where(same, 1.0, 0.0).astype(BF16)
    gdn_in, ssd_in = [], []
    for b in range(NB):
        small_t = small_ref[:, b * TB:(b + 1) * TB]
        sp_t = _softplus(small_t + bias_ref[...])
        da_t = sp_t * neg_a
        ssd_in.append((sp_t, _cumsum_lanes(da_t, m_block)))
        gc_t = _cumsum_lanes(da_t, m_chunk)
        g_end_t = _cumsum_lanes(da_t, m_whole_chunk)
        gdn_in.append((group(b, GQKV_OFF, GDN_CONV_DIM), group(b, GZ_OFF, GDN_DIM), _sigmoid(small_t),
                       gc_t, g_end_t))
    _gdn(gdn_in, gnormw_ref[...], gstate_ref, mix_ref, fill)
    for b in range(NB):
        _attention(group(b, Q_OFF, ATT_DIM), group(b, K_OFF, ATT_KV_DIM), group(b, V_OFF, ATT_KV_DIM),
                   b, kprev_ref, vprev_ref, sinks_ref, layer, t, mix_ref)
        fill(2)
    for b in range(NB):
        sp_t, cs_t = ssd_in[b]
        _ssd(group(b, XBC_OFF, SSD_CONV_DIM), group(b, Z_OFF, SSD_DIM), sp_t, cs_t, dvec_ref[...],
             snormw_ref[...], b, sstate_ref, mix_ref)
        fill(2)
    fill(PROJ_COLS)

    mixed = jnp.dot(_bf(mix_ref[...]), wout_ref[...], preferred_element_type=F32)
    y = _rmsnorm(mixed, postw_ref[...])
    for b in range(NB):
        o_ref[b] = x_ref[b] + y[b * TB:(b + 1) * TB]


def _mixer_call(x3d, params, layer):
    bsz, seq, _ = x3d.shape
    n_blk = seq // TB
    grid = (bsz // NB, n_blk)

    def par_spec(arr, **kw):
        shape = arr.shape[1:]
        return pl.BlockSpec((None,) + shape, lambda b, t: (layer,) + (0,) * len(shape), **kw)

    sinks, pre_norm, w_in_a, w_in_b, w_small_t, convw, convb, wout = params[:8]
    rest = params[8:]
    in_specs = ([pl.BlockSpec(memory_space=pltpu.SMEM),
                 pl.BlockSpec((NB, TB, D_MODEL), lambda b, t: (b, t, 0)),
                 pl.BlockSpec((NB, TB, D_MODEL), lambda b, t: (b, jnp.minimum(t + 1, n_blk - 1), 0)),
                 par_spec(pre_norm),
                 par_spec(w_in_a, pipeline_mode=pl.Buffered(1)),
                 par_spec(w_in_b, pipeline_mode=pl.Buffered(1)),
                 par_spec(w_small_t),
                 par_spec(convw),
                 par_spec(convb),
                 par_spec(wout, pipeline_mode=pl.Buffered(1))]
                + [par_spec(p) for p in rest])
    scratch = [
        pltpu.VMEM((NB * TB, PROJ_COLS), F32),
        pltpu.VMEM((SMALL_ROWS, NB * TB), F32),
        pltpu.VMEM((NB * TB, PROJ_COLS), F32),
        pltpu.VMEM((SMALL_ROWS, NB * TB), F32),
        pltpu.VMEM((NB, SUBLANES, CONV_COLS), F32),
        pltpu.VMEM((NB, WINDOW, ATT_KV_DIM), F32),
        pltpu.VMEM((NB, WINDOW, ATT_KV_DIM), F32),
        pltpu.VMEM((NB, SSD_GROUPS, SSD_STATE, SSD_HPG * SSD_HEAD_DIM), F32),
        pltpu.VMEM((NB, GDN_HEADS, GDN_HEAD_K, GDN_HEAD_V), F32),
        pltpu.VMEM((NB * TB, D_MODEL), F32),
    ]
    return pl.pallas_call(
        functools.partial(_mixer_kernel, layer, n_blk),
        grid=grid,
        in_specs=in_specs,
        out_specs=pl.BlockSpec((NB, TB, D_MODEL), lambda b, t: (b, t, 0)),
        out_shape=jax.ShapeDtypeStruct(x3d.shape, F32),
        scratch_shapes=scratch,
        compiler_params=pltpu.CompilerParams(
            dimension_semantics=("arbitrary", "arbitrary"), vmem_limit_bytes=VMEM_LIMIT),
        name="mixer",
    )(sinks, x3d, x3d, pre_norm, w_in_a, w_in_b, w_small_t, convw, convb, wout, *rest)


def _ffn_kernel(x_ref, prew_ref, postw_ref, wg_ref, wu_ref, wd_ref, o_ref):
    x = x_ref[...]
    h = _bf(_rmsnorm(x, prew_ref[...]))
    acc = jnp.zeros((TM_FFN, D_MODEL), F32)
    for c in range(FF // FF_CHUNK):
        c0, c1 = c * FF_CHUNK, (c + 1) * FF_CHUNK
        gate = jnp.dot(h, wg_ref[:, c0:c1], preferred_element_type=F32)
        up = jnp.dot(h, wu_ref[:, c0:c1], preferred_element_type=F32)
        acc = acc + jnp.dot(_bf(_silu(gate) * up), wd_ref[c0:c1, :], preferred_element_type=F32)
    o_ref[...] = x + _rmsnorm(acc, postw_ref[...])


def _ffn_call(x2d, pre_norm, post_norm, wg, wu, wd, layer):
    m = x2d.shape[0]

    def wspec(shape):
        return pl.BlockSpec((None,) + shape, lambda i: (layer, 0, 0), pipeline_mode=pl.Buffered(1))

    return pl.pallas_call(
        _ffn_kernel,
        grid=(m // TM_FFN,),
        in_specs=[
            pl.BlockSpec((TM_FFN, D_MODEL), lambda i: (i, 0)),
            pl.BlockSpec((None, 1, D_MODEL), lambda i: (layer, 0, 0)),
            pl.BlockSpec((None, 1, D_MODEL), lambda i: (layer, 0, 0)),
            wspec((D_MODEL, FF)),
            wspec((D_MODEL, FF)),
            wspec((FF, D_MODEL)),
        ],
        out_specs=pl.BlockSpec((TM_FFN, D_MODEL), lambda i: (i, 0)),
        out_shape=jax.ShapeDtypeStruct(x2d.shape, F32),
        compiler_params=pltpu.CompilerParams(
            dimension_semantics=("arbitrary",), vmem_limit_bytes=VMEM_LIMIT),
        name="ffn",
    )(x2d, pre_norm, post_norm, wg, wu, wd)


def _regroup_w_in(w_in):
    dt0 = W_SPLIT
    g0 = dt0 + SSD_HEADS
    b0 = g0 + GDN_CONV_DIM + GDN_DIM
    small = jnp.concatenate([w_in[:, :, dt0:g0], w_in[:, :, b0:]], axis=-1)
    return (w_in[:, :, :dt0].astype(BF16), w_in[:, :, g0:b0].astype(BF16),
            jnp.swapaxes(small, 1, 2).astype(BF16))


def _pack_small(ssd_vals, gdn_vals):
    depth = ssd_vals.shape[0]
    mid = jnp.zeros((depth, GDN_HEADS), F32)
    return jnp.concatenate([ssd_vals.astype(F32), mid, gdn_vals.astype(F32)], axis=-1)[:, :, None]


def kernel(x, pre_mix_norm, post_mix_norm, pre_ffn_norm, post_ffn_norm, w_in, w_out, attn_sinks, ssd_conv_w, ssd_conv_b, ssd_dt_bias, ssd_A_log, ssd_D, ssd_norm_w, gdn_conv_w, gdn_dt_bias, gdn_A_log, gdn_norm_w, ffn_w_gate, ffn_w_up, ffn_w_down):
    bsz, seq, _ = x.shape
    depth = w_in.shape[0]
    assert seq % TB == 0 and (bsz * seq) % TM_FFN == 0 and bsz % NB == 0

    w_in_a, w_in_b, w_small_t = _regroup_w_in(w_in)
    vec3 = lambda a: a.reshape(depth, 1, a.shape[-1]).astype(F32)
    convw = jnp.concatenate([ssd_conv_w, gdn_conv_w], axis=-1).astype(F32)
    convb = jnp.concatenate([ssd_conv_b.astype(F32), jnp.zeros((depth, GDN_CONV_DIM), F32)], axis=-1)[:, None, :]
    params = (
        attn_sinks.reshape(-1).astype(F32),
        vec3(pre_mix_norm),
        w_in_a,
        w_in_b,
        w_small_t,
        convw,
        convb,
        w_out.astype(BF16),
        vec3(post_mix_norm),
        _pack_small(ssd_dt_bias, gdn_dt_bias),
        _pack_small(ssd_A_log, gdn_A_log),
        vec3(jnp.repeat(ssd_D, SSD_HEAD_DIM, axis=-1)),
        vec3(ssd_norm_w),
        vec3(jnp.tile(gdn_norm_w, (1, GDN_HEADS))),
    )
    pre_ffn = vec3(pre_ffn_norm)
    post_ffn = vec3(post_ffn_norm)
    wg = ffn_w_gate.astype(BF16)
    wu = ffn_w_up.astype(BF16)
    wd = ffn_w_down.astype(BF16)

    for layer in range(depth):
        x = _mixer_call(x, params, layer)
        x = _ffn_call(x.reshape(bsz * seq, D_MODEL), pre_ffn, post_ffn, wg, wu, wd, layer).reshape(bsz, seq, D_MODEL)
    return x
```

```python
import functools

import jax
import jax.numpy as jnp
from jax import lax
from jax.experimental import pallas as pl
from jax.experimental.pallas import tpu as pltpu

F32 = jnp.float32
BF16 = jnp.bfloat16

D_MODEL = 1024
CHUNK = 64
EPS = 1e-6
CONV_K = 4
MASKED = -1e30

SSD_HEADS = 8
SSD_HEAD_DIM = 64
SSD_DIM = SSD_HEADS * SSD_HEAD_DIM
SSD_GROUPS = 2
SSD_STATE = 128
SSD_HPG = SSD_HEADS // SSD_GROUPS
SSD_CONV_DIM = SSD_DIM + 2 * SSD_GROUPS * SSD_STATE

ATT_HEADS = 4
ATT_KV_HEADS = 2
ATT_HEAD_DIM = 64
ATT_DIM = ATT_HEADS * ATT_HEAD_DIM
ATT_KV_DIM = ATT_KV_HEADS * ATT_HEAD_DIM
WINDOW = 128
WIN_CHUNKS = WINDOW // CHUNK

GDN_HEADS = 4
GDN_HEAD_K = 64
GDN_HEAD_V = 64
GDN_KDIM = GDN_HEADS * GDN_HEAD_K
GDN_DIM = GDN_HEADS * GDN_HEAD_V
GDN_CONV_DIM = 2 * GDN_KDIM + GDN_DIM

FF = 2816
LANES = 128
SUBLANES = 8
BF16_ROWS = 16

PROJ_WIDTHS = (ATT_DIM, ATT_KV_DIM, ATT_KV_DIM, SSD_DIM, SSD_CONV_DIM, GDN_CONV_DIM, GDN_DIM)
PROJ_OFFS = tuple(sum(PROJ_WIDTHS[:i]) for i in range(len(PROJ_WIDTHS)))
PROJ_COLS = sum(PROJ_WIDTHS)
Q_OFF, K_OFF, V_OFF, Z_OFF, XBC_OFF, GQKV_OFF, GZ_OFF = PROJ_OFFS
CONV_LO = XBC_OFF
CONV_HI = GZ_OFF
CONV_COLS = CONV_HI - CONV_LO
W_SPLIT = GQKV_OFF
SMALL_ROWS = SSD_HEADS + 2 * GDN_HEADS
SMALL_DT = 0
SMALL_BETA = SSD_HEADS
SMALL_DECAY = SSD_HEADS + GDN_HEADS

TM_FFN = 512
FF_CHUNK = 256
TB = 256
NB = 2
PROJ_CHUNK = 256
ATT_QB = 128
GDN_SPAN = 128
FILL_TOP = 0
FILL_GDN_PREP = 20
FILL_GDN_LEVEL = 5
FILL_GDN_CHAIN = 0
FILL_ATT = 0
FILL_SSD = 0
VMEM_LIMIT = 56 * 1024 * 1024


def _bf(x):
    return x.astype(BF16)


def _mm(a, b):
    return jnp.dot(_bf(a), _bf(b), preferred_element_type=F32)


def _mm_nt(a, b):
    return lax.dot_general(_bf(a), _bf(b), (((1,), (1,)), ((), ())), preferred_element_type=F32)


def _mm_tn(a, b):
    return lax.dot_general(_bf(a), _bf(b), (((0,), (0,)), ((), ())), preferred_element_type=F32)


def _split(x, terms):
    out = []
    for _ in range(terms - 1):
        t = x.astype(BF16)
        out.append(t)
        x = x - t.astype(F32)
    out.append(x.astype(BF16))
    return out


def _cumsum_lanes(x_t, m_bf16):
    return sum(jnp.dot(t, m_bf16, preferred_element_type=F32) for t in _split(x_t, 3))


def _expand_heads(x_t, r_bf16):
    terms = jnp.concatenate(_split(x_t, 2), axis=0)
    return lax.dot_general(terms, jnp.concatenate([r_bf16, r_bf16], axis=0), (((0,), (0,)), ((), ())),
                           preferred_element_type=F32)


def _columns(x_t):
    pad = jnp.zeros((LANES - x_t.shape[0], x_t.shape[1]), F32)
    return jnp.concatenate([x_t, pad], axis=0).T


def _sigmoid(x):
    return 0.5 + 0.5 * jnp.tanh(0.5 * x)


def _silu(x):
    hx = 0.5 * x
    return hx + hx * jnp.tanh(hx)


def _softplus(x):
    return jnp.maximum(x, 0.0) + jnp.log1p(jnp.exp(-jnp.abs(x)))


def _rmsnorm(x, w):
    return x * lax.rsqrt(jnp.mean(x * x, axis=-1, keepdims=True) + EPS) * w


def _iota2(shape):
    return (lax.broadcasted_iota(jnp.int32, shape, 0), lax.broadcasted_iota(jnp.int32, shape, 1))


def _head_expand_matrix(src_off, n_heads, width):
    ri, ci = _iota2((SMALL_ROWS, n_heads * width))
    return jnp.where(ri - src_off == ci // width, 1.0, 0.0).astype(BF16)


def _block_ones(n, seg):
    ri, ci = _iota2((n, n))
    return jnp.where((ri // seg) == (ci // seg), 1.0, 0.0).astype(BF16)


def _shift_rows(x, tail, j):
    row = lax.broadcasted_iota(jnp.int32, (SUBLANES, x.shape[1]), 0)
    xs = pltpu.roll(x, j, axis=0)
    first = jnp.where(row < j, pltpu.roll(tail, j, axis=0), xs[0:SUBLANES])
    return jnp.concatenate([first, xs[SUBLANES:]], axis=0)


def _causal_conv(x, tail, w):
    assert CONV_K == 4
    x1 = _shift_rows(x, tail, 1)
    u = x * w[3:4, :] + x1 * w[2:3, :]
    v = x * w[1:2, :] + x1 * w[0:1, :]
    v_tail = tail * w[1:2, :] + pltpu.roll(tail, 1, axis=0) * w[0:1, :]
    return u + _shift_rows(v, v_tail, 2)


def _proj_pieces(x_ref, nw_ref, w_refs, wsmall_ref, convw_ref, convb_ref, tail_ref, proj_ref, small_ref):
    cell = {}

    def norm():
        x = x_ref[...].reshape(NB * TB, D_MODEL)
        cell["h"] = _bf(_rmsnorm(x, nw_ref[...]))

    def dot(key):
        c0, b = key

        def run():
            w_ref, w0 = (w_refs[0], c0) if c0 < W_SPLIT else (w_refs[1], c0 - W_SPLIT)
            cell[key] = jnp.dot(cell["h"][b * TB:(b + 1) * TB], w_ref[:, w0:w0 + PROJ_CHUNK],
                                preferred_element_type=F32)
        return run

    def post(key):
        c0, b = key

        def run():
            y = cell.pop(key)
            if CONV_LO <= c0 < CONV_HI:
                cols = slice(c0 - CONV_LO, c0 - CONV_LO + PROJ_CHUNK)
                conv = _causal_conv(y, tail_ref[b, :, cols], convw_ref[:, cols])
                tail_ref[b, :, cols] = y[TB - SUBLANES:TB]
                y = _silu(conv + convb_ref[:, cols])
            proj_ref[b * TB:(b + 1) * TB, c0:c0 + PROJ_CHUNK] = y
        return run

    def small():
        small_ref[...] = lax.dot_general(wsmall_ref[...], cell["h"], (((1,), (1,)), ((), ())),
                                          preferred_element_type=F32)

    keys = [(c0, b) for c0 in range(0, PROJ_COLS, PROJ_CHUNK) for b in range(NB)]
    seq = [norm, dot(keys[0])]
    for prev, key in zip(keys[:-1], keys[1:]):
        seq += [dot(key), post(prev)]
    return seq + [post(keys[-1]), small]


def _attention(q, k, v, b, kprev_ref, vprev_ref, sinks_ref, layer, t, mix_ref, fill):
    nk = ATT_QB + WINDOW
    kext = jnp.concatenate([kprev_ref[b], k], axis=0)
    vext = jnp.concatenate([vprev_ref[b], v], axis=0)
    kprev_ref[b] = k[TB - WINDOW:TB]
    vprev_ref[b] = v[TB - WINDOW:TB]
    qi, kj = _iota2((ATT_QB, nk))
    dist = jnp.abs(qi + WINDOW - kj).astype(F32)
    qc = qi // CHUNK
    kc = kj // CHUNK - WIN_CHUNKS
    in_band = jnp.where(kc <= qc, jnp.where(kc >= qc - WIN_CHUNKS, 1, 0), 0)
    grp = ATT_HEADS // ATT_KV_HEADS
    for a in range(TB // ATT_QB):
        first_chunk = t * (TB // CHUNK) + a * (ATT_QB // CHUNK)
        valid = (in_band * jnp.where(kc + first_chunk >= 0, 1, 0)) > 0
        rows = slice(a * ATT_QB, (a + 1) * ATT_QB)
        out_rows = slice(b * TB + a * ATT_QB, b * TB + (a + 1) * ATT_QB)
        krows = slice(a * ATT_QB, a * ATT_QB + nk)
        for kh in range(ATT_KV_HEADS):
            kk = kext[krows, kh * ATT_HEAD_DIM:(kh + 1) * ATT_HEAD_DIM]
            vv = vext[krows, kh * ATT_HEAD_DIM:(kh + 1) * ATT_HEAD_DIM]
            for gi in range(grp):
                h = kh * grp + gi
                slope = 2.0 ** (-8.0 * (h + 1) / ATT_HEADS)
                sink = sinks_ref[layer * ATT_HEADS + h]
                qh = q[rows, h * ATT_HEAD_DIM:(h + 1) * ATT_HEAD_DIM]
                s = _mm_nt(qh, kk) * (ATT_HEAD_DIM ** -0.5) - slope * dist
                s = jnp.where(valid, s, MASKED)
                m = jnp.maximum(jnp.max(s, axis=-1, keepdims=True), sink)
                p = jnp.exp(s - m)
                den = jnp.sum(p, axis=-1, keepdims=True) + jnp.exp(sink - m)
                mix_ref[out_rows, h * ATT_HEAD_DIM:(h + 1) * ATT_HEAD_DIM] = _mm(p, vv) / den
            fill(FILL_ATT)


def _ssd(xbc, z, sp_t, cs_t, dvec, norm_w, b, state_ref, mix_ref, fill):
    xs = xbc[:, :SSD_DIM]
    bm = xbc[:, SSD_DIM:SSD_DIM + SSD_GROUPS * SSD_STATE]
    cm = xbc[:, SSD_DIM + SSD_GROUPS * SSD_STATE:]
    expand = _head_expand_matrix(SMALL_DT, SSD_HEADS, SSD_HEAD_DIM)
    cs = _columns(cs_t)
    cs_last_t = cs_t[:, TB - 1:TB]
    xc = xs * _expand_heads(sp_t, expand)
    xcd = xc * _expand_heads(jnp.exp(cs_last_t - cs_t), expand)
    ecs = _expand_heads(jnp.exp(cs_t), expand)
    elast = _expand_heads(jnp.broadcast_to(jnp.exp(cs_last_t), (SMALL_ROWS, SMALL_ROWS)), expand)[0:1]
    ri, ci = _iota2((TB, TB))
    tri = ci <= ri
    gw = SSD_HPG * SSD_HEAD_DIM
    ys = []
    for g in range(SSD_GROUPS):
        bg = bm[:, g * SSD_STATE:(g + 1) * SSD_STATE]
        cg = cm[:, g * SSD_STATE:(g + 1) * SSD_STATE]
        cb = _mm_nt(cg, bg)
        prev = state_ref[b, g]
        y_off = _mm(cg, prev) * ecs[:, g * gw:(g + 1) * gw]
        state_ref[b, g] = prev * elast[:, g * gw:(g + 1) * gw] + _mm_tn(bg, xcd[:, g * gw:(g + 1) * gw])
        yd = []
        for hh in range(SSD_HPG):
            h = g * SSD_HPG + hh
            diff = cs[:, h:h + 1] - cs_t[h:h + 1, :]
            lmat = jnp.exp(jnp.where(tri, diff, MASKED))
            yd.append(_mm(cb * lmat, xc[:, h * SSD_HEAD_DIM:(h + 1) * SSD_HEAD_DIM]))
            fill(FILL_SSD)
        ys.append(jnp.concatenate(yd, axis=1) + y_off)
    y = jnp.concatenate(ys, axis=1) + xs * dvec
    gated = y * _silu(z)
    rows = slice(b * TB, (b + 1) * TB)
    for g in range(SSD_GROUPS):
        gg = gated[:, g * gw:(g + 1) * gw]
        gg = gg * lax.rsqrt(jnp.mean(gg * gg, axis=-1, keepdims=True) + EPS)
        mix_ref[rows, ATT_DIM + g * gw:ATT_DIM + (g + 1) * gw] = gg * norm_w[:, g * gw:(g + 1) * gw]


def _gdn(per_b, norm_w, state_ref, mix_ref, fill):
    nb = len(per_b)
    ones = _block_ones(GDN_KDIM, GDN_HEAD_K)

    def seg_sum(x2):
        return jnp.dot(_bf(x2), ones, preferred_element_type=F32)

    exp_beta = _head_expand_matrix(SMALL_BETA, GDN_HEADS, GDN_HEAD_K)
    exp_dec = _head_expand_matrix(SMALL_DECAY, GDN_HEADS, GDN_HEAD_K)
    ri, ci = _iota2((GDN_SPAN, GDN_SPAN))
    same = (ri // CHUNK) == (ci // CHUNK)
    tri = jnp.logical_and(same, ci <= ri)
    strict = jnp.logical_and(same, ci < ri)
    heads = range(GDN_HEADS)
    n_chunks = TB // CHUNK
    n_sp = TB // GDN_SPAN
    hs = [slice(h * GDN_HEAD_K, (h + 1) * GDN_HEAD_K) for h in heads]
    rs = [slice(c * CHUNK, (c + 1) * CHUNK) for c in range(n_chunks)]
    spans = [slice(a * GDN_SPAN, (a + 1) * GDN_SPAN) for a in range(n_sp)]
    bh = [(b, h) for b in range(nb) for h in heads]

    p, sol, qk, qeg, kd = {}, {}, {}, [], []
    for b, (gq, _, beta_t, gc_t, g_end_t) in enumerate(per_b):
        q = gq[:, :GDN_KDIM]
        k = gq[:, GDN_KDIM:2 * GDN_KDIM]
        v = gq[:, 2 * GDN_KDIM:]
        qn = q * lax.rsqrt(seg_sum(q * q) + EPS) * (GDN_HEAD_K ** -0.5)
        kn = k * lax.rsqrt(seg_sum(k * k) + EPS)
        beta_b = _expand_heads(beta_t, exp_beta)
        eg_b = _expand_heads(jnp.exp(gc_t), exp_dec)
        kbeta = kn * beta_b
        vbeta = v * beta_b
        kbg = kbeta * eg_b
        qeg.append(qn * eg_b)
        kd.append(kn * _expand_heads(jnp.exp(g_end_t - gc_t), exp_dec))
        gc = _columns(gc_t)
        for a, sp in enumerate(spans):
            for h in heads:
                row = SMALL_DECAY + h
                diff = gc[sp, row:row + 1] - gc_t[row:row + 1, sp]
                decay = jnp.exp(jnp.where(tri, diff, MASKED))
                u = (b, a, h)
                p[u] = jnp.where(strict, _mm_nt(kbeta[sp, hs[h]], kn[sp, hs[h]]) * (-decay), 0.0)
                qk[u] = _mm_nt(qn[sp, hs[h]], kn[sp, hs[h]]) * decay
                sol[u] = jnp.concatenate([vbeta[sp, hs[h]], kbg[sp, hs[h]]], axis=1)
    units = list(p.keys())
    fill(FILL_GDN_PREP)
    steps = CHUNK.bit_length() - 1
    for j in range(steps):
        skip = (1 << j) // BF16_ROWS * BF16_ROWS
        keep = CHUNK - skip

        def live_rows(x):
            if skip == 0:
                return x
            return jnp.concatenate([x[c * CHUNK + skip:(c + 1) * CHUNK] for c in range(GDN_SPAN // CHUNK)], axis=0)

        def all_rows(r):
            if skip == 0:
                return r
            zeros = jnp.zeros((skip, r.shape[1]), r.dtype)
            return jnp.concatenate(
                [piece for c in range(GDN_SPAN // CHUNK) for piece in (zeros, r[c * keep:(c + 1) * keep])], axis=0)

        pb = {u: _bf(p[u]) for u in units}
        sb = {u: _bf(sol[u]) for u in units}
        if j + 1 < steps:
            res = {u: all_rows(jnp.dot(live_rows(pb[u]), jnp.concatenate([pb[u], sb[u]], axis=1),
                                       preferred_element_type=F32)) for u in units}
            p = {u: res[u][:, :GDN_SPAN] for u in units}
            sol = {u: sol[u] + res[u][:, GDN_SPAN:] for u in units}
        else:
            sol = {u: sol[u] + all_rows(jnp.dot(live_rows(pb[u]), sb[u], preferred_element_type=F32))
                   for u in units}
        fill(FILL_GDN_LEVEL)
    qks_u = {u: _mm(qk[u], sol[u]) for u in units}
    solh = {(b, h): jnp.concatenate([sol[(b, a, h)] for a in range(n_sp)], axis=0) for b, h in bh}
    qks = {(b, h): jnp.concatenate([qks_u[(b, a, h)] for a in range(n_sp)], axis=0) for b, h in bh}
    qeff = {(b, h): qeg[b][:, hs[h]] - qks[(b, h)][:, GDN_HEAD_V:] for b, h in bh}
    kuw = [{(b, h): _mm_tn(kd[b][rs[c], hs[h]], solh[(b, h)][rs[c]]) for b, h in bh} for c in range(n_chunks)]
    state = {(b, h): state_ref[b, h] for b, h in bh}
    states = []
    for c in range(n_chunks):
        states.append(state)
        col = (c + 1) * CHUNK - 1
        e_end = [jnp.exp(per_b[b][3][:, col:col + 1]) for b in range(nb)]
        state = {(b, h): state[(b, h)] * e_end[b][SMALL_DECAY + h:SMALL_DECAY + h + 1, :]
                 - _mm(kuw[c][(b, h)][:, GDN_HEAD_V:], state[(b, h)]) + kuw[c][(b, h)][:, :GDN_HEAD_V]
                 for b, h in bh}
        fill(FILL_GDN_CHAIN)
    for b, h in bh:
        state_ref[b, h] = state[(b, h)]
    outs = [{(b, h): _mm(qeff[(b, h)][rs[c]], states[c][(b, h)]) + qks[(b, h)][rs[c], :GDN_HEAD_V]
             for b, h in bh} for c in range(n_chunks)]
    for b in range(nb):
        o = jnp.concatenate([jnp.concatenate([outs[c][(b, h)] for h in heads], axis=1)
                             for c in range(n_chunks)], axis=0)
        ms = seg_sum(o * o) * (1.0 / GDN_HEAD_V)
        mix_ref[b * TB:(b + 1) * TB, ATT_DIM + SSD_DIM:] = o * lax.rsqrt(ms + EPS) * norm_w * _silu(per_b[b][1])


def _mixer_kernel(layer, n_blk,
                  sinks_ref, x_ref, xnext_ref, prew_ref, wina_ref, winb_ref, wsmall_ref, convw_ref, convb_ref,
                  wout_ref, postw_ref, bias_ref, alog_ref, dvec_ref, snormw_ref, gnormw_ref,
                  o_ref,
                  proj_ref, small_ref, pnext_ref, snext_ref, tail_ref, kprev_ref, vprev_ref, sstate_ref, gstate_ref, mix_ref):
    t = pl.program_id(1)
    proj_args = (prew_ref, (wina_ref, winb_ref), wsmall_ref, convw_ref, convb_ref, tail_ref, pnext_ref, snext_ref)

    @pl.when(t == 0)
    def _():
        tail_ref[...] = jnp.zeros_like(tail_ref)
        kprev_ref[...] = jnp.zeros_like(kprev_ref)
        vprev_ref[...] = jnp.zeros_like(vprev_ref)
        sstate_ref[...] = jnp.zeros_like(sstate_ref)
        gstate_ref[...] = jnp.zeros_like(gstate_ref)
        for piece in _proj_pieces(x_ref, *proj_args):
            piece()

    proj_ref[...] = pnext_ref[...]
    small_ref[...] = snext_ref[...]
    pieces = iter(_proj_pieces(xnext_ref, *proj_args))

    def fill(n):
        for _ in range(n):
            piece = next(pieces, None)
            if piece is not None:
                piece()

    def group(b, off, width):
        return proj_ref[b * TB:(b + 1) * TB, off:off + width]

    fill(FILL_TOP)
    row = lax.broadcasted_iota(jnp.int32, (SMALL_ROWS, 1), 0)
    is_decay = jnp.where(row < SMALL_BETA, 1.0, jnp.where(row >= SMALL_DECAY, 1.0, 0.0))
    neg_a = -jnp.exp(alog_ref[...]) * is_decay
    ri, ci = _iota2((TB, TB))
    same = (ri // CHUNK) == (ci // CHUNK)
    before = jnp.where(ri <= ci, 1.0, 0.0)
    m_block = before.astype(BF16)
    m_chunk = jnp.where(same, before, 0.0).astype(BF16)
    m_whole_chunk = jnp.where(same, 1.0, 0.0).astype(BF16)
    gdn_in, ssd_in = [], []
    for b in range(NB):
        small_t = small_ref[:, b * TB:(b + 1) * TB]
        sp_t = _softplus(small_t + bias_ref[...])
        da_t = sp_t * neg_a
        ssd_in.append((sp_t, _cumsum_lanes(da_t, m_block)))
        gc_t = _cumsum_lanes(da_t, m_chunk)
        g_end_t = _cumsum_lanes(da_t, m_whole_chunk)
        gdn_in.append((group(b, GQKV_OFF, GDN_CONV_DIM), group(b, GZ_OFF, GDN_DIM), _sigmoid(small_t),
                       gc_t, g_end_t))
    _gdn(gdn_in, gnormw_ref[...], gstate_ref, mix_ref, fill)
    for b in range(NB):
        _attention(group(b, Q_OFF, ATT_DIM), group(b, K_OFF, ATT_KV_DIM), group(b, V_OFF, ATT_KV_DIM),
                   b, kprev_ref, vprev_ref, sinks_ref, layer, t, mix_ref, fill)
    for b in range(NB):
        sp_t, cs_t = ssd_in[b]
        _ssd(group(b, XBC_OFF, SSD_CONV_DIM), group(b, Z_OFF, SSD_DIM), sp_t, cs_t, dvec_ref[...],
             snormw_ref[...], b, sstate_ref, mix_ref, fill)
    fill(PROJ_COLS)

    mixed = jnp.dot(_bf(mix_ref[...]), wout_ref[...], preferred_element_type=F32)
    y = _rmsnorm(mixed, postw_ref[...])
    for b in range(NB):
        o_ref[b] = x_ref[b] + y[b * TB:(b + 1) * TB]


def _mixer_call(x3d, params, layer):
    bsz, seq, _ = x3d.shape
    n_blk = seq // TB
    grid = (bsz // NB, n_blk)

    def par_spec(arr, **kw):
        shape = arr.shape[1:]
        return pl.BlockSpec((None,) + shape, lambda b, t: (layer,) + (0,) * len(shape), **kw)

    sinks, pre_norm, w_in_a, w_in_b, w_small_t, convw, convb, wout = params[:8]
    rest = params[8:]
    in_specs = ([pl.BlockSpec(memory_space=pltpu.SMEM),
                 pl.BlockSpec((NB, TB, D_MODEL), lambda b, t: (b, t, 0)),
                 pl.BlockSpec((NB, TB, D_MODEL), lambda b, t: (b, jnp.minimum(t + 1, n_blk - 1), 0)),
                 par_spec(pre_norm),
                 par_spec(w_in_a, pipeline_mode=pl.Buffered(1)),
                 par_spec(w_in_b, pipeline_mode=pl.Buffered(1)),
                 par_spec(w_small_t),
                 par_spec(convw),
                 par_spec(convb),
                 par_spec(wout, pipeline_mode=pl.Buffered(1))]
                + [par_spec(p) for p in rest])
    scratch = [
        pltpu.VMEM((NB * TB, PROJ_COLS), F32),
        pltpu.VMEM((SMALL_ROWS, NB * TB), F32),
        pltpu.VMEM((NB * TB, PROJ_COLS), F32),
        pltpu.VMEM((SMALL_ROWS, NB * TB), F32),
        pltpu.VMEM((NB, SUBLANES, CONV_COLS), F32),
        pltpu.VMEM((NB, WINDOW, ATT_KV_DIM), F32),
        pltpu.VMEM((NB, WINDOW, ATT_KV_DIM), F32),
        pltpu.VMEM((NB, SSD_GROUPS, SSD_STATE, SSD_HPG * SSD_HEAD_DIM), F32),
        pltpu.VMEM((NB, GDN_HEADS, GDN_HEAD_K, GDN_HEAD_V), F32),
        pltpu.VMEM((NB * TB, D_MODEL), F32),
    ]
    return pl.pallas_call(
        functools.partial(_mixer_kernel, layer, n_blk),
        grid=grid,
        in_specs=in_specs,
        out_specs=pl.BlockSpec((NB, TB, D_MODEL), lambda b, t: (b, t, 0)),
        out_shape=jax.ShapeDtypeStruct(x3d.shape, F32),
        scratch_shapes=scratch,
        compiler_params=pltpu.CompilerParams(
            dimension_semantics=("arbitrary", "arbitrary"), vmem_limit_bytes=VMEM_LIMIT),
        name="mixer",
    )(sinks, x3d, x3d, pre_norm, w_in_a, w_in_b, w_small_t, convw, convb, wout, *rest)


def _ffn_kernel(x_ref, prew_ref, postw_ref, wg_ref, wu_ref, wd_ref, o_ref):
    x = x_ref[...]
    h = _bf(_rmsnorm(x, prew_ref[...]))
    acc = jnp.zeros((TM_FFN, D_MODEL), F32)
    for c in range(FF // FF_CHUNK):
        c0, c1 = c * FF_CHUNK, (c + 1) * FF_CHUNK
        gate = jnp.dot(h, wg_ref[:, c0:c1], preferred_element_type=F32)
        up = jnp.dot(h, wu_ref[:, c0:c1], preferred_element_type=F32)
        acc = acc + jnp.dot(_bf(_silu(gate) * up), wd_ref[c0:c1, :], preferred_element_type=F32)
    o_ref[...] = x + _rmsnorm(acc, postw_ref[...])


def _ffn_call(x2d, pre_norm, post_norm, wg, wu, wd, layer):
    m = x2d.shape[0]

    def wspec(shape):
        return pl.BlockSpec((None,) + shape, lambda i: (layer, 0, 0), pipeline_mode=pl.Buffered(1))

    return pl.pallas_call(
        _ffn_kernel,
        grid=(m // TM_FFN,),
        in_specs=[
            pl.BlockSpec((TM_FFN, D_MODEL), lambda i: (i, 0)),
            pl.BlockSpec((None, 1, D_MODEL), lambda i: (layer, 0, 0)),
            pl.BlockSpec((None, 1, D_MODEL), lambda i: (layer, 0, 0)),
            wspec((D_MODEL, FF)),
            wspec((D_MODEL, FF)),
            wspec((FF, D_MODEL)),
        ],
        out_specs=pl.BlockSpec((TM_FFN, D_MODEL), lambda i: (i, 0)),
        out_shape=jax.ShapeDtypeStruct(x2d.shape, F32),
        compiler_params=pltpu.CompilerParams(
            dimension_semantics=("arbitrary",), vmem_limit_bytes=VMEM_LIMIT),
        name="ffn",
    )(x2d, pre_norm, post_norm, wg, wu, wd)


def _regroup_w_in(w_in):
    dt0 = W_SPLIT
    g0 = dt0 + SSD_HEADS
    b0 = g0 + GDN_CONV_DIM + GDN_DIM
    small = jnp.concatenate([w_in[:, :, dt0:g0], w_in[:, :, b0:]], axis=-1)
    return (w_in[:, :, :dt0].astype(BF16), w_in[:, :, g0:b0].astype(BF16),
            jnp.swapaxes(small, 1, 2).astype(BF16))


def _pack_small(ssd_vals, gdn_vals):
    depth = ssd_vals.shape[0]
    mid = jnp.zeros((depth, GDN_HEADS), F32)
    return jnp.concatenate([ssd_vals.astype(F32), mid, gdn_vals.astype(F32)], axis=-1)[:, :, None]


def kernel(x, pre_mix_norm, post_mix_norm, pre_ffn_norm, post_ffn_norm, w_in, w_out, attn_sinks, ssd_conv_w, ssd_conv_b, ssd_dt_bias, ssd_A_log, ssd_D, ssd_norm_w, gdn_conv_w, gdn_dt_bias, gdn_A_log, gdn_norm_w, ffn_w_gate, ffn_w_up, ffn_w_down):
    bsz, seq, _ = x.shape
    depth = w_in.shape[0]
    assert seq % TB == 0 and (bsz * seq) % TM_FFN == 0 and bsz % NB == 0

    w_in_a, w_in_b, w_small_t = _regroup_w_in(w_in)
    vec3 = lambda a: a.reshape(depth, 1, a.shape[-1]).astype(F32)
    convw = jnp.concatenate([ssd_conv_w, gdn_conv_w], axis=-1).astype(F32)
    convb = jnp.concatenate([ssd_conv_b.astype(F32), jnp.zeros((depth, GDN_CONV_DIM), F32)], axis=-1)[:, None, :]
    params = (
        attn_sinks.reshape(-1).astype(F32),
        vec3(pre_mix_norm),
        w_in_a,
        w_in_b,
        w_small_t,
        convw,
        convb,
        w_out.astype(BF16),
        vec3(post_mix_norm),
        _pack_small(ssd_dt_bias, gdn_dt_bias),
        _pack_small(ssd_A_log, gdn_A_log),
        vec3(jnp.repeat(ssd_D, SSD_HEAD_DIM, axis=-1)),
        vec3(ssd_norm_w),
        vec3(jnp.tile(gdn_norm_w, (1, GDN_HEADS))),
    )
    pre_ffn = vec3(pre_ffn_norm)
    post_ffn = vec3(post_ffn_norm)
    wg = ffn_w_gate.astype(BF16)
    wu = ffn_w_up.astype(BF16)
    wd = ffn_w_down.astype(BF16)

    for layer in range(depth):
        x = _mixer_call(x, params, layer)
        x = _ffn_call(x.reshape(bsz * seq, D_MODEL), pre_ffn, post_ffn, wg, wu, wd, layer).reshape(bsz, seq, D_MODEL)
    return x
```

```python
import functools

import jax
import jax.numpy as jnp
from jax import lax
from jax.experimental import pallas as pl
from jax.experimental.pallas import tpu as pltpu

F32 = jnp.float32
BF16 = jnp.bfloat16

D_MODEL = 1024
CHUNK = 64
EPS = 1e-6
CONV_K = 4
MASKED = -1e30

SSD_HEADS = 8
SSD_HEAD_DIM = 64
SSD_DIM = SSD_HEADS * SSD_HEAD_DIM
SSD_GROUPS = 2
SSD_STATE = 128
SSD_HPG = SSD_HEADS // SSD_GROUPS
SSD_CONV_DIM = SSD_DIM + 2 * SSD_GROUPS * SSD_STATE

ATT_HEADS = 4
ATT_KV_HEADS = 2
ATT_HEAD_DIM = 64
ATT_DIM = ATT_HEADS * ATT_HEAD_DIM
ATT_KV_DIM = ATT_KV_HEADS * ATT_HEAD_DIM
WINDOW = 128
WIN_CHUNKS = WINDOW // CHUNK

GDN_HEADS = 4
GDN_HEAD_K = 64
GDN_HEAD_V = 64
GDN_KDIM = GDN_HEADS * GDN_HEAD_K
GDN_DIM = GDN_HEADS * GDN_HEAD_V
GDN_CONV_DIM = 2 * GDN_KDIM + GDN_DIM

FF = 2816
LANES = 128
SUBLANES = 8
BF16_ROWS = 16

PROJ_WIDTHS = (ATT_DIM, ATT_KV_DIM, ATT_KV_DIM, SSD_DIM, SSD_CONV_DIM, GDN_CONV_DIM, GDN_DIM)
PROJ_OFFS = tuple(sum(PROJ_WIDTHS[:i]) for i in range(len(PROJ_WIDTHS)))
PROJ_COLS = sum(PROJ_WIDTHS)
Q_OFF, K_OFF, V_OFF, Z_OFF, XBC_OFF, GQKV_OFF, GZ_OFF = PROJ_OFFS
CONV_LO = XBC_OFF
CONV_HI = GZ_OFF
CONV_COLS = CONV_HI - CONV_LO
W_SPLIT = GQKV_OFF
SMALL_ROWS = SSD_HEADS + 2 * GDN_HEADS
SMALL_DT = 0
SMALL_BETA = SSD_HEADS
SMALL_DECAY = SSD_HEADS + GDN_HEADS

TM_FFN = 512
FF_CHUNK = 256
TB = 256
NB = 2
PROJ_CHUNK = 256
ATT_QB = 128
GDN_SPAN = 128
SSD_CHUNK = 128
FILL_TOP = 0
FILL_GDN_PREP = 20
FILL_GDN_LEVEL = 5
FILL_GDN_CHAIN = 0
FILL_ATT = 0
FILL_SSD = 0
VMEM_LIMIT = 56 * 1024 * 1024


def _bf(x):
    return x.astype(BF16)


def _mm(a, b):
    return jnp.dot(_bf(a), _bf(b), preferred_element_type=F32)


def _mm_nt(a, b):
    return lax.dot_general(_bf(a), _bf(b), (((1,), (1,)), ((), ())), preferred_element_type=F32)


def _mm_tn(a, b):
    return lax.dot_general(_bf(a), _bf(b), (((0,), (0,)), ((), ())), preferred_element_type=F32)


def _split(x, terms):
    out = []
    for _ in range(terms - 1):
        t = x.astype(BF16)
        out.append(t)
        x = x - t.astype(F32)
    out.append(x.astype(BF16))
    return out


def _cumsum_lanes(x_t, m_bf16):
    return sum(jnp.dot(t, m_bf16, preferred_element_type=F32) for t in _split(x_t, 3))


def _expand_heads(x_t, r_bf16):
    terms = jnp.concatenate(_split(x_t, 2), axis=0)
    return lax.dot_general(terms, jnp.concatenate([r_bf16, r_bf16], axis=0), (((0,), (0,)), ((), ())),
                           preferred_element_type=F32)


def _columns(x_t):
    pad = jnp.zeros((LANES - x_t.shape[0], x_t.shape[1]), F32)
    return jnp.concatenate([x_t, pad], axis=0).T


def _sigmoid(x):
    return 0.5 + 0.5 * jnp.tanh(0.5 * x)


def _silu(x):
    hx = 0.5 * x
    return hx + hx * jnp.tanh(hx)


def _softplus(x):
    return jnp.maximum(x, 0.0) + jnp.log1p(jnp.exp(-jnp.abs(x)))


def _rmsnorm(x, w):
    return x * lax.rsqrt(jnp.mean(x * x, axis=-1, keepdims=True) + EPS) * w


def _iota2(shape):
    return (lax.broadcasted_iota(jnp.int32, shape, 0), lax.broadcasted_iota(jnp.int32, shape, 1))


def _head_expand_matrix(src_off, n_heads, width):
    ri, ci = _iota2((SMALL_ROWS, n_heads * width))
    return jnp.where(ri - src_off == ci // width, 1.0, 0.0).astype(BF16)


def _block_ones(n, seg):
    ri, ci = _iota2((n, n))
    return jnp.where((ri // seg) == (ci // seg), 1.0, 0.0).astype(BF16)


def _shift_rows(x, tail, j):
    row = lax.broadcasted_iota(jnp.int32, (SUBLANES, x.shape[1]), 0)
    xs = pltpu.roll(x, j, axis=0)
    first = jnp.where(row < j, pltpu.roll(tail, j, axis=0), xs[0:SUBLANES])
    return jnp.concatenate([first, xs[SUBLANES:]], axis=0)


def _causal_conv(x, tail, w):
    assert CONV_K == 4
    x1 = _shift_rows(x, tail, 1)
    u = x * w[3:4, :] + x1 * w[2:3, :]
    v = x * w[1:2, :] + x1 * w[0:1, :]
    v_tail = tail * w[1:2, :] + pltpu.roll(tail, 1, axis=0) * w[0:1, :]
    return u + _shift_rows(v, v_tail, 2)


def _proj_pieces(x_ref, nw_ref, w_refs, wsmall_ref, convw_ref, convb_ref, tail_ref, proj_ref, small_ref):
    cell = {}

    def norm():
        x = x_ref[...].reshape(NB * TB, D_MODEL)
        cell["h"] = _bf(_rmsnorm(x, nw_ref[...]))

    def dot(key):
        c0, b = key

        def run():
            w_ref, w0 = (w_refs[0], c0) if c0 < W_SPLIT else (w_refs[1], c0 - W_SPLIT)
            cell[key] = jnp.dot(cell["h"][b * TB:(b + 1) * TB], w_ref[:, w0:w0 + PROJ_CHUNK],
                                preferred_element_type=F32)
        return run

    def post(key):
        c0, b = key

        def run():
            y = cell.pop(key)
            if CONV_LO <= c0 < CONV_HI:
                cols = slice(c0 - CONV_LO, c0 - CONV_LO + PROJ_CHUNK)
                conv = _causal_conv(y, tail_ref[b, :, cols], convw_ref[:, cols])
                tail_ref[b, :, cols] = y[TB - SUBLANES:TB]
                y = _silu(conv + convb_ref[:, cols])
            proj_ref[b * TB:(b + 1) * TB, c0:c0 + PROJ_CHUNK] = y
        return run

    def small():
        small_ref[...] = lax.dot_general(wsmall_ref[...], cell["h"], (((1,), (1,)), ((), ())),
                                          preferred_element_type=F32)

    keys = [(c0, b) for c0 in range(0, PROJ_COLS, PROJ_CHUNK) for b in range(NB)]
    seq = [norm, dot(keys[0])]
    for prev, key in zip(keys[:-1], keys[1:]):
        seq += [dot(key), post(prev)]
    return seq + [post(keys[-1]), small]


def _attention(q, k, v, b, kprev_ref, vprev_ref, sinks_ref, layer, t, mix_ref, fill):
    nk = ATT_QB + WINDOW
    kext = jnp.concatenate([kprev_ref[b], k], axis=0)
    vext = jnp.concatenate([vprev_ref[b], v], axis=0)
    kprev_ref[b] = k[TB - WINDOW:TB]
    vprev_ref[b] = v[TB - WINDOW:TB]
    qi, kj = _iota2((ATT_QB, nk))
    dist = jnp.abs(qi + WINDOW - kj).astype(F32)
    qc = qi // CHUNK
    kc = kj // CHUNK - WIN_CHUNKS
    in_band = jnp.where(kc <= qc, jnp.where(kc >= qc - WIN_CHUNKS, 1, 0), 0)
    grp = ATT_HEADS // ATT_KV_HEADS
    for a in range(TB // ATT_QB):
        first_chunk = t * (TB // CHUNK) + a * (ATT_QB // CHUNK)
        valid = (in_band * jnp.where(kc + first_chunk >= 0, 1, 0)) > 0
        rows = slice(a * ATT_QB, (a + 1) * ATT_QB)
        out_rows = slice(b * TB + a * ATT_QB, b * TB + (a + 1) * ATT_QB)
        krows = slice(a * ATT_QB, a * ATT_QB + nk)
        for kh in range(ATT_KV_HEADS):
            kk = kext[krows, kh * ATT_HEAD_DIM:(kh + 1) * ATT_HEAD_DIM]
            vv = vext[krows, kh * ATT_HEAD_DIM:(kh + 1) * ATT_HEAD_DIM]
            for gi in range(grp):
                h = kh * grp + gi
                slope = 2.0 ** (-8.0 * (h + 1) / ATT_HEADS)
                sink = sinks_ref[layer * ATT_HEADS + h]
                qh = q[rows, h * ATT_HEAD_DIM:(h + 1) * ATT_HEAD_DIM]
                s = _mm_nt(qh, kk) * (ATT_HEAD_DIM ** -0.5) - slope * dist
                s = jnp.where(valid, s, MASKED)
                m = jnp.maximum(jnp.max(s, axis=-1, keepdims=True), sink)
                p = jnp.exp(s - m)
                den = jnp.sum(p, axis=-1, keepdims=True) + jnp.exp(sink - m)
                mix_ref[out_rows, h * ATT_HEAD_DIM:(h + 1) * ATT_HEAD_DIM] = _mm(p, vv) / den
            fill(FILL_ATT)


def _ssd(xbc, z, sp_t, cs_t, cs_end_t, dvec, norm_w, b, state_ref, mix_ref, fill):
    xs = xbc[:, :SSD_DIM]
    bm = xbc[:, SSD_DIM:SSD_DIM + SSD_GROUPS * SSD_STATE]
    cm = xbc[:, SSD_DIM + SSD_GROUPS * SSD_STATE:]
    n_sub = TB // SSD_CHUNK
    expand = _head_expand_matrix(SMALL_DT, SSD_HEADS, SSD_HEAD_DIM)
    cs = _columns(cs_t)
    xc = xs * _expand_heads(sp_t, expand)
    xcd = xc * _expand_heads(jnp.exp(cs_end_t - cs_t), expand)
    ecs = _expand_heads(jnp.exp(cs_t), expand)
    col = lax.broadcasted_iota(jnp.int32, (SMALL_ROWS, SMALL_ROWS), 1)
    totals = jnp.zeros((SMALL_ROWS, SMALL_ROWS), F32)
    for c in range(n_sub):
        totals = jnp.where(col == c, jnp.exp(cs_end_t[:, c * SSD_CHUNK:c * SSD_CHUNK + 1]), totals)
    etot = _expand_heads(totals, expand)
    ri, ci = _iota2((SSD_CHUNK, SSD_CHUNK))
    tri = ci <= ri
    gw = SSD_HPG * SSD_HEAD_DIM
    state = [state_ref[b, g] for g in range(SSD_GROUPS)]
    y_rows = []
    for c in range(n_sub):
        rs = slice(c * SSD_CHUNK, (c + 1) * SSD_CHUNK)
        ys = []
        for g in range(SSD_GROUPS):
            bg = bm[rs, g * SSD_STATE:(g + 1) * SSD_STATE]
            cg = cm[rs, g * SSD_STATE:(g + 1) * SSD_STATE]
            cb = _mm_nt(cg, bg)
            y_off = _mm(cg, state[g]) * ecs[rs, g * gw:(g + 1) * gw]
            state[g] = state[g] * etot[c:c + 1, g * gw:(g + 1) * gw] + _mm_tn(bg, xcd[rs, g * gw:(g + 1) * gw])
            yd = []
            for hh in range(SSD_HPG):
                h = g * SSD_HPG + hh
                diff = cs[rs, h:h + 1] - cs_t[h:h + 1, rs]
                lmat = jnp.exp(jnp.where(tri, diff, MASKED))
                yd.append(_mm(cb * lmat, xc[rs, h * SSD_HEAD_DIM:(h + 1) * SSD_HEAD_DIM]))
                fill(FILL_SSD)
            ys.append(jnp.concatenate(yd, axis=1) + y_off)
        y_rows.append(jnp.concatenate(ys, axis=1))
    for g in range(SSD_GROUPS):
        state_ref[b, g] = state[g]
    y = jnp.concatenate(y_rows, axis=0) + xs * dvec
    gated = y * _silu(z)
    rows = slice(b * TB, (b + 1) * TB)
    for g in range(SSD_GROUPS):
        gg = gated[:, g * gw:(g + 1) * gw]
        gg = gg * lax.rsqrt(jnp.mean(gg * gg, axis=-1, keepdims=True) + EPS)
        mix_ref[rows, ATT_DIM + g * gw:ATT_DIM + (g + 1) * gw] = gg * norm_w[:, g * gw:(g + 1) * gw]


def _gdn(per_b, norm_w, state_ref, mix_ref, fill):
    nb = len(per_b)
    ones = _block_ones(GDN_KDIM, GDN_HEAD_K)

    def seg_sum(x2):
        return jnp.dot(_bf(x2), ones, preferred_element_type=F32)

    exp_beta = _head_expand_matrix(SMALL_BETA, GDN_HEADS, GDN_HEAD_K)
    exp_dec = _head_expand_matrix(SMALL_DECAY, GDN_HEADS, GDN_HEAD_K)
    ri, ci = _iota2((GDN_SPAN, GDN_SPAN))
    same = (ri // CHUNK) == (ci // CHUNK)
    tri = jnp.logical_and(same, ci <= ri)
    strict = jnp.logical_and(same, ci < ri)
    heads = range(GDN_HEADS)
    n_chunks = TB // CHUNK
    n_sp = TB // GDN_SPAN
    hs = [slice(h * GDN_HEAD_K, (h + 1) * GDN_HEAD_K) for h in heads]
    rs = [slice(c * CHUNK, (c + 1) * CHUNK) for c in range(n_chunks)]
    spans = [slice(a * GDN_SPAN, (a + 1) * GDN_SPAN) for a in range(n_sp)]
    bh = [(b, h) for b in range(nb) for h in heads]

    p, sol, qk, qeg, kd = {}, {}, {}, [], []
    for b, (gq, _, beta_t, gc_t, g_end_t) in enumerate(per_b):
        q = gq[:, :GDN_KDIM]
        k = gq[:, GDN_KDIM:2 * GDN_KDIM]
        v = gq[:, 2 * GDN_KDIM:]
        qn = q * lax.rsqrt(seg_sum(q * q) + EPS) * (GDN_HEAD_K ** -0.5)
        kn = k * lax.rsqrt(seg_sum(k * k) + EPS)
        beta_b = _expand_heads(beta_t, exp_beta)
        eg_b = _expand_heads(jnp.exp(gc_t), exp_dec)
        kbeta = kn * beta_b
        vbeta = v * beta_b
        kbg = kbeta * eg_b
        qeg.append(qn * eg_b)
        kd.append(kn * _expand_heads(jnp.exp(g_end_t - gc_t), exp_dec))
        gc = _columns(gc_t)
        for a, sp in enumerate(spans):
            for h in heads:
                row = SMALL_DECAY + h
                diff = gc[sp, row:row + 1] - gc_t[row:row + 1, sp]
                decay = jnp.exp(jnp.where(tri, diff, MASKED))
                u = (b, a, h)
                p[u] = jnp.where(strict, _mm_nt(kbeta[sp, hs[h]], kn[sp, hs[h]]) * (-decay), 0.0)
                qk[u] = _mm_nt(qn[sp, hs[h]], kn[sp, hs[h]]) * decay
                sol[u] = jnp.concatenate([vbeta[sp, hs[h]], kbg[sp, hs[h]]], axis=1)
    units = list(p.keys())
    fill(FILL_GDN_PREP)
    steps = CHUNK.bit_length() - 1
    for j in range(steps):
        skip = (1 << j) // BF16_ROWS * BF16_ROWS
        keep = CHUNK - skip

        def live_rows(x):
            if skip == 0:
                return x
            return jnp.concatenate([x[c * CHUNK + skip:(c + 1) * CHUNK] for c in range(GDN_SPAN // CHUNK)], axis=0)

        def all_rows(r):
            if skip == 0:
                return r
            zeros = jnp.zeros((skip, r.shape[1]), r.dtype)
            return jnp.concatenate(
                [piece for c in range(GDN_SPAN // CHUNK) for piece in (zeros, r[c * keep:(c + 1) * keep])], axis=0)

        pb = {u: _bf(p[u]) for u in units}
        sb = {u: _bf(sol[u]) for u in units}
        if j + 1 < steps:
            res = {u: all_rows(jnp.dot(live_rows(pb[u]), jnp.concatenate([pb[u], sb[u]], axis=1),
                                       preferred_element_type=F32)) for u in units}
            p = {u: res[u][:, :GDN_SPAN] for u in units}
            sol = {u: sol[u] + res[u][:, GDN_SPAN:] for u in units}
        else:
            sol = {u: sol[u] + all_rows(jnp.dot(live_rows(pb[u]), sb[u], preferred_element_type=F32))
                   for u in units}
        fill(FILL_GDN_LEVEL)
    qks_u = {u: _mm(qk[u], sol[u]) for u in units}
    solh = {(b, h): jnp.concatenate([sol[(b, a, h)] for a in range(n_sp)], axis=0) for b, h in bh}
    qks = {(b, h): jnp.concatenate([qks_u[(b, a, h)] for a in range(n_sp)], axis=0) for b, h in bh}
    qeff = {(b, h): qeg[b][:, hs[h]] - qks[(b, h)][:, GDN_HEAD_V:] for b, h in bh}
    kuw = [{(b, h): _mm_tn(kd[b][rs[c], hs[h]], solh[(b, h)][rs[c]]) for b, h in bh} for c in range(n_chunks)]
    state = {(b, h): state_ref[b, h] for b, h in bh}
    states = []
    for c in range(n_chunks):
        states.append(state)
        col = (c + 1) * CHUNK - 1
        e_end = [jnp.exp(per_b[b][3][:, col:col + 1]) for b in range(nb)]
        state = {(b, h): state[(b, h)] * e_end[b][SMALL_DECAY + h:SMALL_DECAY + h + 1, :]
                 - _mm(kuw[c][(b, h)][:, GDN_HEAD_V:], state[(b, h)]) + kuw[c][(b, h)][:, :GDN_HEAD_V]
                 for b, h in bh}
        fill(FILL_GDN_CHAIN)
    for b, h in bh:
        state_ref[b, h] = state[(b, h)]
    outs = [{(b, h): _mm(qeff[(b, h)][rs[c]], states[c][(b, h)]) + qks[(b, h)][rs[c], :GDN_HEAD_V]
             for b, h in bh} for c in range(n_chunks)]
    for b in range(nb):
        o = jnp.concatenate([jnp.concatenate([outs[c][(b, h)] for h in heads], axis=1)
                             for c in range(n_chunks)], axis=0)
        ms = seg_sum(o * o) * (1.0 / GDN_HEAD_V)
        mix_ref[b * TB:(b + 1) * TB, ATT_DIM + SSD_DIM:] = o * lax.rsqrt(ms + EPS) * norm_w * _silu(per_b[b][1])


def _mixer_kernel(layer, n_blk,
                  sinks_ref, x_ref, xnext_ref, prew_ref, wina_ref, winb_ref, wsmall_ref, convw_ref, convb_ref,
                  wout_ref, postw_ref, bias_ref, alog_ref, dvec_ref, snormw_ref, gnormw_ref,
                  o_ref,
                  proj_ref, small_ref, pnext_ref, snext_ref, tail_ref, kprev_ref, vprev_ref, sstate_ref, gstate_ref, mix_ref):
    t = pl.program_id(1)
    proj_args = (prew_ref, (wina_ref, winb_ref), wsmall_ref, convw_ref, convb_ref, tail_ref, pnext_ref, snext_ref)

    @pl.when(t == 0)
    def _():
        tail_ref[...] = jnp.zeros_like(tail_ref)
        kprev_ref[...] = jnp.zeros_like(kprev_ref)
        vprev_ref[...] = jnp.zeros_like(vprev_ref)
        sstate_ref[...] = jnp.zeros_like(sstate_ref)
        gstate_ref[...] = jnp.zeros_like(gstate_ref)
        for piece in _proj_pieces(x_ref, *proj_args):
            piece()

    proj_ref[...] = pnext_ref[...]
    small_ref[...] = snext_ref[...]
    pieces = iter(_proj_pieces(xnext_ref, *proj_args))

    def fill(n):
        for _ in range(n):
            piece = next(pieces, None)
            if piece is not None:
                piece()

    def group(b, off, width):
        return proj_ref[b * TB:(b + 1) * TB, off:off + width]

    fill(FILL_TOP)
    row = lax.broadcasted_iota(jnp.int32, (SMALL_ROWS, 1), 0)
    is_decay = jnp.where(row < SMALL_BETA, 1.0, jnp.where(row >= SMALL_DECAY, 1.0, 0.0))
    neg_a = -jnp.exp(alog_ref[...]) * is_decay
    ri, ci = _iota2((TB, TB))
    same = (ri // CHUNK) == (ci // CHUNK)
    same_ssd = (ri // SSD_CHUNK) == (ci // SSD_CHUNK)
    before = jnp.where(ri <= ci, 1.0, 0.0)
    m_ssd = jnp.where(same_ssd, before, 0.0).astype(BF16)
    m_whole_ssd = jnp.where(same_ssd, 1.0, 0.0).astype(BF16)
    m_chunk = jnp.where(same, before, 0.0).astype(BF16)
    m_whole_chunk = jnp.where(same, 1.0, 0.0).astype(BF16)
    gdn_in, ssd_in = [], []
    for b in range(NB):
        small_t = small_ref[:, b * TB:(b + 1) * TB]
        sp_t = _softplus(small_t + bias_ref[...])
        da_t = sp_t * neg_a
        ssd_in.append((sp_t, _cumsum_lanes(da_t, m_ssd), _cumsum_lanes(da_t, m_whole_ssd)))
        gc_t = _cumsum_lanes(da_t, m_chunk)
        g_end_t = _cumsum_lanes(da_t, m_whole_chunk)
        gdn_in.append((group(b, GQKV_OFF, GDN_CONV_DIM), group(b, GZ_OFF, GDN_DIM), _sigmoid(small_t),
                       gc_t, g_end_t))
    _gdn(gdn_in, gnormw_ref[...], gstate_ref, mix_ref, fill)
    for b in range(NB):
        _attention(group(b, Q_OFF, ATT_DIM), group(b, K_OFF, ATT_KV_DIM), group(b, V_OFF, ATT_KV_DIM),
                   b, kprev_ref, vprev_ref, sinks_ref, layer, t, mix_ref, fill)
    for b in range(NB):
        sp_t, cs_t, cs_end_t = ssd_in[b]
        _ssd(group(b, XBC_OFF, SSD_CONV_DIM), group(b, Z_OFF, SSD_DIM), sp_t, cs_t, cs_end_t, dvec_ref[...],
             snormw_ref[...], b, sstate_ref, mix_ref, fill)
    fill(PROJ_COLS)

    mixed = jnp.dot(_bf(mix_ref[...]), wout_ref[...], preferred_element_type=F32)
    y = _rmsnorm(mixed, postw_ref[...])
    for b in range(NB):
        o_ref[b] = x_ref[b] + y[b * TB:(b + 1) * TB]


def _mixer_call(x3d, params, layer):
    bsz, seq, _ = x3d.shape
    n_blk = seq // TB
    grid = (bsz // NB, n_blk)

    def par_spec(arr, **kw):
        shape = arr.shape[1:]
        return pl.BlockSpec((None,) + shape, lambda b, t: (layer,) + (0,) * len(shape), **kw)

    sinks, pre_norm, w_in_a, w_in_b, w_small_t, convw, convb, wout = params[:8]
    rest = params[8:]
    in_specs = ([pl.BlockSpec(memory_space=pltpu.SMEM),
                 pl.BlockSpec((NB, TB, D_MODEL), lambda b, t: (b, t, 0)),
                 pl.BlockSpec((NB, TB, D_MODEL), lambda b, t: (b, jnp.minimum(t + 1, n_blk - 1), 0)),
                 par_spec(pre_norm),
                 par_spec(w_in_a, pipeline_mode=pl.Buffered(1)),
                 par_spec(w_in_b, pipeline_mode=pl.Buffered(1)),
                 par_spec(w_small_t),
                 par_spec(convw),
                 par_spec(convb),
                 par_spec(wout, pipeline_mode=pl.Buffered(1))]
                + [par_spec(p) for p in rest])
    scratch = [
        pltpu.VMEM((NB * TB, PROJ_COLS), F32),
        pltpu.VMEM((SMALL_ROWS, NB * TB), F32),
        pltpu.VMEM((NB * TB, PROJ_COLS), F32),
        pltpu.VMEM((SMALL_ROWS, NB * TB), F32),
        pltpu.VMEM((NB, SUBLANES, CONV_COLS), F32),
        pltpu.VMEM((NB, WINDOW, ATT_KV_DIM), F32),
        pltpu.VMEM((NB, WINDOW, ATT_KV_DIM), F32),
        pltpu.VMEM((NB, SSD_GROUPS, SSD_STATE, SSD_HPG * SSD_HEAD_DIM), F32),
        pltpu.VMEM((NB, GDN_HEADS, GDN_HEAD_K, GDN_HEAD_V), F32),
        pltpu.VMEM((NB * TB, D_MODEL), F32),
    ]
    return pl.pallas_call(
        functools.partial(_mixer_kernel, layer, n_blk),
        grid=grid,
        in_specs=in_specs,
        out_specs=pl.BlockSpec((NB, TB, D_MODEL), lambda b, t: (b, t, 0)),
        out_shape=jax.ShapeDtypeStruct(x3d.shape, F32),
        scratch_shapes=scratch,
        compiler_params=pltpu.CompilerParams(
            dimension_semantics=("arbitrary", "arbitrary"), vmem_limit_bytes=VMEM_LIMIT),
        name="mixer",
    )(sinks, x3d, x3d, pre_norm, w_in_a, w_in_b, w_small_t, convw, convb, wout, *rest)


def _ffn_kernel(x_ref, prew_ref, postw_ref, wg_ref, wu_ref, wd_ref, o_ref):
    x = x_ref[...]
    h = _bf(_rmsnorm(x, prew_ref[...]))
    acc = jnp.zeros((TM_FFN, D_MODEL), F32)
    for c in range(FF // FF_CHUNK):
        c0, c1 = c * FF_CHUNK, (c + 1) * FF_CHUNK
        gate = jnp.dot(h, wg_ref[:, c0:c1], preferred_element_type=F32)
        up = jnp.dot(h, wu_ref[:, c0:c1], preferred_element_type=F32)
        acc = acc + jnp.dot(_bf(_silu(gate) * up), wd_ref[c0:c1, :], preferred_element_type=F32)
    o_ref[...] = x + _rmsnorm(acc, postw_ref[...])


def _ffn_call(x2d, pre_norm, post_norm, wg, wu, wd, layer):
    m = x2d.shape[0]

    def wspec(shape):
        return pl.BlockSpec((None,) + shape, lambda i: (layer, 0, 0), pipeline_mode=pl.Buffered(1))

    return pl.pallas_call(
        _ffn_kernel,
        grid=(m // TM_FFN,),
        in_specs=[
            pl.BlockSpec((TM_FFN, D_MODEL), lambda i: (i, 0)),
            pl.BlockSpec((None, 1, D_MODEL), lambda i: (layer, 0, 0)),
            pl.BlockSpec((None, 1, D_MODEL), lambda i: (layer, 0, 0)),
            wspec((D_MODEL, FF)),
            wspec((D_MODEL, FF)),
            wspec((FF, D_MODEL)),
        ],
        out_specs=pl.BlockSpec((TM_FFN, D_MODEL), lambda i: (i, 0)),
        out_shape=jax.ShapeDtypeStruct(x2d.shape, F32),
        compiler_params=pltpu.CompilerParams(
            dimension_semantics=("arbitrary",), vmem_limit_bytes=VMEM_LIMIT),
        name="ffn",
    )(x2d, pre_norm, post_norm, wg, wu, wd)


def _regroup_w_in(w_in):
    dt0 = W_SPLIT
    g0 = dt0 + SSD_HEADS
    b0 = g0 + GDN_CONV_DIM + GDN_DIM
    small = jnp.concatenate([w_in[:, :, dt0:g0], w_in[:, :, b0:]], axis=-1)
    return (w_in[:, :, :dt0].astype(BF16), w_in[:, :, g0:b0].astype(BF16),
            jnp.swapaxes(small, 1, 2).astype(BF16))


def _pack_small(ssd_vals, gdn_vals):
    depth = ssd_vals.shape[0]
    mid = jnp.zeros((depth, GDN_HEADS), F32)
    return jnp.concatenate([ssd_vals.astype(F32), mid, gdn_vals.astype(F32)], axis=-1)[:, :, None]


def kernel(x, pre_mix_norm, post_mix_norm, pre_ffn_norm, post_ffn_norm, w_in, w_out, attn_sinks, ssd_conv_w, ssd_conv_b, ssd_dt_bias, ssd_A_log, ssd_D, ssd_norm_w, gdn_conv_w, gdn_dt_bias, gdn_A_log, gdn_norm_w, ffn_w_gate, ffn_w_up, ffn_w_down):
    bsz, seq, _ = x.shape
    depth = w_in.shape[0]
    assert seq % TB == 0 and (bsz * seq) % TM_FFN == 0 and bsz % NB == 0

    w_in_a, w_in_b, w_small_t = _regroup_w_in(w_in)
    vec3 = lambda a: a.reshape(depth, 1, a.shape[-1]).astype(F32)
    convw = jnp.concatenate([ssd_conv_w, gdn_conv_w], axis=-1).astype(F32)
    convb = jnp.concatenate([ssd_conv_b.astype(F32), jnp.zeros((depth, GDN_CONV_DIM), F32)], axis=-1)[:, None, :]
    params = (
        attn_sinks.reshape(-1).astype(F32),
        vec3(pre_mix_norm),
        w_in_a,
        w_in_b,
        w_small_t,
        convw,
        convb,
        w_out.astype(BF16),
        vec3(post_mix_norm),
        _pack_small(ssd_dt_bias, gdn_dt_bias),
        _pack_small(ssd_A_log, gdn_A_log),
        vec3(jnp.repeat(ssd_D, SSD_HEAD_DIM, axis=-1)),
        vec3(ssd_norm_w),
        vec3(jnp.tile(gdn_norm_w, (1, GDN_HEADS))),
    )
    pre_ffn = vec3(pre_ffn_norm)
    post_ffn = vec3(post_ffn_norm)
    wg = ffn_w_gate.astype(BF16)
    wu = ffn_w_up.astype(BF16)
    wd = ffn_w_down.astype(BF16)

    for layer in range(depth):
        x = _mixer_call(x, params, layer)
        x = _ffn_call(x.reshape(bsz * seq, D_MODEL), pre_ffn, post_ffn, wg, wu, wd, layer).reshape(bsz, seq, D_MODEL)
    return x
```

```python
import functools

import jax
import jax.numpy as jnp
from jax import lax
from jax.experimental import pallas as pl
from jax.experimental.pallas import tpu as pltpu

F32 = jnp.float32
BF16 = jnp.bfloat16

D_MODEL = 1024
CHUNK = 64
EPS = 1e-6
CONV_K = 4
MASKED = -1e30

SSD_HEADS = 8
SSD_HEAD_DIM = 64
SSD_DIM = SSD_HEADS * SSD_HEAD_DIM
SSD_GROUPS = 2
SSD_STATE = 128
SSD_HPG = SSD_HEADS // SSD_GROUPS
SSD_CONV_DIM = SSD_DIM + 2 * SSD_GROUPS * SSD_STATE

ATT_HEADS = 4
ATT_KV_HEADS = 2
ATT_HEAD_DIM = 64
ATT_DIM = ATT_HEADS * ATT_HEAD_DIM
ATT_KV_DIM = ATT_KV_HEADS * ATT_HEAD_DIM
WINDOW = 128
WIN_CHUNKS = WINDOW // CHUNK

GDN_HEADS = 4
GDN_HEAD_K = 64
GDN_HEAD_V = 64
GDN_KDIM = GDN_HEADS * GDN_HEAD_K
GDN_DIM = GDN_HEADS * GDN_HEAD_V
GDN_CONV_DIM = 2 * GDN_KDIM + GDN_DIM

FF = 2816
LANES = 128
SUBLANES = 8
BF16_ROWS = 16

PROJ_WIDTHS = (ATT_DIM, ATT_KV_DIM, ATT_KV_DIM, SSD_DIM, SSD_CONV_DIM, GDN_CONV_DIM, GDN_DIM)
PROJ_OFFS = tuple(sum(PROJ_WIDTHS[:i]) for i in range(len(PROJ_WIDTHS)))
PROJ_COLS = sum(PROJ_WIDTHS)
Q_OFF, K_OFF, V_OFF, Z_OFF, XBC_OFF, GQKV_OFF, GZ_OFF = PROJ_OFFS
CONV_LO = XBC_OFF
CONV_HI = GZ_OFF
CONV_COLS = CONV_HI - CONV_LO
W_SPLIT = GQKV_OFF
SMALL_ROWS = SSD_HEADS + 2 * GDN_HEADS
SMALL_DT = 0
SMALL_BETA = SSD_HEADS
SMALL_DECAY = SSD_HEADS + GDN_HEADS

TM_FFN = 512
FF_CHUNK = 256
TB = 256
NB = 2
PROJ_CHUNK = 256
ATT_QB = 128
GDN_SPAN = 128
SSD_CHUNK = 128
FILL_GDN_PREP = 20
FILL_GDN_LEVEL = 5
V7X_VMEM_BYTES = 64 * 1024 * 1024
VMEM_LIMIT = V7X_VMEM_BYTES * 7 // 8


def _bf(x):
    return x.astype(BF16)


def _mm(a, b):
    return jnp.dot(_bf(a), _bf(b), preferred_element_type=F32)


def _mm_nt(a, b):
    return lax.dot_general(_bf(a), _bf(b), (((1,), (1,)), ((), ())), preferred_element_type=F32)


def _mm_tn(a, b):
    return lax.dot_general(_bf(a), _bf(b), (((0,), (0,)), ((), ())), preferred_element_type=F32)


def _split(x, terms):
    out = []
    for _ in range(terms - 1):
        t = x.astype(BF16)
        out.append(t)
        x = x - t.astype(F32)
    out.append(x.astype(BF16))
    return out


def _cumsum_lanes(x_t, m_bf16):
    return sum(jnp.dot(t, m_bf16, preferred_element_type=F32) for t in _split(x_t, 3))


def _expand_heads(x_t, r_bf16):
    terms = jnp.concatenate(_split(x_t, 2), axis=0)
    return lax.dot_general(terms, jnp.concatenate([r_bf16, r_bf16], axis=0), (((0,), (0,)), ((), ())),
                           preferred_element_type=F32)


def _columns(x_t):
    pad = jnp.zeros((LANES - x_t.shape[0], x_t.shape[1]), F32)
    return jnp.concatenate([x_t, pad], axis=0).T


def _sigmoid(x):
    return 0.5 + 0.5 * jnp.tanh(0.5 * x)


def _silu(x):
    hx = 0.5 * x
    return hx + hx * jnp.tanh(hx)


def _softplus(x):
    return jnp.maximum(x, 0.0) + jnp.log1p(jnp.exp(-jnp.abs(x)))


def _rmsnorm(x, w):
    return x * lax.rsqrt(jnp.mean(x * x, axis=-1, keepdims=True) + EPS) * w


def _iota2(shape):
    return (lax.broadcasted_iota(jnp.int32, shape, 0), lax.broadcasted_iota(jnp.int32, shape, 1))


def _head_expand_matrix(src_off, n_heads, width):
    ri, ci = _iota2((SMALL_ROWS, n_heads * width))
    return jnp.where(ri - src_off == ci // width, 1.0, 0.0).astype(BF16)


def _block_ones(n, seg):
    ri, ci = _iota2((n, n))
    return jnp.where((ri // seg) == (ci // seg), 1.0, 0.0).astype(BF16)


def _shift_rows(x, tail, j):
    row = lax.broadcasted_iota(jnp.int32, (SUBLANES, x.shape[1]), 0)
    xs = pltpu.roll(x, j, axis=0)
    first = jnp.where(row < j, pltpu.roll(tail, j, axis=0), xs[0:SUBLANES])
    return jnp.concatenate([first, xs[SUBLANES:]], axis=0)


def _causal_conv(x, tail, w):
    assert CONV_K == 4
    x1 = _shift_rows(x, tail, 1)
    u = x * w[3:4, :] + x1 * w[2:3, :]
    v = x * w[1:2, :] + x1 * w[0:1, :]
    v_tail = tail * w[1:2, :] + pltpu.roll(tail, 1, axis=0) * w[0:1, :]
    return u + _shift_rows(v, v_tail, 2)


def _proj_pieces(x_ref, nw_ref, w_refs, wsmall_ref, convw_ref, convb_ref, tail_ref, proj_ref, small_ref):
    cell = {}

    def norm():
        x = x_ref[...].reshape(NB * TB, D_MODEL)
        cell["h"] = _bf(_rmsnorm(x, nw_ref[...]))

    def dot(key):
        c0, b = key

        def run():
            w_ref, w0 = (w_refs[0], c0) if c0 < W_SPLIT else (w_refs[1], c0 - W_SPLIT)
            cell[key] = jnp.dot(cell["h"][b * TB:(b + 1) * TB], w_ref[:, w0:w0 + PROJ_CHUNK],
                                preferred_element_type=F32)
        return run

    def post(key):
        c0, b = key

        def run():
            y = cell.pop(key)
            if CONV_LO <= c0 < CONV_HI:
                cols = slice(c0 - CONV_LO, c0 - CONV_LO + PROJ_CHUNK)
                conv = _causal_conv(y, tail_ref[b, :, cols], convw_ref[:, cols])
                tail_ref[b, :, cols] = y[TB - SUBLANES:TB]
                y = _silu(conv + convb_ref[:, cols])
            proj_ref[b * TB:(b + 1) * TB, c0:c0 + PROJ_CHUNK] = y
        return run

    def small():
        small_ref[...] = lax.dot_general(wsmall_ref[...], cell["h"], (((1,), (1,)), ((), ())),
                                          preferred_element_type=F32)

    keys = [(c0, b) for c0 in range(0, PROJ_COLS, PROJ_CHUNK) for b in range(NB)]
    seq = [norm, dot(keys[0])]
    for prev, key in zip(keys[:-1], keys[1:]):
        seq += [dot(key), post(prev)]
    return seq + [post(keys[-1]), small]


def _attention(q, k, v, b, kprev_ref, vprev_ref, sinks_ref, layer, t, mix_ref):
    nk = ATT_QB + WINDOW
    kext = jnp.concatenate([kprev_ref[b], k], axis=0)
    vext = jnp.concatenate([vprev_ref[b], v], axis=0)
    kprev_ref[b] = k[TB - WINDOW:TB]
    vprev_ref[b] = v[TB - WINDOW:TB]
    qi, kj = _iota2((ATT_QB, nk))
    dist = jnp.abs(qi + WINDOW - kj).astype(F32)
    qc = qi // CHUNK
    kc = kj // CHUNK - WIN_CHUNKS
    in_band = jnp.where(kc <= qc, jnp.where(kc >= qc - WIN_CHUNKS, 1, 0), 0)
    grp = ATT_HEADS // ATT_KV_HEADS
    for a in range(TB // ATT_QB):
        first_chunk = t * (TB // CHUNK) + a * (ATT_QB // CHUNK)
        valid = (in_band * jnp.where(kc + first_chunk >= 0, 1, 0)) > 0
        rows = slice(a * ATT_QB, (a + 1) * ATT_QB)
        out_rows = slice(b * TB + a * ATT_QB, b * TB + (a + 1) * ATT_QB)
        krows = slice(a * ATT_QB, a * ATT_QB + nk)
        for kh in range(ATT_KV_HEADS):
            kk = kext[krows, kh * ATT_HEAD_DIM:(kh + 1) * ATT_HEAD_DIM]
            vv = vext[krows, kh * ATT_HEAD_DIM:(kh + 1) * ATT_HEAD_DIM]
            for gi in range(grp):
                h = kh * grp + gi
                slope = 2.0 ** (-8.0 * (h + 1) / ATT_HEADS)
                sink = sinks_ref[layer * ATT_HEADS + h]
                qh = q[rows, h * ATT_HEAD_DIM:(h + 1) * ATT_HEAD_DIM]
                s = _mm_nt(qh, kk) * (ATT_HEAD_DIM ** -0.5) - slope * dist
                s = jnp.where(valid, s, MASKED)
                m = jnp.maximum(jnp.max(s, axis=-1, keepdims=True), sink)
                p = jnp.exp(s - m)
                den = jnp.sum(p, axis=-1, keepdims=True) + jnp.exp(sink - m)
                mix_ref[out_rows, h * ATT_HEAD_DIM:(h + 1) * ATT_HEAD_DIM] = _mm(p, vv) / den


def _ssd(xbc, z, sp_t, cs_t, cs_end_t, dvec, norm_w, b, state_ref, mix_ref):
    xs = xbc[:, :SSD_DIM]
    bm = xbc[:, SSD_DIM:SSD_DIM + SSD_GROUPS * SSD_STATE]
    cm = xbc[:, SSD_DIM + SSD_GROUPS * SSD_STATE:]
    n_sub = TB // SSD_CHUNK
    expand = _head_expand_matrix(SMALL_DT, SSD_HEADS, SSD_HEAD_DIM)
    cs = _columns(cs_t)
    xc = xs * _expand_heads(sp_t, expand)
    xcd = xc * _expand_heads(jnp.exp(cs_end_t - cs_t), expand)
    ecs = _expand_heads(jnp.exp(cs_t), expand)
    col = lax.broadcasted_iota(jnp.int32, (SMALL_ROWS, SMALL_ROWS), 1)
    totals = jnp.zeros((SMALL_ROWS, SMALL_ROWS), F32)
    for c in range(n_sub):
        totals = jnp.where(col == c, jnp.exp(cs_end_t[:, c * SSD_CHUNK:c * SSD_CHUNK + 1]), totals)
    etot = _expand_heads(totals, expand)
    ri, ci = _iota2((SSD_CHUNK, SSD_CHUNK))
    tri = ci <= ri
    gw = SSD_HPG * SSD_HEAD_DIM
    state = [state_ref[b, g] for g in range(SSD_GROUPS)]
    y_rows = []
    for c in range(n_sub):
        rs = slice(c * SSD_CHUNK, (c + 1) * SSD_CHUNK)
        ys = []
        for g in range(SSD_GROUPS):
            bg = bm[rs, g * SSD_STATE:(g + 1) * SSD_STATE]
            cg = cm[rs, g * SSD_STATE:(g + 1) * SSD_STATE]
            cb = _mm_nt(cg, bg)
            y_off = _mm(cg, state[g]) * ecs[rs, g * gw:(g + 1) * gw]
            state[g] = state[g] * etot[c:c + 1, g * gw:(g + 1) * gw] + _mm_tn(bg, xcd[rs, g * gw:(g + 1) * gw])
            yd = []
            for hh in range(SSD_HPG):
                h = g * SSD_HPG + hh
                diff = cs[rs, h:h + 1] - cs_t[h:h + 1, rs]
                lmat = jnp.exp(jnp.where(tri, diff, MASKED))
                yd.append(_mm(cb * lmat, xc[rs, h * SSD_HEAD_DIM:(h + 1) * SSD_HEAD_DIM]))
            ys.append(jnp.concatenate(yd, axis=1) + y_off)
        y_rows.append(jnp.concatenate(ys, axis=1))
    for g in range(SSD_GROUPS):
        state_ref[b, g] = state[g]
    y = jnp.concatenate(y_rows, axis=0) + xs * dvec
    gated = y * _silu(z)
    rows = slice(b * TB, (b + 1) * TB)
    for g in range(SSD_GROUPS):
        gg = gated[:, g * gw:(g + 1) * gw]
        gg = gg * lax.rsqrt(jnp.mean(gg * gg, axis=-1, keepdims=True) + EPS)
        mix_ref[rows, ATT_DIM + g * gw:ATT_DIM + (g + 1) * gw] = gg * norm_w[:, g * gw:(g + 1) * gw]


def _gdn(per_b, norm_w, state_ref, mix_ref, fill):
    nb = len(per_b)
    ones = _block_ones(GDN_KDIM, GDN_HEAD_K)

    def seg_sum(x2):
        return jnp.dot(_bf(x2), ones, preferred_element_type=F32)

    exp_beta = _head_expand_matrix(SMALL_BETA, GDN_HEADS, GDN_HEAD_K)
    exp_dec = _head_expand_matrix(SMALL_DECAY, GDN_HEADS, GDN_HEAD_K)
    ri, ci = _iota2((GDN_SPAN, GDN_SPAN))
    same = (ri // CHUNK) == (ci // CHUNK)
    tri = jnp.logical_and(same, ci <= ri)
    strict = jnp.logical_and(same, ci < ri)
    heads = range(GDN_HEADS)
    n_chunks = TB // CHUNK
    n_sp = TB // GDN_SPAN
    hs = [slice(h * GDN_HEAD_K, (h + 1) * GDN_HEAD_K) for h in heads]
    rs = [slice(c * CHUNK, (c + 1) * CHUNK) for c in range(n_chunks)]
    spans = [slice(a * GDN_SPAN, (a + 1) * GDN_SPAN) for a in range(n_sp)]
    bh = [(b, h) for b in range(nb) for h in heads]

    p, sol, qk, qeg, kd = {}, {}, {}, [], []
    for b, (gq, _, beta_t, gc_t, g_end_t) in enumerate(per_b):
        q = gq[:, :GDN_KDIM]
        k = gq[:, GDN_KDIM:2 * GDN_KDIM]
        v = gq[:, 2 * GDN_KDIM:]
        qn = q * lax.rsqrt(seg_sum(q * q) + EPS) * (GDN_HEAD_K ** -0.5)
        kn = k * lax.rsqrt(seg_sum(k * k) + EPS)
        beta_b = _expand_heads(beta_t, exp_beta)
        eg_b = _expand_heads(jnp.exp(gc_t), exp_dec)
        kbeta = kn * beta_b
        vbeta = v * beta_b
        kbg = kbeta * eg_b
        qeg.append(qn * eg_b)
        kd.append(kn * _expand_heads(jnp.exp(g_end_t - gc_t), exp_dec))
        gc = _columns(gc_t)
        for a, sp in enumerate(spans):
            for h in heads:
                row = SMALL_DECAY + h
                diff = gc[sp, row:row + 1] - gc_t[row:row + 1, sp]
                decay = jnp.exp(jnp.where(tri, diff, MASKED))
                u = (b, a, h)
                p[u] = jnp.where(strict, _mm_nt(kbeta[sp, hs[h]], kn[sp, hs[h]]) * (-decay), 0.0)
                qk[u] = _mm_nt(qn[sp, hs[h]], kn[sp, hs[h]]) * decay
                sol[u] = jnp.concatenate([vbeta[sp, hs[h]], kbg[sp, hs[h]]], axis=1)
    units = list(p.keys())
    fill(FILL_GDN_PREP)
    steps = CHUNK.bit_length() - 1
    for j in range(steps):
        skip = (1 << j) // BF16_ROWS * BF16_ROWS
        keep = CHUNK - skip

        def live_rows(x):
            if skip == 0:
                return x
            return jnp.concatenate([x[c * CHUNK + skip:(c + 1) * CHUNK] for c in range(GDN_SPAN // CHUNK)], axis=0)

        def all_rows(r):
            if skip == 0:
                return r
            zeros = jnp.zeros((skip, r.shape[1]), r.dtype)
            return jnp.concatenate(
                [piece for c in range(GDN_SPAN // CHUNK) for piece in (zeros, r[c * keep:(c + 1) * keep])], axis=0)

        pb = {u: _bf(p[u]) for u in units}
        sb = {u: _bf(sol[u]) for u in units}
        if j + 1 < steps:
            res = {u: all_rows(jnp.dot(live_rows(pb[u]), jnp.concatenate([pb[u], sb[u]], axis=1),
                                       preferred_element_type=F32)) for u in units}
            p = {u: res[u][:, :GDN_SPAN] for u in units}
            sol = {u: sol[u] + res[u][:, GDN_SPAN:] for u in units}
        else:
            sol = {u: sol[u] + all_rows(jnp.dot(live_rows(pb[u]), sb[u], preferred_element_type=F32))
                   for u in units}
        fill(FILL_GDN_LEVEL)
    qks_u = {u: _mm(qk[u], sol[u]) for u in units}
    solh = {(b, h): jnp.concatenate([sol[(b, a, h)] for a in range(n_sp)], axis=0) for b, h in bh}
    qks = {(b, h): jnp.concatenate([qks_u[(b, a, h)] for a in range(n_sp)], axis=0) for b, h in bh}
    qeff = {(b, h): qeg[b][:, hs[h]] - qks[(b, h)][:, GDN_HEAD_V:] for b, h in bh}
    kuw = [{(b, h): _mm_tn(kd[b][rs[c], hs[h]], solh[(b, h)][rs[c]]) for b, h in bh} for c in range(n_chunks)]
    state = {(b, h): state_ref[b, h] for b, h in bh}
    states = []
    for c in range(n_chunks):
        states.append(state)
        col = (c + 1) * CHUNK - 1
        e_end = [jnp.exp(per_b[b][3][:, col:col + 1]) for b in range(nb)]
        state = {(b, h): state[(b, h)] * e_end[b][SMALL_DECAY + h:SMALL_DECAY + h + 1, :]
                 - _mm(kuw[c][(b, h)][:, GDN_HEAD_V:], state[(b, h)]) + kuw[c][(b, h)][:, :GDN_HEAD_V]
                 for b, h in bh}
    for b, h in bh:
        state_ref[b, h] = state[(b, h)]
    outs = [{(b, h): _mm(qeff[(b, h)][rs[c]], states[c][(b, h)]) + qks[(b, h)][rs[c], :GDN_HEAD_V]
             for b, h in bh} for c in range(n_chunks)]
    for b in range(nb):
        o = jnp.concatenate([jnp.concatenate([outs[c][(b, h)] for h in heads], axis=1)
                             for c in range(n_chunks)], axis=0)
        ms = seg_sum(o * o) * (1.0 / GDN_HEAD_V)
        mix_ref[b * TB:(b + 1) * TB, ATT_DIM + SSD_DIM:] = o * lax.rsqrt(ms + EPS) * norm_w * _silu(per_b[b][1])


def _mixer_kernel(layer,
                  sinks_ref, x_ref, xnext_ref, prew_ref, wina_ref, winb_ref, wsmall_ref, convw_ref, convb_ref,
                  wout_ref, postw_ref, bias_ref, alog_ref, dvec_ref, snormw_ref, gnormw_ref,
                  o_ref,
                  proj_ref, small_ref, pnext_ref, snext_ref, tail_ref, kprev_ref, vprev_ref, sstate_ref, gstate_ref, mix_ref):
    t = pl.program_id(1)
    proj_args = (prew_ref, (wina_ref, winb_ref), wsmall_ref, convw_ref, convb_ref, tail_ref, pnext_ref, snext_ref)

    @pl.when(t == 0)
    def _():
        tail_ref[...] = jnp.zeros_like(tail_ref)
        kprev_ref[...] = jnp.zeros_like(kprev_ref)
        vprev_ref[...] = jnp.zeros_like(vprev_ref)
        sstate_ref[...] = jnp.zeros_like(sstate_ref)
        gstate_ref[...] = jnp.zeros_like(gstate_ref)
        for piece in _proj_pieces(x_ref, *proj_args):
            piece()

    proj_ref[...] = pnext_ref[...]
    small_ref[...] = snext_ref[...]
    pieces = iter(_proj_pieces(xnext_ref, *proj_args))

    def fill(n):
        for _ in range(n):
            piece = next(pieces, None)
            if piece is not None:
                piece()

    def group(b, off, width):
        return proj_ref[b * TB:(b + 1) * TB, off:off + width]

    row = lax.broadcasted_iota(jnp.int32, (SMALL_ROWS, 1), 0)
    is_decay = jnp.where(row < SMALL_BETA, 1.0, jnp.where(row >= SMALL_DECAY, 1.0, 0.0))
    neg_a = -jnp.exp(alog_ref[...]) * is_decay
    ri, ci = _iota2((TB, TB))
    same = (ri // CHUNK) == (ci // CHUNK)
    same_ssd = (ri // SSD_CHUNK) == (ci // SSD_CHUNK)
    before = jnp.where(ri <= ci, 1.0, 0.0)
    m_ssd = jnp.where(same_ssd, before, 0.0).astype(BF16)
    m_whole_ssd = jnp.where(same_ssd, 1.0, 0.0).astype(BF16)
    m_chunk = jnp.where(same, before, 0.0).astype(BF16)
    m_whole_chunk = jnp.where(same, 1.0, 0.0).astype(BF16)
    gdn_in, ssd_in = [], []
    for b in range(NB):
        small_t = small_ref[:, b * TB:(b + 1) * TB]
        sp_t = _softplus(small_t + bias_ref[...])
        da_t = sp_t * neg_a
        ssd_in.append((sp_t, _cumsum_lanes(da_t, m_ssd), _cumsum_lanes(da_t, m_whole_ssd)))
        gc_t = _cumsum_lanes(da_t, m_chunk)
        g_end_t = _cumsum_lanes(da_t, m_whole_chunk)
        gdn_in.append((group(b, GQKV_OFF, GDN_CONV_DIM), group(b, GZ_OFF, GDN_DIM), _sigmoid(small_t),
                       gc_t, g_end_t))
    _gdn(gdn_in, gnormw_ref[...], gstate_ref, mix_ref, fill)
    for b in range(NB):
        _attention(group(b, Q_OFF, ATT_DIM), group(b, K_OFF, ATT_KV_DIM), group(b, V_OFF, ATT_KV_DIM),
                   b, kprev_ref, vprev_ref, sinks_ref, layer, t, mix_ref)
    for b in range(NB):
        sp_t, cs_t, cs_end_t = ssd_in[b]
        _ssd(group(b, XBC_OFF, SSD_CONV_DIM), group(b, Z_OFF, SSD_DIM), sp_t, cs_t, cs_end_t, dvec_ref[...],
             snormw_ref[...], b, sstate_ref, mix_ref)
    fill(PROJ_COLS)

    mixed = jnp.dot(_bf(mix_ref[...]), wout_ref[...], preferred_element_type=F32)
    y = _rmsnorm(mixed, postw_ref[...])
    for b in range(NB):
        o_ref[b] = x_ref[b] + y[b * TB:(b + 1) * TB]


def _mixer_call(x3d, params, layer):
    bsz, seq, _ = x3d.shape
    n_blk = seq // TB
    grid = (bsz // NB, n_blk)

    def par_spec(arr, **kw):
        shape = arr.shape[1:]
        return pl.BlockSpec((None,) + shape, lambda b, t: (layer,) + (0,) * len(shape), **kw)

    sinks, pre_norm, w_in_a, w_in_b, w_small_t, convw, convb, wout = params[:8]
    rest = params[8:]
    in_specs = ([pl.BlockSpec(memory_space=pltpu.SMEM),
                 pl.BlockSpec((NB, TB, D_MODEL), lambda b, t: (b, t, 0)),
                 pl.BlockSpec((NB, TB, D_MODEL), lambda b, t: (b, jnp.minimum(t + 1, n_blk - 1), 0)),
                 par_spec(pre_norm),
                 par_spec(w_in_a, pipeline_mode=pl.Buffered(1)),
                 par_spec(w_in_b, pipeline_mode=pl.Buffered(1)),
                 par_spec(w_small_t),
                 par_spec(convw),
                 par_spec(convb),
                 par_spec(wout, pipeline_mode=pl.Buffered(1))]
                + [par_spec(p) for p in rest])
    scratch = [
        pltpu.VMEM((NB * TB, PROJ_COLS), F32),
        pltpu.VMEM((SMALL_ROWS, NB * TB), F32),
        pltpu.VMEM((NB * TB, PROJ_COLS), F32),
        pltpu.VMEM((SMALL_ROWS, NB * TB), F32),
        pltpu.VMEM((NB, SUBLANES, CONV_COLS), F32),
        pltpu.VMEM((NB, WINDOW, ATT_KV_DIM), F32),
        pltpu.VMEM((NB, WINDOW, ATT_KV_DIM), F32),
        pltpu.VMEM((NB, SSD_GROUPS, SSD_STATE, SSD_HPG * SSD_HEAD_DIM), F32),
        pltpu.VMEM((NB, GDN_HEADS, GDN_HEAD_K, GDN_HEAD_V), F32),
        pltpu.VMEM((NB * TB, D_MODEL), F32),
    ]
    return pl.pallas_call(
        functools.partial(_mixer_kernel, layer),
        grid=grid,
        in_specs=in_specs,
        out_specs=pl.BlockSpec((NB, TB, D_MODEL), lambda b, t: (b, t, 0)),
        out_shape=jax.ShapeDtypeStruct(x3d.shape, F32),
        scratch_shapes=scratch,
        compiler_params=pltpu.CompilerParams(
            dimension_semantics=("arbitrary", "arbitrary"), vmem_limit_bytes=VMEM_LIMIT),
        name="mixer",
    )(sinks, x3d, x3d, pre_norm, w_in_a, w_in_b, w_small_t, convw, convb, wout, *rest)


def _ffn_kernel(x_ref, prew_ref, postw_ref, wg_ref, wu_ref, wd_ref, o_ref):
    x = x_ref[...]
    h = _bf(_rmsnorm(x, prew_ref[...]))
    acc = jnp.zeros((TM_FFN, D_MODEL), F32)
    for c in range(FF // FF_CHUNK):
        c0, c1 = c * FF_CHUNK, (c + 1) * FF_CHUNK
        gate = jnp.dot(h, wg_ref[:, c0:c1], preferred_element_type=F32)
        up = jnp.dot(h, wu_ref[:, c0:c1], preferred_element_type=F32)
        acc = acc + jnp.dot(_bf(_silu(gate) * up), wd_ref[c0:c1, :], preferred_element_type=F32)
    o_ref[...] = x + _rmsnorm(acc, postw_ref[...])


def _ffn_call(x2d, pre_norm, post_norm, wg, wu, wd, layer):
    m = x2d.shape[0]

    def wspec(shape):
        return pl.BlockSpec((None,) + shape, lambda i: (layer, 0, 0), pipeline_mode=pl.Buffered(1))

    return pl.pallas_call(
        _ffn_kernel,
        grid=(m // TM_FFN,),
        in_specs=[
            pl.BlockSpec((TM_FFN, D_MODEL), lambda i: (i, 0)),
            pl.BlockSpec((None, 1, D_MODEL), lambda i: (layer, 0, 0)),
            pl.BlockSpec((None, 1, D_MODEL), lambda i: (layer, 0, 0)),
            wspec((D_MODEL, FF)),
            wspec((D_MODEL, FF)),
            wspec((FF, D_MODEL)),
        ],
        out_specs=pl.BlockSpec((TM_FFN, D_MODEL), lambda i: (i, 0)),
        out_shape=jax.ShapeDtypeStruct(x2d.shape, F32),
        compiler_params=pltpu.CompilerParams(
            dimension_semantics=("arbitrary",), vmem_limit_bytes=VMEM_LIMIT),
        name="ffn",
    )(x2d, pre_norm, post_norm, wg, wu, wd)


def _regroup_w_in(w_in):
    dt0 = W_SPLIT
    g0 = dt0 + SSD_HEADS
    b0 = g0 + GDN_CONV_DIM + GDN_DIM
    small = jnp.concatenate([w_in[:, :, dt0:g0], w_in[:, :, b0:]], axis=-1)
    return (w_in[:, :, :dt0].astype(BF16), w_in[:, :, g0:b0].astype(BF16),
            jnp.swapaxes(small, 1, 2).astype(BF16))


def _pack_small(ssd_vals, gdn_vals):
    depth = ssd_vals.shape[0]
    mid = jnp.zeros((depth, GDN_HEADS), F32)
    return jnp.concatenate([ssd_vals.astype(F32), mid, gdn_vals.astype(F32)], axis=-1)[:, :, None]


def kernel(x, pre_mix_norm, post_mix_norm, pre_ffn_norm, post_ffn_norm, w_in, w_out, attn_sinks, ssd_conv_w, ssd_conv_b, ssd_dt_bias, ssd_A_log, ssd_D, ssd_norm_w, gdn_conv_w, gdn_dt_bias, gdn_A_log, gdn_norm_w, ffn_w_gate, ffn_w_up, ffn_w_down):
    bsz, seq, _ = x.shape
    depth = w_in.shape[0]
    assert seq % TB == 0 and (bsz * seq) % TM_FFN == 0 and bsz % NB == 0

    w_in_a, w_in_b, w_small_t = _regroup_w_in(w_in)
    vec3 = lambda a: a.reshape(depth, 1, a.shape[-1]).astype(F32)
    convw = jnp.concatenate([ssd_conv_w, gdn_conv_w], axis=-1).astype(F32)
    convb = jnp.concatenate([ssd_conv_b.astype(F32), jnp.zeros((depth, GDN_CONV_DIM), F32)], axis=-1)[:, None, :]
    params = (
        attn_sinks.reshape(-1).astype(F32),
        vec3(pre_mix_norm),
        w_in_a,
        w_in_b,
        w_small_t,
        convw,
        convb,
        w_out.astype(BF16),
        vec3(post_mix_norm),
        _pack_small(ssd_dt_bias, gdn_dt_bias),
        _pack_small(ssd_A_log, gdn_A_log),
        vec3(jnp.repeat(ssd_D, SSD_HEAD_DIM, axis=-1)),
        vec3(ssd_norm_w),
        vec3(jnp.tile(gdn_norm_w, (1, GDN_HEADS))),
    )
    pre_ffn = vec3(pre_ffn_norm)
    post_ffn = vec3(post_ffn_norm)
    wg = ffn_w_gate.astype(BF16)
    wu = ffn_w_up.astype(BF16)
    wd = ffn_w_down.astype(BF16)

    for layer in range(depth):
        x = _mixer_call(x, params, layer)
        x = _ffn_call(x.reshape(bsz * seq, D_MODEL), pre_ffn, post_ffn, wg, wu, wd, layer).reshape(bsz, seq, D_MODEL)
    return x
```

```python
import functools

import jax
import jax.numpy as jnp
from jax import lax
from jax.experimental import pallas as pl
from jax.experimental.pallas import tpu as pltpu

F32 = jnp.float32
BF16 = jnp.bfloat16

D_MODEL = 1024
CHUNK = 64
EPS = 1e-6
CONV_K = 4
MASKED = -1e30

SSD_HEADS = 8
SSD_HEAD_DIM = 64
SSD_DIM = SSD_HEADS * SSD_HEAD_DIM
SSD_GROUPS = 2
SSD_STATE = 128
SSD_HPG = SSD_HEADS // SSD_GROUPS
SSD_CONV_DIM = SSD_DIM + 2 * SSD_GROUPS * SSD_STATE

ATT_HEADS = 4
ATT_KV_HEADS = 2
ATT_HEAD_DIM = 64
ATT_DIM = ATT_HEADS * ATT_HEAD_DIM
ATT_KV_DIM = ATT_KV_HEADS * ATT_HEAD_DIM
WINDOW = 128
WIN_CHUNKS = WINDOW // CHUNK

GDN_HEADS = 4
GDN_HEAD_K = 64
GDN_HEAD_V = 64
GDN_KDIM = GDN_HEADS * GDN_HEAD_K
GDN_DIM = GDN_HEADS * GDN_HEAD_V
GDN_CONV_DIM = 2 * GDN_KDIM + GDN_DIM

FF = 2816
LANES = 128
SUBLANES = 8
BF16_ROWS = 16

PROJ_WIDTHS = (ATT_DIM, ATT_KV_DIM, ATT_KV_DIM, SSD_DIM, SSD_CONV_DIM, GDN_CONV_DIM, GDN_DIM)
PROJ_OFFS = tuple(sum(PROJ_WIDTHS[:i]) for i in range(len(PROJ_WIDTHS)))
PROJ_COLS = sum(PROJ_WIDTHS)
Q_OFF, K_OFF, V_OFF, Z_OFF, XBC_OFF, GQKV_OFF, GZ_OFF = PROJ_OFFS
CONV_LO = XBC_OFF
CONV_HI = GZ_OFF
CONV_COLS = CONV_HI - CONV_LO
W_SPLIT = GQKV_OFF
GDN_Q_OFF = GQKV_OFF
GDN_K_OFF = GQKV_OFF + GDN_KDIM
SMALL_ROWS = SSD_HEADS + 2 * GDN_HEADS
SMALL_DT = 0
SMALL_BETA = SSD_HEADS
SMALL_DECAY = SSD_HEADS + GDN_HEADS

TM_FFN = 512
FF_CHUNK = 256
TB = 256
NB = 2
PROJ_CHUNK = 256
ATT_QB = 128
GDN_SPAN = 128
SSD_CHUNK = 128
FILL_GDN_PREP = 20
FILL_GDN_LEVEL = 5
V7X_VMEM_BYTES = 64 * 1024 * 1024
VMEM_LIMIT = V7X_VMEM_BYTES * 7 // 8


def _bf(x):
    return x.astype(BF16)


def _mm(a, b):
    return jnp.dot(_bf(a), _bf(b), preferred_element_type=F32)


def _mm_nt(a, b):
    return lax.dot_general(_bf(a), _bf(b), (((1,), (1,)), ((), ())), preferred_element_type=F32)


def _mm_tn(a, b):
    return lax.dot_general(_bf(a), _bf(b), (((0,), (0,)), ((), ())), preferred_element_type=F32)


def _split(x, terms):
    out = []
    for _ in range(terms - 1):
        t = x.astype(BF16)
        out.append(t)
        x = x - t.astype(F32)
    out.append(x.astype(BF16))
    return out


def _cumsum_lanes(x_t, m_bf16):
    return sum(jnp.dot(t, m_bf16, preferred_element_type=F32) for t in _split(x_t, 3))


def _expand_heads(x_t, r_bf16):
    terms = jnp.concatenate(_split(x_t, 2), axis=0)
    return lax.dot_general(terms, jnp.concatenate([r_bf16, r_bf16], axis=0), (((0,), (0,)), ((), ())),
                           preferred_element_type=F32)


def _columns(x_t):
    pad = jnp.zeros((LANES - x_t.shape[0], x_t.shape[1]), F32)
    return jnp.concatenate([x_t, pad], axis=0).T


def _sigmoid(x):
    return 0.5 + 0.5 * jnp.tanh(0.5 * x)


def _silu(x):
    hx = 0.5 * x
    return hx + hx * jnp.tanh(hx)


def _softplus(x):
    return jnp.maximum(x, 0.0) + jnp.log1p(jnp.exp(-jnp.abs(x)))


def _rmsnorm(x, w):
    return x * lax.rsqrt(jnp.mean(x * x, axis=-1, keepdims=True) + EPS) * w


def _iota2(shape):
    return (lax.broadcasted_iota(jnp.int32, shape, 0), lax.broadcasted_iota(jnp.int32, shape, 1))


def _head_expand_matrix(src_off, n_heads, width):
    ri, ci = _iota2((SMALL_ROWS, n_heads * width))
    return jnp.where(ri - src_off == ci // width, 1.0, 0.0).astype(BF16)


def _block_ones(n, seg):
    ri, ci = _iota2((n, n))
    return jnp.where((ri // seg) == (ci // seg), 1.0, 0.0).astype(BF16)


def _shift_rows(x, tail, j):
    row = lax.broadcasted_iota(jnp.int32, (SUBLANES, x.shape[1]), 0)
    xs = pltpu.roll(x, j, axis=0)
    first = jnp.where(row < j, pltpu.roll(tail, j, axis=0), xs[0:SUBLANES])
    return jnp.concatenate([first, xs[SUBLANES:]], axis=0)


def _causal_conv(x, tail, w):
    assert CONV_K == 4
    x1 = _shift_rows(x, tail, 1)
    u = x * w[3:4, :] + x1 * w[2:3, :]
    v = x * w[1:2, :] + x1 * w[0:1, :]
    v_tail = tail * w[1:2, :] + pltpu.roll(tail, 1, axis=0) * w[0:1, :]
    return u + _shift_rows(v, v_tail, 2)


def _proj_pieces(x_ref, nw_ref, w_refs, wsmall_ref, convw_ref, convb_ref, tail_ref, proj_ref, small_ref):
    cell = {}

    def norm():
        x = x_ref[...].reshape(NB * TB, D_MODEL)
        cell["h"] = _bf(_rmsnorm(x, nw_ref[...]))

    def dot(key):
        c0, b = key

        def run():
            w_ref, w0 = (w_refs[0], c0) if c0 < W_SPLIT else (w_refs[1], c0 - W_SPLIT)
            cell[key] = jnp.dot(cell["h"][b * TB:(b + 1) * TB], w_ref[:, w0:w0 + PROJ_CHUNK],
                                preferred_element_type=F32)
        return run

    def post(key):
        c0, b = key

        def run():
            y = cell.pop(key)
            if CONV_LO <= c0 < CONV_HI:
                cols = slice(c0 - CONV_LO, c0 - CONV_LO + PROJ_CHUNK)
                conv = _causal_conv(y, tail_ref[b, :, cols], convw_ref[:, cols])
                tail_ref[b, :, cols] = y[TB - SUBLANES:TB]
                y = _silu(conv + convb_ref[:, cols])
            if c0 == Q_OFF:
                y = y * (ATT_HEAD_DIM ** -0.5)
            elif Z_OFF <= c0 < XBC_OFF or c0 >= GZ_OFF:
                y = _silu(y)
            elif c0 in (GDN_Q_OFF, GDN_K_OFF):
                ssq = jnp.dot(_bf(y * y), _block_ones(GDN_KDIM, GDN_HEAD_K), preferred_element_type=F32)
                y = y * lax.rsqrt(ssq + EPS)
                if c0 == GDN_Q_OFF:
                    y = y * (GDN_HEAD_K ** -0.5)
            proj_ref[b * TB:(b + 1) * TB, c0:c0 + PROJ_CHUNK] = y
        return run

    def small():
        small_ref[...] = lax.dot_general(wsmall_ref[...], cell["h"], (((1,), (1,)), ((), ())),
                                          preferred_element_type=F32)

    keys = [(c0, b) for c0 in range(0, PROJ_COLS, PROJ_CHUNK) for b in range(NB)]
    seq = [norm, dot(keys[0])]
    for prev, key in zip(keys[:-1], keys[1:]):
        seq += [dot(key), post(prev)]
    return seq + [post(keys[-1]), small]


def _attention(q, k, v, b, kprev_ref, vprev_ref, sinks_ref, layer, t, mix_ref):
    nk = ATT_QB + WINDOW
    kext = jnp.concatenate([kprev_ref[b], k], axis=0)
    vext = jnp.concatenate([vprev_ref[b], v], axis=0)
    kprev_ref[b] = k[TB - WINDOW:TB]
    vprev_ref[b] = v[TB - WINDOW:TB]
    qi, kj = _iota2((ATT_QB, nk))
    dist = jnp.abs(qi + WINDOW - kj).astype(F32)
    qc = qi // CHUNK
    kc = kj // CHUNK - WIN_CHUNKS
    in_band = jnp.where(kc <= qc, jnp.where(kc >= qc - WIN_CHUNKS, 1, 0), 0)
    grp = ATT_HEADS // ATT_KV_HEADS
    for a in range(TB // ATT_QB):
        first_chunk = t * (TB // CHUNK) + a * (ATT_QB // CHUNK)
        valid = (in_band * jnp.where(kc + first_chunk >= 0, 1, 0)) > 0
        rows = slice(a * ATT_QB, (a + 1) * ATT_QB)
        out_rows = slice(b * TB + a * ATT_QB, b * TB + (a + 1) * ATT_QB)
        krows = slice(a * ATT_QB, a * ATT_QB + nk)
        for kh in range(ATT_KV_HEADS):
            kk = kext[krows, kh * ATT_HEAD_DIM:(kh + 1) * ATT_HEAD_DIM]
            vv = vext[krows, kh * ATT_HEAD_DIM:(kh + 1) * ATT_HEAD_DIM]
            for gi in range(grp):
                h = kh * grp + gi
                slope = 2.0 ** (-8.0 * (h + 1) / ATT_HEADS)
                sink = sinks_ref[layer * ATT_HEADS + h]
                qh = q[rows, h * ATT_HEAD_DIM:(h + 1) * ATT_HEAD_DIM]
                s = _mm_nt(qh, kk) - slope * dist
                s = jnp.where(valid, s, MASKED)
                m = jnp.maximum(jnp.max(s, axis=-1, keepdims=True), sink)
                p = jnp.exp(s - m)
                den = jnp.sum(p, axis=-1, keepdims=True) + jnp.exp(sink - m)
                mix_ref[out_rows, h * ATT_HEAD_DIM:(h + 1) * ATT_HEAD_DIM] = _mm(p, vv) / den


def _ssd(xbc, z, sp_t, cs_t, cs_end_t, dvec, norm_w, b, state_ref, mix_ref):
    xs = xbc[:, :SSD_DIM]
    bm = xbc[:, SSD_DIM:SSD_DIM + SSD_GROUPS * SSD_STATE]
    cm = xbc[:, SSD_DIM + SSD_GROUPS * SSD_STATE:]
    n_sub = TB // SSD_CHUNK
    expand = _head_expand_matrix(SMALL_DT, SSD_HEADS, SSD_HEAD_DIM)
    cs = _columns(cs_t)
    xc = xs * _expand_heads(sp_t, expand)
    xcd = xc * _expand_heads(jnp.exp(cs_end_t - cs_t), expand)
    ecs = _expand_heads(jnp.exp(cs_t), expand)
    col = lax.broadcasted_iota(jnp.int32, (SMALL_ROWS, SMALL_ROWS), 1)
    totals = jnp.zeros((SMALL_ROWS, SMALL_ROWS), F32)
    for c in range(n_sub):
        totals = jnp.where(col == c, jnp.exp(cs_end_t[:, c * SSD_CHUNK:c * SSD_CHUNK + 1]), totals)
    etot = _expand_heads(totals, expand)
    ri, ci = _iota2((SSD_CHUNK, SSD_CHUNK))
    tri = ci <= ri
    gw = SSD_HPG * SSD_HEAD_DIM
    state = [state_ref[b, g] for g in range(SSD_GROUPS)]
    y_rows = []
    for c in range(n_sub):
        rs = slice(c * SSD_CHUNK, (c + 1) * SSD_CHUNK)
        ys = []
        for g in range(SSD_GROUPS):
            bg = bm[rs, g * SSD_STATE:(g + 1) * SSD_STATE]
            cg = cm[rs, g * SSD_STATE:(g + 1) * SSD_STATE]
            cb = _mm_nt(cg, bg)
            y_off = _mm(cg, state[g]) * ecs[rs, g * gw:(g + 1) * gw]
            state[g] = state[g] * etot[c:c + 1, g * gw:(g + 1) * gw] + _mm_tn(bg, xcd[rs, g * gw:(g + 1) * gw])
            yd = []
            for hh in range(SSD_HPG):
                h = g * SSD_HPG + hh
                diff = cs[rs, h:h + 1] - cs_t[h:h + 1, rs]
                lmat = jnp.exp(jnp.where(tri, diff, MASKED))
                yd.append(_mm(cb * lmat, xc[rs, h * SSD_HEAD_DIM:(h + 1) * SSD_HEAD_DIM]))
            ys.append(jnp.concatenate(yd, axis=1) + y_off)
        y_rows.append(jnp.concatenate(ys, axis=1))
    for g in range(SSD_GROUPS):
        state_ref[b, g] = state[g]
    y = jnp.concatenate(y_rows, axis=0) + xs * dvec
    gated = y * z
    rows = slice(b * TB, (b + 1) * TB)
    for g in range(SSD_GROUPS):
        gg = gated[:, g * gw:(g + 1) * gw]
        gg = gg * lax.rsqrt(jnp.mean(gg * gg, axis=-1, keepdims=True) + EPS)
        mix_ref[rows, ATT_DIM + g * gw:ATT_DIM + (g + 1) * gw] = gg * norm_w[:, g * gw:(g + 1) * gw]


def _gdn(per_b, norm_w, state_ref, mix_ref, fill):
    nb = len(per_b)
    ones = _block_ones(GDN_KDIM, GDN_HEAD_K)

    def seg_sum(x2):
        return jnp.dot(_bf(x2), ones, preferred_element_type=F32)

    exp_beta = _head_expand_matrix(SMALL_BETA, GDN_HEADS, GDN_HEAD_K)
    exp_dec = _head_expand_matrix(SMALL_DECAY, GDN_HEADS, GDN_HEAD_K)
    ri, ci = _iota2((GDN_SPAN, GDN_SPAN))
    same = (ri // CHUNK) == (ci // CHUNK)
    tri = jnp.logical_and(same, ci <= ri)
    strict = jnp.logical_and(same, ci < ri)
    heads = range(GDN_HEADS)
    n_chunks = TB // CHUNK
    n_sp = TB // GDN_SPAN
    hs = [slice(h * GDN_HEAD_K, (h + 1) * GDN_HEAD_K) for h in heads]
    rs = [slice(c * CHUNK, (c + 1) * CHUNK) for c in range(n_chunks)]
    spans = [slice(a * GDN_SPAN, (a + 1) * GDN_SPAN) for a in range(n_sp)]
    bh = [(b, h) for b in range(nb) for h in heads]

    p, sol, qk, qeg, kd = {}, {}, {}, [], []
    for b, (gq, _, beta_t, gc_t, g_end_t) in enumerate(per_b):
        q = gq[:, :GDN_KDIM]
        k = gq[:, GDN_KDIM:2 * GDN_KDIM]
        v = gq[:, 2 * GDN_KDIM:]
        qn, kn = q, k
        beta_b = _expand_heads(beta_t, exp_beta)
        eg_b = _expand_heads(jnp.exp(gc_t), exp_dec)
        kbeta = kn * beta_b
        vbeta = v * beta_b
        kbg = kbeta * eg_b
        qeg.append(qn * eg_b)
        kd.append(kn * _expand_heads(jnp.exp(g_end_t - gc_t), exp_dec))
        gc = _columns(gc_t)
        for a, sp in enumerate(spans):
            for h in heads:
                row = SMALL_DECAY + h
                diff = gc[sp, row:row + 1] - gc_t[row:row + 1, sp]
                decay = jnp.exp(jnp.where(tri, diff, MASKED))
                u = (b, a, h)
                p[u] = jnp.where(strict, _mm_nt(kbeta[sp, hs[h]], kn[sp, hs[h]]) * (-decay), 0.0)
                qk[u] = _mm_nt(qn[sp, hs[h]], kn[sp, hs[h]]) * decay
                sol[u] = jnp.concatenate([vbeta[sp, hs[h]], kbg[sp, hs[h]]], axis=1)
    units = list(p.keys())
    fill(FILL_GDN_PREP)
    steps = CHUNK.bit_length() - 1
    for j in range(steps):
        skip = (1 << j) // BF16_ROWS * BF16_ROWS
        keep = CHUNK - skip

        def live_rows(x):
            if skip == 0:
                return x
            return jnp.concatenate([x[c * CHUNK + skip:(c + 1) * CHUNK] for c in range(GDN_SPAN // CHUNK)], axis=0)

        def all_rows(r):
            if skip == 0:
                return r
            zeros = jnp.zeros((skip, r.shape[1]), r.dtype)
            return jnp.concatenate(
                [piece for c in range(GDN_SPAN // CHUNK) for piece in (zeros, r[c * keep:(c + 1) * keep])], axis=0)

        pb = {u: _bf(p[u]) for u in units}
        sb = {u: _bf(sol[u]) for u in units}
        if j + 1 < steps:
            res = {u: all_rows(jnp.dot(live_rows(pb[u]), jnp.concatenate([pb[u], sb[u]], axis=1),
                                       preferred_element_type=F32)) for u in units}
            p = {u: res[u][:, :GDN_SPAN] for u in units}
            sol = {u: sol[u] + res[u][:, GDN_SPAN:] for u in units}
        else:
            sol = {u: sol[u] + all_rows(jnp.dot(live_rows(pb[u]), sb[u], preferred_element_type=F32))
                   for u in units}
        fill(FILL_GDN_LEVEL)
    qks_u = {u: _mm(qk[u], sol[u]) for u in units}
    solh = {(b, h): jnp.concatenate([sol[(b, a, h)] for a in range(n_sp)], axis=0) for b, h in bh}
    qks = {(b, h): jnp.concatenate([qks_u[(b, a, h)] for a in range(n_sp)], axis=0) for b, h in bh}
    qeff = {(b, h): qeg[b][:, hs[h]] - qks[(b, h)][:, GDN_HEAD_V:] for b, h in bh}
    kuw = [{(b, h): _mm_tn(kd[b][rs[c], hs[h]], solh[(b, h)][rs[c]]) for b, h in bh} for c in range(n_chunks)]
    state = {(b, h): state_ref[b, h] for b, h in bh}
    states = []
    for c in range(n_chunks):
        states.append(state)
        col = (c + 1) * CHUNK - 1
        e_end = [jnp.exp(per_b[b][3][:, col:col + 1]) for b in range(nb)]
        state = {(b, h): state[(b, h)] * e_end[b][SMALL_DECAY + h:SMALL_DECAY + h + 1, :]
                 - _mm(kuw[c][(b, h)][:, GDN_HEAD_V:], state[(b, h)]) + kuw[c][(b, h)][:, :GDN_HEAD_V]
                 for b, h in bh}
    for b, h in bh:
        state_ref[b, h] = state[(b, h)]
    outs = [{(b, h): _mm(qeff[(b, h)][rs[c]], states[c][(b, h)]) + qks[(b, h)][rs[c], :GDN_HEAD_V]
             for b, h in bh} for c in range(n_chunks)]
    for b in range(nb):
        o = jnp.concatenate([jnp.concatenate([outs[c][(b, h)] for h in heads], axis=1)
                             for c in range(n_chunks)], axis=0)
        ms = seg_sum(o * o) * (1.0 / GDN_HEAD_V)
        mix_ref[b * TB:(b + 1) * TB, ATT_DIM + SSD_DIM:] = o * lax.rsqrt(ms + EPS) * norm_w * per_b[b][1]


def _mixer_kernel(layer,
                  sinks_ref, x_ref, xnext_ref, prew_ref, wina_ref, winb_ref, wsmall_ref, convw_ref, convb_ref,
                  wout_ref, postw_ref, bias_ref, alog_ref, dvec_ref, snormw_ref, gnormw_ref,
                  o_ref,
                  proj_ref, small_ref, pnext_ref, snext_ref, tail_ref, kprev_ref, vprev_ref, sstate_ref, gstate_ref, mix_ref):
    t = pl.program_id(1)
    proj_args = (prew_ref, (wina_ref, winb_ref), wsmall_ref, convw_ref, convb_ref, tail_ref, pnext_ref, snext_ref)

    @pl.when(t == 0)
    def _():
        tail_ref[...] = jnp.zeros_like(tail_ref)
        kprev_ref[...] = jnp.zeros_like(kprev_ref)
        vprev_ref[...] = jnp.zeros_like(vprev_ref)
        sstate_ref[...] = jnp.zeros_like(sstate_ref)
        gstate_ref[...] = jnp.zeros_like(gstate_ref)
        for piece in _proj_pieces(x_ref, *proj_args):
            piece()

    proj_ref[...] = pnext_ref[...]
    small_ref[...] = snext_ref[...]
    pieces = iter(_proj_pieces(xnext_ref, *proj_args))

    def fill(n):
        for _ in range(n):
            piece = next(pieces, None)
            if piece is not None:
                piece()

    def group(b, off, width):
        return proj_ref[b * TB:(b + 1) * TB, off:off + width]

    row = lax.broadcasted_iota(jnp.int32, (SMALL_ROWS, 1), 0)
    is_decay = jnp.where(row < SMALL_BETA, 1.0, jnp.where(row >= SMALL_DECAY, 1.0, 0.0))
    neg_a = -jnp.exp(alog_ref[...]) * is_decay
    ri, ci = _iota2((TB, TB))
    same = (ri // CHUNK) == (ci // CHUNK)
    same_ssd = (ri // SSD_CHUNK) == (ci // SSD_CHUNK)
    before = jnp.where(ri <= ci, 1.0, 0.0)
    m_ssd = jnp.where(same_ssd, before, 0.0).astype(BF16)
    m_whole_ssd = jnp.where(same_ssd, 1.0, 0.0).astype(BF16)
    m_chunk = jnp.where(same, before, 0.0).astype(BF16)
    m_whole_chunk = jnp.where(same, 1.0, 0.0).astype(BF16)
    gdn_in, ssd_in = [], []
    for b in range(NB):
        small_t = small_ref[:, b * TB:(b + 1) * TB]
        sp_t = _softplus(small_t + bias_ref[...])
        da_t = sp_t * neg_a
        ssd_in.append((sp_t, _cumsum_lanes(da_t, m_ssd), _cumsum_lanes(da_t, m_whole_ssd)))
        gc_t = _cumsum_lanes(da_t, m_chunk)
        g_end_t = _cumsum_lanes(da_t, m_whole_chunk)
        gdn_in.append((group(b, GQKV_OFF, GDN_CONV_DIM), group(b, GZ_OFF, GDN_DIM), _sigmoid(small_t),
                       gc_t, g_end_t))
    _gdn(gdn_in, gnormw_ref[...], gstate_ref, mix_ref, fill)
    for b in range(NB):
        _attention(group(b, Q_OFF, ATT_DIM), group(b, K_OFF, ATT_KV_DIM), group(b, V_OFF, ATT_KV_DIM),
                   b, kprev_ref, vprev_ref, sinks_ref, layer, t, mix_ref)
    for b in range(NB):
        sp_t, cs_t, cs_end_t = ssd_in[b]
        _ssd(group(b, XBC_OFF, SSD_CONV_DIM), group(b, Z_OFF, SSD_DIM), sp_t, cs_t, cs_end_t, dvec_ref[...],
             snormw_ref[...], b, sstate_ref, mix_ref)
    fill(PROJ_COLS)

    mixed = jnp.dot(_bf(mix_ref[...]), wout_ref[...], preferred_element_type=F32)
    y = _rmsnorm(mixed, postw_ref[...])
    for b in range(NB):
        o_ref[b] = x_ref[b] + y[b * TB:(b + 1) * TB]


def _mixer_call(x3d, params, layer):
    bsz, seq, _ = x3d.shape
    n_blk = seq // TB
    grid = (bsz // NB, n_blk)

    def par_spec(arr, **kw):
        shape = arr.shape[1:]
        return pl.BlockSpec((None,) + shape, lambda b, t: (layer,) + (0,) * len(shape), **kw)

    sinks, pre_norm, w_in_a, w_in_b, w_small_t, convw, convb, wout = params[:8]
    rest = params[8:]
    in_specs = ([pl.BlockSpec(memory_space=pltpu.SMEM),
                 pl.BlockSpec((NB, TB, D_MODEL), lambda b, t: (b, t, 0)),
                 pl.BlockSpec((NB, TB, D_MODEL), lambda b, t: (b, jnp.minimum(t + 1, n_blk - 1), 0)),
                 par_spec(pre_norm),
                 par_spec(w_in_a, pipeline_mode=pl.Buffered(1)),
                 par_spec(w_in_b, pipeline_mode=pl.Buffered(1)),
                 par_spec(w_small_t),
                 par_spec(convw),
                 par_spec(convb),
                 par_spec(wout, pipeline_mode=pl.Buffered(1))]
                + [par_spec(p) for p in rest])
    scratch = [
        pltpu.VMEM((NB * TB, PROJ_COLS), F32),
        pltpu.VMEM((SMALL_ROWS, NB * TB), F32),
        pltpu.VMEM((NB * TB, PROJ_COLS), F32),
        pltpu.VMEM((SMALL_ROWS, NB * TB), F32),
        pltpu.VMEM((NB, SUBLANES, CONV_COLS), F32),
        pltpu.VMEM((NB, WINDOW, ATT_KV_DIM), F32),
        pltpu.VMEM((NB, WINDOW, ATT_KV_DIM), F32),
        pltpu.VMEM((NB, SSD_GROUPS, SSD_STATE, SSD_HPG * SSD_HEAD_DIM), F32),
        pltpu.VMEM((NB, GDN_HEADS, GDN_HEAD_K, GDN_HEAD_V), F32),
        pltpu.VMEM((NB * TB, D_MODEL), F32),
    ]
    return pl.pallas_call(
        functools.partial(_mixer_kernel, layer),
        grid=grid,
        in_specs=in_specs,
        out_specs=pl.BlockSpec((NB, TB, D_MODEL), lambda b, t: (b, t, 0)),
        out_shape=jax.ShapeDtypeStruct(x3d.shape, F32),
        scratch_shapes=scratch,
        compiler_params=pltpu.CompilerParams(
            dimension_semantics=("arbitrary", "arbitrary"), vmem_limit_bytes=VMEM_LIMIT),
        name="mixer",
    )(sinks, x3d, x3d, pre_norm, w_in_a, w_in_b, w_small_t, convw, convb, wout, *rest)


def _ffn_kernel(x_ref, prew_ref, postw_ref, wg_ref, wu_ref, wd_ref, o_ref):
    x = x_ref[...]
    h = _bf(_rmsnorm(x, prew_ref[...]))
    acc = jnp.zeros((TM_FFN, D_MODEL), F32)
    for c in range(FF // FF_CHUNK):
        c0, c1 = c * FF_CHUNK, (c + 1) * FF_CHUNK
        gate = jnp.dot(h, wg_ref[:, c0:c1], preferred_element_type=F32)
        up = jnp.dot(h, wu_ref[:, c0:c1], preferred_element_type=F32)
        acc = acc + jnp.dot(_bf(_silu(gate) * up), wd_ref[c0:c1, :], preferred_element_type=F32)
    o_ref[...] = x + _rmsnorm(acc, postw_ref[...])


def _ffn_call(x2d, pre_norm, post_norm, wg, wu, wd, layer):
    m = x2d.shape[0]

    def wspec(shape):
        return pl.BlockSpec((None,) + shape, lambda i: (layer, 0, 0), pipeline_mode=pl.Buffered(1))

    return pl.pallas_call(
        _ffn_kernel,
        grid=(m // TM_FFN,),
        in_specs=[
            pl.BlockSpec((TM_FFN, D_MODEL), lambda i: (i, 0)),
            pl.BlockSpec((None, 1, D_MODEL), lambda i: (layer, 0, 0)),
            pl.BlockSpec((None, 1, D_MODEL), lambda i: (layer, 0, 0)),
            wspec((D_MODEL, FF)),
            wspec((D_MODEL, FF)),
            wspec((FF, D_MODEL)),
        ],
        out_specs=pl.BlockSpec((TM_FFN, D_MODEL), lambda i: (i, 0)),
        out_shape=jax.ShapeDtypeStruct(x2d.shape, F32),
        compiler_params=pltpu.CompilerParams(
            dimension_semantics=("arbitrary",), vmem_limit_bytes=VMEM_LIMIT),
        name="ffn",
    )(x2d, pre_norm, post_norm, wg, wu, wd)


def _regroup_w_in(w_in):
    dt0 = W_SPLIT
    g0 = dt0 + SSD_HEADS
    b0 = g0 + GDN_CONV_DIM + GDN_DIM
    small = jnp.concatenate([w_in[:, :, dt0:g0], w_in[:, :, b0:]], axis=-1)
    return (w_in[:, :, :dt0].astype(BF16), w_in[:, :, g0:b0].astype(BF16),
            jnp.swapaxes(small, 1, 2).astype(BF16))


def _pack_small(ssd_vals, gdn_vals):
    depth = ssd_vals.shape[0]
    mid = jnp.zeros((depth, GDN_HEADS), F32)
    return jnp.concatenate([ssd_vals.astype(F32), mid, gdn_vals.astype(F32)], axis=-1)[:, :, None]


def kernel(x, pre_mix_norm, post_mix_norm, pre_ffn_norm, post_ffn_norm, w_in, w_out, attn_sinks, ssd_conv_w, ssd_conv_b, ssd_dt_bias, ssd_A_log, ssd_D, ssd_norm_w, gdn_conv_w, gdn_dt_bias, gdn_A_log, gdn_norm_w, ffn_w_gate, ffn_w_up, ffn_w_down):
    bsz, seq, _ = x.shape
    depth = w_in.shape[0]
    assert seq % TB == 0 and (bsz * seq) % TM_FFN == 0 and bsz % NB == 0

    w_in_a, w_in_b, w_small_t = _regroup_w_in(w_in)
    vec3 = lambda a: a.reshape(depth, 1, a.shape[-1]).astype(F32)
    convw = jnp.concatenate([ssd_conv_w, gdn_conv_w], axis=-1).astype(F32)
    convb = jnp.concatenate([ssd_conv_b.astype(F32), jnp.zeros((depth, GDN_CONV_DIM), F32)], axis=-1)[:, None, :]
    params = (
        attn_sinks.reshape(-1).astype(F32),
        vec3(pre_mix_norm),
        w_in_a,
        w_in_b,
        w_small_t,
        convw,
        convb,
        w_out.astype(BF16),
        vec3(post_mix_norm),
        _pack_small(ssd_dt_bias, gdn_dt_bias),
        _pack_small(ssd_A_log, gdn_A_log),
        vec3(jnp.repeat(ssd_D, SSD_HEAD_DIM, axis=-1)),
        vec3(ssd_norm_w),
        vec3(jnp.tile(gdn_norm_w, (1, GDN_HEADS))),
    )
    pre_ffn = vec3(pre_ffn_norm)
    post_ffn = vec3(post_ffn_norm)
    wg = ffn_w_gate.astype(BF16)
    wu = ffn_w_up.astype(BF16)
    wd = ffn_w_down.astype(BF16)

    for layer in range(depth):
        x = _mixer_call(x, params, layer)
        x = _ffn_call(x.reshape(bsz * seq, D_MODEL), pre_ffn, post_ffn, wg, wu, wd, layer).reshape(bsz, seq, D_MODEL)
    return x
```

```python
import functools

import jax
import jax.numpy as jnp
from jax import lax
from jax.experimental import pallas as pl
from jax.experimental.pallas import tpu as pltpu

F32 = jnp.float32
BF16 = jnp.bfloat16

D_MODEL = 1024
CHUNK = 64
EPS = 1e-6
CONV_K = 4
MASKED = -1e30

SSD_HEADS = 8
SSD_HEAD_DIM = 64
SSD_DIM = SSD_HEADS * SSD_HEAD_DIM
SSD_GROUPS = 2
SSD_STATE = 128
SSD_HPG = SSD_HEADS // SSD_GROUPS
SSD_CONV_DIM = SSD_DIM + 2 * SSD_GROUPS * SSD_STATE

ATT_HEADS = 4
ATT_KV_HEADS = 2
ATT_HEAD_DIM = 64
ATT_DIM = ATT_HEADS * ATT_HEAD_DIM
ATT_KV_DIM = ATT_KV_HEADS * ATT_HEAD_DIM
WINDOW = 128
WIN_CHUNKS = WINDOW // CHUNK

GDN_HEADS = 4
GDN_HEAD_K = 64
GDN_HEAD_V = 64
GDN_KDIM = GDN_HEADS * GDN_HEAD_K
GDN_DIM = GDN_HEADS * GDN_HEAD_V
GDN_CONV_DIM = 2 * GDN_KDIM + GDN_DIM

FF = 2816
LANES = 128
SUBLANES = 8
BF16_ROWS = 16

PROJ_WIDTHS = (ATT_DIM, ATT_KV_DIM, ATT_KV_DIM, SSD_DIM, SSD_CONV_DIM, GDN_CONV_DIM, GDN_DIM)
PROJ_OFFS = tuple(sum(PROJ_WIDTHS[:i]) for i in range(len(PROJ_WIDTHS)))
PROJ_COLS = sum(PROJ_WIDTHS)
Q_OFF, K_OFF, V_OFF, Z_OFF, XBC_OFF, GQKV_OFF, GZ_OFF = PROJ_OFFS
CONV_LO = XBC_OFF
CONV_HI = GZ_OFF
CONV_COLS = CONV_HI - CONV_LO
W_SPLIT = GQKV_OFF
W_B_SRC = W_SPLIT + SSD_HEADS
SMALL_ROWS = SSD_HEADS + 2 * GDN_HEADS
SMALL_DT = 0
SMALL_BETA = SSD_HEADS
SMALL_DECAY = SSD_HEADS + GDN_HEADS

TM_FFN = 512
FF_CHUNK = 256
TB = 256
NB = 2
PROJ_CHUNK = 256
ATT_QB = 128
GDN_SPAN = 128
SSD_CHUNK = 128
FILL_GDN_PREP = 20
FILL_GDN_LEVEL = 5
V7X_VMEM_BYTES = 64 * 1024 * 1024
VMEM_LIMIT = V7X_VMEM_BYTES * 7 // 8


def _bf(x):
    return x.astype(BF16)


def _mm(a, b):
    return jnp.dot(_bf(a), _bf(b), preferred_element_type=F32)


def _mm_nt(a, b):
    return lax.dot_general(_bf(a), _bf(b), (((1,), (1,)), ((), ())), preferred_element_type=F32)


def _mm_tn(a, b):
    return lax.dot_general(_bf(a), _bf(b), (((0,), (0,)), ((), ())), preferred_element_type=F32)


def _split(x, terms):
    out = []
    for _ in range(terms - 1):
        t = x.astype(BF16)
        out.append(t)
        x = x - t.astype(F32)
    out.append(x.astype(BF16))
    return out


def _cumsum_lanes(x_t, m_bf16):
    return sum(jnp.dot(t, m_bf16, preferred_element_type=F32) for t in _split(x_t, 3))


def _expand_heads(x_t, r_bf16):
    terms = jnp.concatenate(_split(x_t, 2), axis=0)
    return lax.dot_general(terms, jnp.concatenate([r_bf16, r_bf16], axis=0), (((0,), (0,)), ((), ())),
                           preferred_element_type=F32)


def _columns(x_t):
    pad = jnp.zeros((LANES - x_t.shape[0], x_t.shape[1]), F32)
    return jnp.concatenate([x_t, pad], axis=0).T


def _sigmoid(x):
    return 0.5 + 0.5 * jnp.tanh(0.5 * x)


def _silu(x):
    hx = 0.5 * x
    return hx + hx * jnp.tanh(hx)


def _softplus(x):
    return jnp.maximum(x, 0.0) + jnp.log1p(jnp.exp(-jnp.abs(x)))


def _rmsnorm(x, w):
    return x * lax.rsqrt(jnp.mean(x * x, axis=-1, keepdims=True) + EPS) * w


def _iota2(shape):
    return (lax.broadcasted_iota(jnp.int32, shape, 0), lax.broadcasted_iota(jnp.int32, shape, 1))


def _head_expand_matrix(src_off, n_heads, width):
    ri, ci = _iota2((SMALL_ROWS, n_heads * width))
    return jnp.where(ri - src_off == ci // width, 1.0, 0.0).astype(BF16)


def _block_ones(n, seg):
    ri, ci = _iota2((n, n))
    return jnp.where((ri // seg) == (ci // seg), 1.0, 0.0).astype(BF16)


def _shift_rows(x, tail, j):
    row = lax.broadcasted_iota(jnp.int32, (SUBLANES, x.shape[1]), 0)
    xs = pltpu.roll(x, j, axis=0)
    first = jnp.where(row < j, pltpu.roll(tail, j, axis=0), xs[0:SUBLANES])
    return jnp.concatenate([first, xs[SUBLANES:]], axis=0)


def _causal_conv(x, tail, w):
    assert CONV_K == 4
    x1 = _shift_rows(x, tail, 1)
    u = x * w[3:4, :] + x1 * w[2:3, :]
    v = x * w[1:2, :] + x1 * w[0:1, :]
    v_tail = tail * w[1:2, :] + pltpu.roll(tail, 1, axis=0) * w[0:1, :]
    return u + _shift_rows(v, v_tail, 2)


def _proj_pieces(x_ref, nw_ref, w_refs, wsmall_ref, convw_ref, convb_ref, tail_ref, proj_ref, small_ref):
    cell = {}

    def norm():
        x = x_ref[...].reshape(NB * TB, D_MODEL)
        cell["h"] = _bf(_rmsnorm(x, nw_ref[...]))

    def dot(key):
        c0, b = key

        def run():
            w_ref, w0 = (w_refs[0], c0) if c0 < W_SPLIT else (w_refs[1], c0 - W_SPLIT)
            cell[key] = jnp.dot(cell["h"][b * TB:(b + 1) * TB], w_ref[:, w0:w0 + PROJ_CHUNK],
                                preferred_element_type=F32)
        return run

    def post(key):
        c0, b = key

        def run():
            y = cell.pop(key)
            if CONV_LO <= c0 < CONV_HI:
                cols = slice(c0 - CONV_LO, c0 - CONV_LO + PROJ_CHUNK)
                conv = _causal_conv(y, tail_ref[b, :, cols], convw_ref[:, cols])
                tail_ref[b, :, cols] = y[TB - SUBLANES:TB]
                y = _silu(conv + convb_ref[:, cols])
            proj_ref[b * TB:(b + 1) * TB, c0:c0 + PROJ_CHUNK] = y
        return run

    def small():
        small_ref[...] = lax.dot_general(wsmall_ref[...], cell["h"], (((1,), (1,)), ((), ())),
                                          preferred_element_type=F32)

    keys = [(c0, b) for c0 in range(0, PROJ_COLS, PROJ_CHUNK) for b in range(NB)]
    seq = [norm, dot(keys[0])]
    for prev, key in zip(keys[:-1], keys[1:]):
        seq += [dot(key), post(prev)]
    return seq + [post(keys[-1]), small]


def _attention(q, k, v, b, kprev_ref, vprev_ref, sinks_ref, layer, t, mix_ref):
    nk = ATT_QB + WINDOW
    kext = jnp.concatenate([kprev_ref[b], k], axis=0)
    vext = jnp.concatenate([vprev_ref[b], v], axis=0)
    kprev_ref[b] = k[TB - WINDOW:TB]
    vprev_ref[b] = v[TB - WINDOW:TB]
    qi, kj = _iota2((ATT_QB, nk))
    dist = jnp.abs(qi + WINDOW - kj).astype(F32)
    qc = qi // CHUNK
    kc = kj // CHUNK - WIN_CHUNKS
    in_band = jnp.where(kc <= qc, jnp.where(kc >= qc - WIN_CHUNKS, 1, 0), 0)
    grp = ATT_HEADS // ATT_KV_HEADS
    for a in range(TB // ATT_QB):
        first_chunk = t * (TB // CHUNK) + a * (ATT_QB // CHUNK)
        valid = (in_band * jnp.where(kc + first_chunk >= 0, 1, 0)) > 0
        rows = slice(a * ATT_QB, (a + 1) * ATT_QB)
        out_rows = slice(b * TB + a * ATT_QB, b * TB + (a + 1) * ATT_QB)
        krows = slice(a * ATT_QB, a * ATT_QB + nk)
        for kh in range(ATT_KV_HEADS):
            kk = kext[krows, kh * ATT_HEAD_DIM:(kh + 1) * ATT_HEAD_DIM]
            vv = vext[krows, kh * ATT_HEAD_DIM:(kh + 1) * ATT_HEAD_DIM]
            for gi in range(grp):
                h = kh * grp + gi
                slope = 2.0 ** (-8.0 * (h + 1) / ATT_HEADS)
                sink = sinks_ref[layer * ATT_HEADS + h]
                qh = q[rows, h * ATT_HEAD_DIM:(h + 1) * ATT_HEAD_DIM]
                s = _mm_nt(qh, kk) * (ATT_HEAD_DIM ** -0.5) - slope * dist
                s = jnp.where(valid, s, MASKED)
                m = jnp.maximum(jnp.max(s, axis=-1, keepdims=True), sink)
                p = jnp.exp(s - m)
                den = jnp.sum(p, axis=-1, keepdims=True) + jnp.exp(sink - m)
                mix_ref[out_rows, h * ATT_HEAD_DIM:(h + 1) * ATT_HEAD_DIM] = _mm(p, vv) / den


def _ssd(xbc, z, sp_t, cs_t, cs_end_t, dvec, norm_w, b, state_ref, mix_ref):
    xs = xbc[:, :SSD_DIM]
    bm = xbc[:, SSD_DIM:SSD_DIM + SSD_GROUPS * SSD_STATE]
    cm = xbc[:, SSD_DIM + SSD_GROUPS * SSD_STATE:]
    n_sub = TB // SSD_CHUNK
    expand = _head_expand_matrix(SMALL_DT, SSD_HEADS, SSD_HEAD_DIM)
    cs = _columns(cs_t)
    xc = xs * _expand_heads(sp_t, expand)
    xcd = xc * _expand_heads(jnp.exp(cs_end_t - cs_t), expand)
    ecs = _expand_heads(jnp.exp(cs_t), expand)
    col = lax.broadcasted_iota(jnp.int32, (SMALL_ROWS, SMALL_ROWS), 1)
    totals = jnp.zeros((SMALL_ROWS, SMALL_ROWS), F32)
    for c in range(n_sub):
        totals = jnp.where(col == c, jnp.exp(cs_end_t[:, c * SSD_CHUNK:c * SSD_CHUNK + 1]), totals)
    etot = _expand_heads(totals, expand)
    ri, ci = _iota2((SSD_CHUNK, SSD_CHUNK))
    tri = ci <= ri
    gw = SSD_HPG * SSD_HEAD_DIM
    state = [state_ref[b, g] for g in range(SSD_GROUPS)]
    y_rows = []
    for c in range(n_sub):
        rs = slice(c * SSD_CHUNK, (c + 1) * SSD_CHUNK)
        ys = []
        for g in range(SSD_GROUPS):
            bg = bm[rs, g * SSD_STATE:(g + 1) * SSD_STATE]
            cg = cm[rs, g * SSD_STATE:(g + 1) * SSD_STATE]
            cb = _mm_nt(cg, bg)
            y_off = _mm(cg, state[g]) * ecs[rs, g * gw:(g + 1) * gw]
            state[g] = state[g] * etot[c:c + 1, g * gw:(g + 1) * gw] + _mm_tn(bg, xcd[rs, g * gw:(g + 1) * gw])
            yd = []
            for hh in range(SSD_HPG):
                h = g * SSD_HPG + hh
                diff = cs[rs, h:h + 1] - cs_t[h:h + 1, rs]
                lmat = jnp.exp(jnp.where(tri, diff, MASKED))
                yd.append(_mm(cb * lmat, xc[rs, h * SSD_HEAD_DIM:(h + 1) * SSD_HEAD_DIM]))
            ys.append(jnp.concatenate(yd, axis=1) + y_off)
        y_rows.append(jnp.concatenate(ys, axis=1))
    for g in range(SSD_GROUPS):
        state_ref[b, g] = state[g]
    y = jnp.concatenate(y_rows, axis=0) + xs * dvec
    gated = y * _silu(z)
    rows = slice(b * TB, (b + 1) * TB)
    for g in range(SSD_GROUPS):
        gg = gated[:, g * gw:(g + 1) * gw]
        gg = gg * lax.rsqrt(jnp.mean(gg * gg, axis=-1, keepdims=True) + EPS)
        mix_ref[rows, ATT_DIM + g * gw:ATT_DIM + (g + 1) * gw] = gg * norm_w[:, g * gw:(g + 1) * gw]


def _gdn(per_b, norm_w, state_ref, mix_ref, fill):
    nb = len(per_b)
    ones = _block_ones(GDN_KDIM, GDN_HEAD_K)

    def seg_sum(x2):
        return jnp.dot(_bf(x2), ones, preferred_element_type=F32)

    exp_beta = _head_expand_matrix(SMALL_BETA, GDN_HEADS, GDN_HEAD_K)
    exp_dec = _head_expand_matrix(SMALL_DECAY, GDN_HEADS, GDN_HEAD_K)
    ri, ci = _iota2((GDN_SPAN, GDN_SPAN))
    same = (ri // CHUNK) == (ci // CHUNK)
    tri = jnp.logical_and(same, ci <= ri)
    strict = jnp.logical_and(same, ci < ri)
    heads = range(GDN_HEADS)
    n_chunks = TB // CHUNK
    n_sp = TB // GDN_SPAN
    hs = [slice(h * GDN_HEAD_K, (h + 1) * GDN_HEAD_K) for h in heads]
    rs = [slice(c * CHUNK, (c + 1) * CHUNK) for c in range(n_chunks)]
    spans = [slice(a * GDN_SPAN, (a + 1) * GDN_SPAN) for a in range(n_sp)]
    bh = [(b, h) for b in range(nb) for h in heads]

    p, sol, qk, qeg, kd = {}, {}, {}, [], []
    for b, (gq, _, beta_t, gc_t, g_end_t) in enumerate(per_b):
        q = gq[:, :GDN_KDIM]
        k = gq[:, GDN_KDIM:2 * GDN_KDIM]
        v = gq[:, 2 * GDN_KDIM:]
        qn = q * lax.rsqrt(seg_sum(q * q) + EPS) * (GDN_HEAD_K ** -0.5)
        kn = k * lax.rsqrt(seg_sum(k * k) + EPS)
        beta_b = _expand_heads(beta_t, exp_beta)
        eg_b = _expand_heads(jnp.exp(gc_t), exp_dec)
        kbeta = kn * beta_b
        vbeta = v * beta_b
        kbg = kbeta * eg_b
        qeg.append(qn * eg_b)
        kd.append(kn * _expand_heads(jnp.exp(g_end_t - gc_t), exp_dec))
        gc = _columns(gc_t)
        for a, sp in enumerate(spans):
            for h in heads:
                row = SMALL_DECAY + h
                diff = gc[sp, row:row + 1] - gc_t[row:row + 1, sp]
                decay = jnp.exp(jnp.where(tri, diff, MASKED))
                u = (b, a, h)
                p[u] = jnp.where(strict, _mm_nt(kbeta[sp, hs[h]], kn[sp, hs[h]]) * (-decay), 0.0)
                qk[u] = _mm_nt(qn[sp, hs[h]], kn[sp, hs[h]]) * decay
                sol[u] = jnp.concatenate([vbeta[sp, hs[h]], kbg[sp, hs[h]]], axis=1)
    units = list(p.keys())
    fill(FILL_GDN_PREP)
    steps = CHUNK.bit_length() - 1
    for j in range(steps):
        skip = (1 << j) // BF16_ROWS * BF16_ROWS
        keep = CHUNK - skip

        def live_rows(x):
            if skip == 0:
                return x
            return jnp.concatenate([x[c * CHUNK + skip:(c + 1) * CHUNK] for c in range(GDN_SPAN // CHUNK)], axis=0)

        def all_rows(r):
            if skip == 0:
                return r
            zeros = jnp.zeros((skip, r.shape[1]), r.dtype)
            return jnp.concatenate(
                [piece for c in range(GDN_SPAN // CHUNK) for piece in (zeros, r[c * keep:(c + 1) * keep])], axis=0)

        pb = {u: _bf(p[u]) for u in units}
        sb = {u: _bf(sol[u]) for u in units}
        if j + 1 < steps:
            res = {u: all_rows(jnp.dot(live_rows(pb[u]), jnp.concatenate([pb[u], sb[u]], axis=1),
                                       preferred_element_type=F32)) for u in units}
            p = {u: res[u][:, :GDN_SPAN] for u in units}
            sol = {u: sol[u] + res[u][:, GDN_SPAN:] for u in units}
        else:
            sol = {u: sol[u] + all_rows(jnp.dot(live_rows(pb[u]), sb[u], preferred_element_type=F32))
                   for u in units}
        fill(FILL_GDN_LEVEL)
    qks_u = {u: _mm(qk[u], sol[u]) for u in units}
    solh = {(b, h): jnp.concatenate([sol[(b, a, h)] for a in range(n_sp)], axis=0) for b, h in bh}
    qks = {(b, h): jnp.concatenate([qks_u[(b, a, h)] for a in range(n_sp)], axis=0) for b, h in bh}
    qeff = {(b, h): qeg[b][:, hs[h]] - qks[(b, h)][:, GDN_HEAD_V:] for b, h in bh}
    kuw = [{(b, h): _mm_tn(kd[b][rs[c], hs[h]], solh[(b, h)][rs[c]]) for b, h in bh} for c in range(n_chunks)]
    state = {(b, h): state_ref[b, h] for b, h in bh}
    states = []
    for c in range(n_chunks):
        states.append(state)
        col = (c + 1) * CHUNK - 1
        e_end = [jnp.exp(per_b[b][3][:, col:col + 1]) for b in range(nb)]
        state = {(b, h): state[(b, h)] * e_end[b][SMALL_DECAY + h:SMALL_DECAY + h + 1, :]
                 - _mm(kuw[c][(b, h)][:, GDN_HEAD_V:], state[(b, h)]) + kuw[c][(b, h)][:, :GDN_HEAD_V]
                 for b, h in bh}
    for b, h in bh:
        state_ref[b, h] = state[(b, h)]
    outs = [{(b, h): _mm(qeff[(b, h)][rs[c]], states[c][(b, h)]) + qks[(b, h)][rs[c], :GDN_HEAD_V]
             for b, h in bh} for c in range(n_chunks)]
    for b in range(nb):
        o = jnp.concatenate([jnp.concatenate([outs[c][(b, h)] for h in heads], axis=1)
                             for c in range(n_chunks)], axis=0)
        ms = seg_sum(o * o) * (1.0 / GDN_HEAD_V)
        mix_ref[b * TB:(b + 1) * TB, ATT_DIM + SSD_DIM:] = o * lax.rsqrt(ms + EPS) * norm_w * _silu(per_b[b][1])


def _mixer_kernel(layer,
                  sinks_ref, x_ref, xnext_ref, prew_ref, win_ref, wsmall_ref, convw_ref, convb_ref,
                  wout_ref, postw_ref, bias_ref, alog_ref, dvec_ref, snormw_ref, gnormw_ref,
                  o_ref,
                  proj_ref, small_ref, pnext_ref, snext_ref, tail_ref, kprev_ref, vprev_ref, sstate_ref, gstate_ref, mix_ref,
                  winb_ref):
    t = pl.program_id(1)
    proj_args = (prew_ref, (win_ref, winb_ref), wsmall_ref, convw_ref, convb_ref, tail_ref, pnext_ref, snext_ref)

    @pl.when(t == 0)
    def _():
        winb_ref[...] = win_ref[:, W_B_SRC:W_B_SRC + PROJ_COLS - W_SPLIT]
        tail_ref[...] = jnp.zeros_like(tail_ref)
        kprev_ref[...] = jnp.zeros_like(kprev_ref)
        vprev_ref[...] = jnp.zeros_like(vprev_ref)
        sstate_ref[...] = jnp.zeros_like(sstate_ref)
        gstate_ref[...] = jnp.zeros_like(gstate_ref)
        for piece in _proj_pieces(x_ref, *proj_args):
            piece()

    proj_ref[...] = pnext_ref[...]
    small_ref[...] = snext_ref[...]
    pieces = iter(_proj_pieces(xnext_ref, *proj_args))

    def fill(n):
        for _ in range(n):
            piece = next(pieces, None)
            if piece is not None:
                piece()

    def group(b, off, width):
        return proj_ref[b * TB:(b + 1) * TB, off:off + width]

    row = lax.broadcasted_iota(jnp.int32, (SMALL_ROWS, 1), 0)
    is_decay = jnp.where(row < SMALL_BETA, 1.0, jnp.where(row >= SMALL_DECAY, 1.0, 0.0))
    neg_a = -jnp.exp(alog_ref[...]) * is_decay
    ri, ci = _iota2((TB, TB))
    same = (ri // CHUNK) == (ci // CHUNK)
    same_ssd = (ri // SSD_CHUNK) == (ci // SSD_CHUNK)
    before = jnp.where(ri <= ci, 1.0, 0.0)
    m_ssd = jnp.where(same_ssd, before, 0.0).astype(BF16)
    m_whole_ssd = jnp.where(same_ssd, 1.0, 0.0).astype(BF16)
    m_chunk = jnp.where(same, before, 0.0).astype(BF16)
    m_whole_chunk = jnp.where(same, 1.0, 0.0).astype(BF16)
    gdn_in, ssd_in = [], []
    for b in range(NB):
        small_t = small_ref[:, b * TB:(b + 1) * TB]
        sp_t = _softplus(small_t + bias_ref[...])
        da_t = sp_t * neg_a
        ssd_in.append((sp_t, _cumsum_lanes(da_t, m_ssd), _cumsum_lanes(da_t, m_whole_ssd)))
        gc_t = _cumsum_lanes(da_t, m_chunk)
        g_end_t = _cumsum_lanes(da_t, m_whole_chunk)
        gdn_in.append((group(b, GQKV_OFF, GDN_CONV_DIM), group(b, GZ_OFF, GDN_DIM), _sigmoid(small_t),
                       gc_t, g_end_t))
    _gdn(gdn_in, gnormw_ref[...], gstate_ref, mix_ref, fill)
    for b in range(NB):
        _attention(group(b, Q_OFF, ATT_DIM), group(b, K_OFF, ATT_KV_DIM), group(b, V_OFF, ATT_KV_DIM),
                   b, kprev_ref, vprev_ref, sinks_ref, layer, t, mix_ref)
    for b in range(NB):
        sp_t, cs_t, cs_end_t = ssd_in[b]
        _ssd(group(b, XBC_OFF, SSD_CONV_DIM), group(b, Z_OFF, SSD_DIM), sp_t, cs_t, cs_end_t, dvec_ref[...],
             snormw_ref[...], b, sstate_ref, mix_ref)
    fill(PROJ_COLS)

    mixed = jnp.dot(_bf(mix_ref[...]), wout_ref[...], preferred_element_type=F32)
    y = _rmsnorm(mixed, postw_ref[...])
    for b in range(NB):
        o_ref[b] = x_ref[b] + y[b * TB:(b + 1) * TB]


def _mixer_call(x3d, params, layer):
    bsz, seq, _ = x3d.shape
    n_blk = seq // TB
    grid = (bsz // NB, n_blk)

    def par_spec(arr, **kw):
        shape = arr.shape[1:]
        return pl.BlockSpec((None,) + shape, lambda b, t: (layer,) + (0,) * len(shape), **kw)

    sinks, pre_norm, w_in_bf, w_small_t, convw, convb, wout = params[:7]
    rest = params[7:]
    in_specs = ([pl.BlockSpec(memory_space=pltpu.SMEM),
                 pl.BlockSpec((NB, TB, D_MODEL), lambda b, t: (b, t, 0)),
                 pl.BlockSpec((NB, TB, D_MODEL), lambda b, t: (b, jnp.minimum(t + 1, n_blk - 1), 0)),
                 par_spec(pre_norm),
                 par_spec(w_in_bf, pipeline_mode=pl.Buffered(1)),
                 par_spec(w_small_t),
                 par_spec(convw),
                 par_spec(convb),
                 par_spec(wout, pipeline_mode=pl.Buffered(1))]
                + [par_spec(p) for p in rest])
    scratch = [
        pltpu.VMEM((NB * TB, PROJ_COLS), F32),
        pltpu.VMEM((SMALL_ROWS, NB * TB), F32),
        pltpu.VMEM((NB * TB, PROJ_COLS), F32),
        pltpu.VMEM((SMALL_ROWS, NB * TB), F32),
        pltpu.VMEM((NB, SUBLANES, CONV_COLS), F32),
        pltpu.VMEM((NB, WINDOW, ATT_KV_DIM), F32),
        pltpu.VMEM((NB, WINDOW, ATT_KV_DIM), F32),
        pltpu.VMEM((NB, SSD_GROUPS, SSD_STATE, SSD_HPG * SSD_HEAD_DIM), F32),
        pltpu.VMEM((NB, GDN_HEADS, GDN_HEAD_K, GDN_HEAD_V), F32),
        pltpu.VMEM((NB * TB, D_MODEL), F32),
        pltpu.VMEM((D_MODEL, PROJ_COLS - W_SPLIT), BF16),
    ]
    return pl.pallas_call(
        functools.partial(_mixer_kernel, layer),
        grid=grid,
        in_specs=in_specs,
        out_specs=pl.BlockSpec((NB, TB, D_MODEL), lambda b, t: (b, t, 0)),
        out_shape=jax.ShapeDtypeStruct(x3d.shape, F32),
        scratch_shapes=scratch,
        compiler_params=pltpu.CompilerParams(
            dimension_semantics=("arbitrary", "arbitrary"), vmem_limit_bytes=VMEM_LIMIT),
        name="mixer",
    )(sinks, x3d, x3d, pre_norm, w_in_bf, w_small_t, convw, convb, wout, *rest)


def _ffn_kernel(x_ref, prew_ref, postw_ref, wg_ref, wu_ref, wd_ref, o_ref):
    x = x_ref[...]
    h = _bf(_rmsnorm(x, prew_ref[...]))
    acc = jnp.zeros((TM_FFN, D_MODEL), F32)
    for c in range(FF // FF_CHUNK):
        c0, c1 = c * FF_CHUNK, (c + 1) * FF_CHUNK
        gate = jnp.dot(h, wg_ref[:, c0:c1], preferred_element_type=F32)
        up = jnp.dot(h, wu_ref[:, c0:c1], preferred_element_type=F32)
        acc = acc + jnp.dot(_bf(_silu(gate) * up), wd_ref[c0:c1, :], preferred_element_type=F32)
    o_ref[...] = x + _rmsnorm(acc, postw_ref[...])


def _ffn_call(x2d, pre_norm, post_norm, wg, wu, wd, layer):
    m = x2d.shape[0]

    def wspec(shape):
        return pl.BlockSpec((None,) + shape, lambda i: (layer, 0, 0), pipeline_mode=pl.Buffered(1))

    return pl.pallas_call(
        _ffn_kernel,
        grid=(m // TM_FFN,),
        in_specs=[
            pl.BlockSpec((TM_FFN, D_MODEL), lambda i: (i, 0)),
            pl.BlockSpec((None, 1, D_MODEL), lambda i: (layer, 0, 0)),
            pl.BlockSpec((None, 1, D_MODEL), lambda i: (layer, 0, 0)),
            wspec((D_MODEL, FF)),
            wspec((D_MODEL, FF)),
            wspec((FF, D_MODEL)),
        ],
        out_specs=pl.BlockSpec((TM_FFN, D_MODEL), lambda i: (i, 0)),
        out_shape=jax.ShapeDtypeStruct(x2d.shape, F32),
        compiler_params=pltpu.CompilerParams(
            dimension_semantics=("arbitrary",), vmem_limit_bytes=VMEM_LIMIT),
        name="ffn",
    )(x2d, pre_norm, post_norm, wg, wu, wd)


def _regroup_w_in(w_in):
    b0 = W_B_SRC + PROJ_COLS - W_SPLIT
    small = jnp.concatenate([w_in[:, :, W_SPLIT:W_B_SRC], w_in[:, :, b0:]], axis=-1)
    small = lax.optimization_barrier(small)
    return w_in.astype(BF16), jnp.swapaxes(small, 1, 2).astype(BF16)


def _pack_small(ssd_vals, gdn_vals):
    depth = ssd_vals.shape[0]
    mid = jnp.zeros((depth, GDN_HEADS), F32)
    return jnp.concatenate([ssd_vals.astype(F32), mid, gdn_vals.astype(F32)], axis=-1)[:, :, None]


def kernel(x, pre_mix_norm, post_mix_norm, pre_ffn_norm, post_ffn_norm, w_in, w_out, attn_sinks, ssd_conv_w, ssd_conv_b, ssd_dt_bias, ssd_A_log, ssd_D, ssd_norm_w, gdn_conv_w, gdn_dt_bias, gdn_A_log, gdn_norm_w, ffn_w_gate, ffn_w_up, ffn_w_down):
    bsz, seq, _ = x.shape
    depth = w_in.shape[0]
    assert seq % TB == 0 and (bsz * seq) % TM_FFN == 0 and bsz % NB == 0

    w_in_bf, w_small_t = _regroup_w_in(w_in)
    vec3 = lambda a: a.reshape(depth, 1, a.shape[-1]).astype(F32)
    convw = jnp.concatenate([ssd_conv_w, gdn_conv_w], axis=-1).astype(F32)
    convb = jnp.concatenate([ssd_conv_b.astype(F32), jnp.zeros((depth, GDN_CONV_DIM), F32)], axis=-1)[:, None, :]
    params = (
        attn_sinks.reshape(-1).astype(F32),
        vec3(pre_mix_norm),
        w_in_bf,
        w_small_t,
        convw,
        convb,
        w_out.astype(BF16),
        vec3(post_mix_norm),
        _pack_small(ssd_dt_bias, gdn_dt_bias),
        _pack_small(ssd_A_log, gdn_A_log),
        vec3(jnp.repeat(ssd_D, SSD_HEAD_DIM, axis=-1)),
        vec3(ssd_norm_w),
        vec3(jnp.tile(gdn_norm_w, (1, GDN_HEADS))),
    )
    pre_ffn = vec3(pre_ffn_norm)
    post_ffn = vec3(post_ffn_norm)
    wg = ffn_w_gate.astype(BF16)
    wu = ffn_w_up.astype(BF16)
    wd = ffn_w_down.astype(BF16)

    for layer in range(depth):
        x = _mixer_call(x, params, layer)
        x = _ffn_call(x.reshape(bsz * seq, D_MODEL), pre_ffn, post_ffn, wg, wu, wd, layer).reshape(bsz, seq, D_MODEL)
    return x
```

```python
import functools

import jax
import jax.numpy as jnp
from jax import lax
from jax.experimental import pallas as pl
from jax.experimental.pallas import tpu as pltpu

F32 = jnp.float32
BF16 = jnp.bfloat16

D_MODEL = 1024
CHUNK = 64
EPS = 1e-6
CONV_K = 4
MASKED = -1e30

SSD_HEADS = 8
SSD_HEAD_DIM = 64
SSD_DIM = SSD_HEADS * SSD_HEAD_DIM
SSD_GROUPS = 2
SSD_STATE = 128
SSD_HPG = SSD_HEADS // SSD_GROUPS
SSD_CONV_DIM = SSD_DIM + 2 * SSD_GROUPS * SSD_STATE

ATT_HEADS = 4
ATT_KV_HEADS = 2
ATT_HEAD_DIM = 64
ATT_DIM = ATT_HEADS * ATT_HEAD_DIM
ATT_KV_DIM = ATT_KV_HEADS * ATT_HEAD_DIM
WINDOW = 128
WIN_CHUNKS = WINDOW // CHUNK

GDN_HEADS = 4
GDN_HEAD_K = 64
GDN_HEAD_V = 64
GDN_KDIM = GDN_HEADS * GDN_HEAD_K
GDN_DIM = GDN_HEADS * GDN_HEAD_V
GDN_CONV_DIM = 2 * GDN_KDIM + GDN_DIM

FF = 2816
LANES = 128
SUBLANES = 8
BF16_ROWS = 16

PROJ_WIDTHS = (ATT_DIM, ATT_KV_DIM, ATT_KV_DIM, SSD_DIM, SSD_CONV_DIM, GDN_CONV_DIM, GDN_DIM)
PROJ_OFFS = tuple(sum(PROJ_WIDTHS[:i]) for i in range(len(PROJ_WIDTHS)))
PROJ_COLS = sum(PROJ_WIDTHS)
Q_OFF, K_OFF, V_OFF, Z_OFF, XBC_OFF, GQKV_OFF, GZ_OFF = PROJ_OFFS
CONV_LO = XBC_OFF
CONV_HI = GZ_OFF
CONV_COLS = CONV_HI - CONV_LO
W_SPLIT = GQKV_OFF
W_B_SRC = W_SPLIT + SSD_HEADS
SMALL_ROWS = SSD_HEADS + 2 * GDN_HEADS
SMALL_DT = 0
SMALL_BETA = SSD_HEADS
SMALL_DECAY = SSD_HEADS + GDN_HEADS

VEC_WIDTHS = (D_MODEL, D_MODEL, CONV_COLS, SSD_DIM, SSD_DIM, GDN_DIM)
VEC_OFFS = tuple(sum(VEC_WIDTHS[:i]) for i in range(len(VEC_WIDTHS)))

TM_FFN = 1024
FF_CHUNK = 256
TB = 256
NB = 2
PROJ_CHUNK = 256
ATT_QB = 128
GDN_SPAN = 128
SSD_CHUNK = 128
FILL_GDN_PREP = 20
FILL_GDN_LEVEL = 5
V7X_VMEM_BYTES = 64 * 1024 * 1024
VMEM_LIMIT = V7X_VMEM_BYTES * 7 // 8


def _bf(x):
    return x.astype(BF16)


def _mm(a, b):
    return jnp.dot(_bf(a), _bf(b), preferred_element_type=F32)


def _mm_nt(a, b):
    return lax.dot_general(_bf(a), _bf(b), (((1,), (1,)), ((), ())), preferred_element_type=F32)


def _mm_tn(a, b):
    return lax.dot_general(_bf(a), _bf(b), (((0,), (0,)), ((), ())), preferred_element_type=F32)


def _split(x, terms):
    out = []
    for _ in range(terms - 1):
        t = x.astype(BF16)
        out.append(t)
        x = x - t.astype(F32)
    out.append(x.astype(BF16))
    return out


def _cumsum_lanes(x_t, m_bf16):
    return sum(jnp.dot(t, m_bf16, preferred_element_type=F32) for t in _split(x_t, 3))


def _expand_heads(x_t, r_bf16):
    terms = jnp.concatenate(_split(x_t, 2), axis=0)
    return lax.dot_general(terms, jnp.concatenate([r_bf16, r_bf16], axis=0), (((0,), (0,)), ((), ())),
                           preferred_element_type=F32)


def _columns(x_t):
    pad = jnp.zeros((LANES - x_t.shape[0], x_t.shape[1]), F32)
    return jnp.concatenate([x_t, pad], axis=0).T


def _sigmoid(x):
    return 0.5 + 0.5 * jnp.tanh(0.5 * x)


def _silu(x):
    hx = 0.5 * x
    return hx + hx * jnp.tanh(hx)


def _softplus(x):
    return jnp.maximum(x, 0.0) + jnp.log1p(jnp.exp(-jnp.abs(x)))


def _rmsnorm(x, w):
    return x * lax.rsqrt(jnp.mean(x * x, axis=-1, keepdims=True) + EPS) * w


def _iota2(shape):
    return (lax.broadcasted_iota(jnp.int32, shape, 0), lax.broadcasted_iota(jnp.int32, shape, 1))


def _head_expand_matrix(src_off, n_heads, width):
    ri, ci = _iota2((SMALL_ROWS, n_heads * width))
    return jnp.where(ri - src_off == ci // width, 1.0, 0.0).astype(BF16)


def _block_ones(n, seg):
    ri, ci = _iota2((n, n))
    return jnp.where((ri // seg) == (ci // seg), 1.0, 0.0).astype(BF16)


def _shift_rows(x, tail, j):
    row = lax.broadcasted_iota(jnp.int32, (SUBLANES, x.shape[1]), 0)
    xs = pltpu.roll(x, j, axis=0)
    first = jnp.where(row < j, pltpu.roll(tail, j, axis=0), xs[0:SUBLANES])
    return jnp.concatenate([first, xs[SUBLANES:]], axis=0)


def _causal_conv(x, tail, w):
    assert CONV_K == 4
    x1 = _shift_rows(x, tail, 1)
    u = x * w[3:4, :] + x1 * w[2:3, :]
    v = x * w[1:2, :] + x1 * w[0:1, :]
    v_tail = tail * w[1:2, :] + pltpu.roll(tail, 1, axis=0) * w[0:1, :]
    return u + _shift_rows(v, v_tail, 2)


def _proj_pieces(x_ref, nw_ref, w_refs, wsmall_ref, convw_ref, convb_ref, tail_ref, proj_ref, small_ref):
    cell = {}

    def norm():
        x = x_ref[...].reshape(NB * TB, D_MODEL)
        cell["h"] = _bf(_rmsnorm(x, nw_ref[...]))

    def dot(key):
        c0, b = key

        def run():
            w_ref, w0 = (w_refs[0], c0) if c0 < W_SPLIT else (w_refs[1], c0 - W_SPLIT)
            cell[key] = jnp.dot(cell["h"][b * TB:(b + 1) * TB], w_ref[:, w0:w0 + PROJ_CHUNK],
                                preferred_element_type=F32)
        return run

    def post(key):
        c0, b = key

        def run():
            y = cell.pop(key)
            if CONV_LO <= c0 < CONV_HI:
                cols = slice(c0 - CONV_LO, c0 - CONV_LO + PROJ_CHUNK)
                conv = _causal_conv(y, tail_ref[b, :, cols], convw_ref[:, cols])
                tail_ref[b, :, cols] = y[TB - SUBLANES:TB]
                y = _silu(conv + convb_ref[:, cols])
            proj_ref[b * TB:(b + 1) * TB, c0:c0 + PROJ_CHUNK] = y
        return run

    def small():
        small_ref[...] = lax.dot_general(wsmall_ref[...], cell["h"], (((1,), (1,)), ((), ())),
                                          preferred_element_type=F32)

    keys = [(c0, b) for c0 in range(0, PROJ_COLS, PROJ_CHUNK) for b in range(NB)]
    seq = [norm, dot(keys[0])]
    for prev, key in zip(keys[:-1], keys[1:]):
        seq += [dot(key), post(prev)]
    return seq + [post(keys[-1]), small]


def _attention(q, k, v, b, kprev_ref, vprev_ref, sinks_ref, layer, t, mix_ref):
    nk = ATT_QB + WINDOW
    kext = jnp.concatenate([kprev_ref[b], k], axis=0)
    vext = jnp.concatenate([vprev_ref[b], v], axis=0)
    kprev_ref[b] = k[TB - WINDOW:TB]
    vprev_ref[b] = v[TB - WINDOW:TB]
    qi, kj = _iota2((ATT_QB, nk))
    dist = jnp.abs(qi + WINDOW - kj).astype(F32)
    qc = qi // CHUNK
    kc = kj // CHUNK - WIN_CHUNKS
    in_band = jnp.where(kc <= qc, jnp.where(kc >= qc - WIN_CHUNKS, 1, 0), 0)
    grp = ATT_HEADS // ATT_KV_HEADS
    for a in range(TB // ATT_QB):
        first_chunk = t * (TB // CHUNK) + a * (ATT_QB // CHUNK)
        valid = (in_band * jnp.where(kc + first_chunk >= 0, 1, 0)) > 0
        rows = slice(a * ATT_QB, (a + 1) * ATT_QB)
        out_rows = slice(b * TB + a * ATT_QB, b * TB + (a + 1) * ATT_QB)
        krows = slice(a * ATT_QB, a * ATT_QB + nk)
        for kh in range(ATT_KV_HEADS):
            kk = kext[krows, kh * ATT_HEAD_DIM:(kh + 1) * ATT_HEAD_DIM]
            vv = vext[krows, kh * ATT_HEAD_DIM:(kh + 1) * ATT_HEAD_DIM]
            for gi in range(grp):
                h = kh * grp + gi
                slope = 2.0 ** (-8.0 * (h + 1) / ATT_HEADS)
                sink = sinks_ref[layer * ATT_HEADS + h]
                qh = q[rows, h * ATT_HEAD_DIM:(h + 1) * ATT_HEAD_DIM]
                s = _mm_nt(qh, kk) * (ATT_HEAD_DIM ** -0.5) - slope * dist
                s = jnp.where(valid, s, MASKED)
                m = jnp.maximum(jnp.max(s, axis=-1, keepdims=True), sink)
                p = jnp.exp(s - m)
                den = jnp.sum(p, axis=-1, keepdims=True) + jnp.exp(sink - m)
                mix_ref[out_rows, h * ATT_HEAD_DIM:(h + 1) * ATT_HEAD_DIM] = _mm(p, vv) / den


def _ssd(xbc, z, sp_t, cs_t, cs_end_t, dvec, norm_w, b, state_ref, mix_ref):
    xs = xbc[:, :SSD_DIM]
    bm = xbc[:, SSD_DIM:SSD_DIM + SSD_GROUPS * SSD_STATE]
    cm = xbc[:, SSD_DIM + SSD_GROUPS * SSD_STATE:]
    n_sub = TB // SSD_CHUNK
    expand = _head_expand_matrix(SMALL_DT, SSD_HEADS, SSD_HEAD_DIM)
    cs = _columns(cs_t)
    xc = xs * _expand_heads(sp_t, expand)
    xcd = xc * _expand_heads(jnp.exp(cs_end_t - cs_t), expand)
    ecs = _expand_heads(jnp.exp(cs_t), expand)
    col = lax.broadcasted_iota(jnp.int32, (SMALL_ROWS, SMALL_ROWS), 1)
    totals = jnp.zeros((SMALL_ROWS, SMALL_ROWS), F32)
    for c in range(n_sub):
        totals = jnp.where(col == c, jnp.exp(cs_end_t[:, c * SSD_CHUNK:c * SSD_CHUNK + 1]), totals)
    etot = _expand_heads(totals, expand)
    ri, ci = _iota2((SSD_CHUNK, SSD_CHUNK))
    tri = ci <= ri
    gw = SSD_HPG * SSD_HEAD_DIM
    state = [state_ref[b, g] for g in range(SSD_GROUPS)]
    y_rows = []
    for c in range(n_sub):
        rs = slice(c * SSD_CHUNK, (c + 1) * SSD_CHUNK)
        ys = []
        for g in range(SSD_GROUPS):
            bg = bm[rs, g * SSD_STATE:(g + 1) * SSD_STATE]
            cg = cm[rs, g * SSD_STATE:(g + 1) * SSD_STATE]
            cb = _mm_nt(cg, bg)
            y_off = _mm(cg, state[g]) * ecs[rs, g * gw:(g + 1) * gw]
            state[g] = state[g] * etot[c:c + 1, g * gw:(g + 1) * gw] + _mm_tn(bg, xcd[rs, g * gw:(g + 1) * gw])
            yd = []
            for hh in range(SSD_HPG):
                h = g * SSD_HPG + hh
                diff = cs[rs, h:h + 1] - cs_t[h:h + 1, rs]
                lmat = jnp.exp(jnp.where(tri, diff, MASKED))
                yd.append(_mm(cb * lmat, xc[rs, h * SSD_HEAD_DIM:(h + 1) * SSD_HEAD_DIM]))
            ys.append(jnp.concatenate(yd, axis=1) + y_off)
        y_rows.append(jnp.concatenate(ys, axis=1))
    for g in range(SSD_GROUPS):
        state_ref[b, g] = state[g]
    y = jnp.concatenate(y_rows, axis=0) + xs * dvec
    gated = y * _silu(z)
    rows = slice(b * TB, (b + 1) * TB)
    for g in range(SSD_GROUPS):
        gg = gated[:, g * gw:(g + 1) * gw]
        gg = gg * lax.rsqrt(jnp.mean(gg * gg, axis=-1, keepdims=True) + EPS)
        mix_ref[rows, ATT_DIM + g * gw:ATT_DIM + (g + 1) * gw] = gg * norm_w[:, g * gw:(g + 1) * gw]


def _gdn(per_b, norm_w, state_ref, mix_ref, fill):
    nb = len(per_b)
    ones = _block_ones(GDN_KDIM, GDN_HEAD_K)

    def seg_sum(x2):
        return jnp.dot(_bf(x2), ones, preferred_element_type=F32)

    exp_beta = _head_expand_matrix(SMALL_BETA, GDN_HEADS, GDN_HEAD_K)
    exp_dec = _head_expand_matrix(SMALL_DECAY, GDN_HEADS, GDN_HEAD_K)
    ri, ci = _iota2((GDN_SPAN, GDN_SPAN))
    same = (ri // CHUNK) == (ci // CHUNK)
    tri = jnp.logical_and(same, ci <= ri)
    strict = jnp.logical_and(same, ci < ri)
    heads = range(GDN_HEADS)
    n_chunks = TB // CHUNK
    n_sp = TB // GDN_SPAN
    hs = [slice(h * GDN_HEAD_K, (h + 1) * GDN_HEAD_K) for h in heads]
    rs = [slice(c * CHUNK, (c + 1) * CHUNK) for c in range(n_chunks)]
    spans = [slice(a * GDN_SPAN, (a + 1) * GDN_SPAN) for a in range(n_sp)]
    bh = [(b, h) for b in range(nb) for h in heads]

    p, sol, qk, qeg, kd = {}, {}, {}, [], []
    for b, (gq, _, beta_t, gc_t, g_end_t) in enumerate(per_b):
        q = gq[:, :GDN_KDIM]
        k = gq[:, GDN_KDIM:2 * GDN_KDIM]
        v = gq[:, 2 * GDN_KDIM:]
        qn = q * lax.rsqrt(seg_sum(q * q) + EPS) * (GDN_HEAD_K ** -0.5)
        kn = k * lax.rsqrt(seg_sum(k * k) + EPS)
        beta_b = _expand_heads(beta_t, exp_beta)
        eg_b = _expand_heads(jnp.exp(gc_t), exp_dec)
        kbeta = kn * beta_b
        vbeta = v * beta_b
        kbg = kbeta * eg_b
        qeg.append(qn * eg_b)
        kd.append(kn * _expand_heads(jnp.exp(g_end_t - gc_t), exp_dec))
        gc = _columns(gc_t)
        for a, sp in enumerate(spans):
            for h in heads:
                row = SMALL_DECAY + h
                diff = gc[sp, row:row + 1] - gc_t[row:row + 1, sp]
                decay = jnp.exp(jnp.where(tri, diff, MASKED))
                u = (b, a, h)
                p[u] = jnp.where(strict, _mm_nt(kbeta[sp, hs[h]], kn[sp, hs[h]]) * (-decay), 0.0)
                qk[u] = _mm_nt(qn[sp, hs[h]], kn[sp, hs[h]]) * decay
                sol[u] = jnp.concatenate([vbeta[sp, hs[h]], kbg[sp, hs[h]]], axis=1)
    units = list(p.keys())
    fill(FILL_GDN_PREP)
    steps = CHUNK.bit_length() - 1
    for j in range(steps):
        skip = (1 << j) // BF16_ROWS * BF16_ROWS
        keep = CHUNK - skip

        def live_rows(x):
            if skip == 0:
                return x
            return jnp.concatenate([x[c * CHUNK + skip:(c + 1) * CHUNK] for c in range(GDN_SPAN // CHUNK)], axis=0)

        def all_rows(r):
            if skip == 0:
                return r
            zeros = jnp.zeros((skip, r.shape[1]), r.dtype)
            return jnp.concatenate(
                [piece for c in range(GDN_SPAN // CHUNK) for piece in (zeros, r[c * keep:(c + 1) * keep])], axis=0)

        pb = {u: _bf(p[u]) for u in units}
        sb = {u: _bf(sol[u]) for u in units}
        if j + 1 < steps:
            res = {u: all_rows(jnp.dot(live_rows(pb[u]), jnp.concatenate([pb[u], sb[u]], axis=1),
                                       preferred_element_type=F32)) for u in units}
            p = {u: res[u][:, :GDN_SPAN] for u in units}
            sol = {u: sol[u] + res[u][:, GDN_SPAN:] for u in units}
        else:
            sol = {u: sol[u] + all_rows(jnp.dot(live_rows(pb[u]), sb[u], preferred_element_type=F32))
                   for u in units}
        fill(FILL_GDN_LEVEL)
    qks_u = {u: _mm(qk[u], sol[u]) for u in units}
    solh = {(b, h): jnp.concatenate([sol[(b, a, h)] for a in range(n_sp)], axis=0) for b, h in bh}
    qks = {(b, h): jnp.concatenate([qks_u[(b, a, h)] for a in range(n_sp)], axis=0) for b, h in bh}
    qeff = {(b, h): qeg[b][:, hs[h]] - qks[(b, h)][:, GDN_HEAD_V:] for b, h in bh}
    kuw = [{(b, h): _mm_tn(kd[b][rs[c], hs[h]], solh[(b, h)][rs[c]]) for b, h in bh} for c in range(n_chunks)]
    state = {(b, h): state_ref[b, h] for b, h in bh}
    states = []
    for c in range(n_chunks):
        states.append(state)
        col = (c + 1) * CHUNK - 1
        e_end = [jnp.exp(per_b[b][3][:, col:col + 1]) for b in range(nb)]
        state = {(b, h): state[(b, h)] * e_end[b][SMALL_DECAY + h:SMALL_DECAY + h + 1, :]
                 - _mm(kuw[c][(b, h)][:, GDN_HEAD_V:], state[(b, h)]) + kuw[c][(b, h)][:, :GDN_HEAD_V]
                 for b, h in bh}
    for b, h in bh:
        state_ref[b, h] = state[(b, h)]
    outs = [{(b, h): _mm(qeff[(b, h)][rs[c]], states[c][(b, h)]) + qks[(b, h)][rs[c], :GDN_HEAD_V]
             for b, h in bh} for c in range(n_chunks)]
    for b in range(nb):
        o = jnp.concatenate([jnp.concatenate([outs[c][(b, h)] for h in heads], axis=1)
                             for c in range(n_chunks)], axis=0)
        ms = seg_sum(o * o) * (1.0 / GDN_HEAD_V)
        mix_ref[b * TB:(b + 1) * TB, ATT_DIM + SSD_DIM:] = o * lax.rsqrt(ms + EPS) * norm_w * _silu(per_b[b][1])


def _mixer_kernel(layer,
                  sinks_ref, x_ref, xnext_ref, vec_ref, col_ref, win_ref, wsmall_ref, convw_ref, wout_ref,
                  o_ref,
                  proj_ref, small_ref, pnext_ref, snext_ref, tail_ref, kprev_ref, vprev_ref, sstate_ref, gstate_ref, mix_ref,
                  winb_ref):
    t = pl.program_id(1)
    prew_ref, postw_ref, convb_ref, dvec_ref, snormw_ref, gnormw_ref = (
        vec_ref.at[:, off:off + width] for off, width in zip(VEC_OFFS, VEC_WIDTHS))
    bias_ref, alog_ref = col_ref.at[:, 0:1], col_ref.at[:, 1:2]
    proj_args = (prew_ref, (win_ref, winb_ref), wsmall_ref, convw_ref, convb_ref, tail_ref, pnext_ref, snext_ref)

    @pl.when(t == 0)
    def _():
        winb_ref[...] = win_ref[:, W_B_SRC:W_B_SRC + PROJ_COLS - W_SPLIT]
        tail_ref[...] = jnp.zeros_like(tail_ref)
        kprev_ref[...] = jnp.zeros_like(kprev_ref)
        vprev_ref[...] = jnp.zeros_like(vprev_ref)
        sstate_ref[...] = jnp.zeros_like(sstate_ref)
        gstate_ref[...] = jnp.zeros_like(gstate_ref)
        for piece in _proj_pieces(x_ref, *proj_args):
            piece()

    proj_ref[...] = pnext_ref[...]
    small_ref[...] = snext_ref[...]
    pieces = iter(_proj_pieces(xnext_ref, *proj_args))

    def fill(n):
        for _ in range(n):
            piece = next(pieces, None)
            if piece is not None:
                piece()

    def group(b, off, width):
        return proj_ref[b * TB:(b + 1) * TB, off:off + width]

    row = lax.broadcasted_iota(jnp.int32, (SMALL_ROWS, 1), 0)
    is_decay = jnp.where(row < SMALL_BETA, 1.0, jnp.where(row >= SMALL_DECAY, 1.0, 0.0))
    neg_a = -jnp.exp(alog_ref[...]) * is_decay
    ri, ci = _iota2((TB, TB))
    same = (ri // CHUNK) == (ci // CHUNK)
    same_ssd = (ri // SSD_CHUNK) == (ci // SSD_CHUNK)
    before = jnp.where(ri <= ci, 1.0, 0.0)
    m_ssd = jnp.where(same_ssd, before, 0.0).astype(BF16)
    m_whole_ssd = jnp.where(same_ssd, 1.0, 0.0).astype(BF16)
    m_chunk = jnp.where(same, before, 0.0).astype(BF16)
    m_whole_chunk = jnp.where(same, 1.0, 0.0).astype(BF16)
    gdn_in, ssd_in = [], []
    for b in range(NB):
        small_t = small_ref[:, b * TB:(b + 1) * TB]
        sp_t = _softplus(small_t + bias_ref[...])
        da_t = sp_t * neg_a
        ssd_in.append((sp_t, _cumsum_lanes(da_t, m_ssd), _cumsum_lanes(da_t, m_whole_ssd)))
        gc_t = _cumsum_lanes(da_t, m_chunk)
        g_end_t = _cumsum_lanes(da_t, m_whole_chunk)
        gdn_in.append((group(b, GQKV_OFF, GDN_CONV_DIM), group(b, GZ_OFF, GDN_DIM), _sigmoid(small_t),
                       gc_t, g_end_t))
    _gdn(gdn_in, gnormw_ref[...], gstate_ref, mix_ref, fill)
    for b in range(NB):
        _attention(group(b, Q_OFF, ATT_DIM), group(b, K_OFF, ATT_KV_DIM), group(b, V_OFF, ATT_KV_DIM),
                   b, kprev_ref, vprev_ref, sinks_ref, layer, t, mix_ref)
    for b in range(NB):
        sp_t, cs_t, cs_end_t = ssd_in[b]
        _ssd(group(b, XBC_OFF, SSD_CONV_DIM), group(b, Z_OFF, SSD_DIM), sp_t, cs_t, cs_end_t, dvec_ref[...],
             snormw_ref[...], b, sstate_ref, mix_ref)
    fill(PROJ_COLS)

    mixed = jnp.dot(_bf(mix_ref[...]), wout_ref[...], preferred_element_type=F32)
    y = _rmsnorm(mixed, postw_ref[...])
    for b in range(NB):
        o_ref[b] = x_ref[b] + y[b * TB:(b + 1) * TB]


def _mixer_call(x3d, params, layer):
    bsz, seq, _ = x3d.shape
    n_blk = seq // TB
    grid = (bsz // NB, n_blk)

    def par_spec(arr, **kw):
        shape = arr.shape[1:]
        return pl.BlockSpec((None,) + shape, lambda b, t: (layer,) + (0,) * len(shape), **kw)

    sinks, vecs, cols, w_in_bf, w_small_t, convw, wout = params
    in_specs = [pl.BlockSpec(memory_space=pltpu.SMEM),
                pl.BlockSpec((NB, TB, D_MODEL), lambda b, t: (b, t, 0)),
                pl.BlockSpec((NB, TB, D_MODEL), lambda b, t: (b, jnp.minimum(t + 1, n_blk - 1), 0)),
                par_spec(vecs),
                par_spec(cols),
                par_spec(w_in_bf, pipeline_mode=pl.Buffered(1)),
                par_spec(w_small_t),
                par_spec(convw),
                par_spec(wout, pipeline_mode=pl.Buffered(1))]
    scratch = [
        pltpu.VMEM((NB * TB, PROJ_COLS), F32),
        pltpu.VMEM((SMALL_ROWS, NB * TB), F32),
        pltpu.VMEM((NB * TB, PROJ_COLS), F32),
        pltpu.VMEM((SMALL_ROWS, NB * TB), F32),
        pltpu.VMEM((NB, SUBLANES, CONV_COLS), F32),
        pltpu.VMEM((NB, WINDOW, ATT_KV_DIM), F32),
        pltpu.VMEM((NB, WINDOW, ATT_KV_DIM), F32),
        pltpu.VMEM((NB, SSD_GROUPS, SSD_STATE, SSD_HPG * SSD_HEAD_DIM), F32),
        pltpu.VMEM((NB, GDN_HEADS, GDN_HEAD_K, GDN_HEAD_V), F32),
        pltpu.VMEM((NB * TB, D_MODEL), F32),
        pltpu.VMEM((D_MODEL, PROJ_COLS - W_SPLIT), BF16),
    ]
    return pl.pallas_call(
        functools.partial(_mixer_kernel, layer),
        grid=grid,
        in_specs=in_specs,
        out_specs=pl.BlockSpec((NB, TB, D_MODEL), lambda b, t: (b, t, 0)),
        out_shape=jax.ShapeDtypeStruct(x3d.shape, F32),
        scratch_shapes=scratch,
        compiler_params=pltpu.CompilerParams(
            dimension_semantics=("arbitrary", "arbitrary"), vmem_limit_bytes=VMEM_LIMIT),
        name="mixer",
    )(sinks, x3d, x3d, vecs, cols, w_in_bf, w_small_t, convw, wout)


def _ffn_kernel(x_ref, vec_ref, wg_ref, wu_ref, wd_ref, o_ref):
    prew_ref, postw_ref = vec_ref.at[:, 0:D_MODEL], vec_ref.at[:, D_MODEL:2 * D_MODEL]
    x = x_ref[...]
    h = _bf(_rmsnorm(x, prew_ref[...]))
    acc = jnp.zeros((TM_FFN, D_MODEL), F32)
    for c in range(FF // FF_CHUNK):
        c0, c1 = c * FF_CHUNK, (c + 1) * FF_CHUNK
        gate = jnp.dot(h, wg_ref[:, c0:c1], preferred_element_type=F32)
        up = jnp.dot(h, wu_ref[:, c0:c1], preferred_element_type=F32)
        acc = acc + jnp.dot(_bf(_silu(gate) * up), wd_ref[c0:c1, :], preferred_element_type=F32)
    o_ref[...] = x + _rmsnorm(acc, postw_ref[...])


def _ffn_call(x2d, norms, wg, wu, wd, layer):
    m = x2d.shape[0]

    def wspec(shape):
        return pl.BlockSpec((None,) + shape, lambda i: (layer, 0, 0), pipeline_mode=pl.Buffered(1))

    return pl.pallas_call(
        _ffn_kernel,
        grid=(m // TM_FFN,),
        in_specs=[
            pl.BlockSpec((TM_FFN, D_MODEL), lambda i: (i, 0)),
            pl.BlockSpec((None, 1, 2 * D_MODEL), lambda i: (layer, 0, 0)),
            wspec((D_MODEL, FF)),
            wspec((D_MODEL, FF)),
            wspec((FF, D_MODEL)),
        ],
        out_specs=pl.BlockSpec((TM_FFN, D_MODEL), lambda i: (i, 0)),
        out_shape=jax.ShapeDtypeStruct(x2d.shape, F32),
        compiler_params=pltpu.CompilerParams(
            dimension_semantics=("arbitrary",), vmem_limit_bytes=VMEM_LIMIT),
        name="ffn",
    )(x2d, norms, wg, wu, wd)


def _regroup_w_in(w_in):
    b0 = W_B_SRC + PROJ_COLS - W_SPLIT
    small = jnp.concatenate([w_in[:, :, W_SPLIT:W_B_SRC], w_in[:, :, b0:]], axis=-1)
    small = lax.optimization_barrier(small)
    return w_in.astype(BF16), jnp.swapaxes(small, 1, 2).astype(BF16)


def _pack_small(ssd_vals, gdn_vals):
    depth = ssd_vals.shape[0]
    mid = jnp.zeros((depth, GDN_HEADS), F32)
    return jnp.concatenate([ssd_vals.astype(F32), mid, gdn_vals.astype(F32)], axis=-1)[:, :, None]


def kernel(x, pre_mix_norm, post_mix_norm, pre_ffn_norm, post_ffn_norm, w_in, w_out, attn_sinks, ssd_conv_w, ssd_conv_b, ssd_dt_bias, ssd_A_log, ssd_D, ssd_norm_w, gdn_conv_w, gdn_dt_bias, gdn_A_log, gdn_norm_w, ffn_w_gate, ffn_w_up, ffn_w_down):
    bsz, seq, _ = x.shape
    depth = w_in.shape[0]
    assert seq % TB == 0 and (bsz * seq) % TM_FFN == 0 and bsz % NB == 0

    w_in_bf, w_small_t = _regroup_w_in(w_in)
    convw = jnp.concatenate([ssd_conv_w, gdn_conv_w], axis=-1).astype(F32)
    vecs = jnp.concatenate(
        [pre_mix_norm, post_mix_norm, ssd_conv_b, jnp.zeros((depth, GDN_CONV_DIM), ssd_conv_b.dtype),
         jnp.repeat(ssd_D, SSD_HEAD_DIM, axis=-1), ssd_norm_w, jnp.tile(gdn_norm_w, (1, GDN_HEADS))],
        axis=-1).astype(F32)[:, None, :]
    cols = jnp.concatenate([_pack_small(ssd_dt_bias, gdn_dt_bias), _pack_small(ssd_A_log, gdn_A_log)], axis=-1)
    params = (attn_sinks.reshape(-1).astype(F32), vecs, cols, w_in_bf, w_small_t, convw, w_out.astype(BF16))
    ffn_norms = jnp.concatenate([pre_ffn_norm, post_ffn_norm], axis=-1).astype(F32)[:, None, :]
    wg = ffn_w_gate.astype(BF16)
    wu = ffn_w_up.astype(BF16)
    wd = ffn_w_down.astype(BF16)

    for layer in range(depth):
        x = _mixer_call(x, params, layer)
        x = _ffn_call(x.reshape(bsz * seq, D_MODEL), ffn_norms, wg, wu, wd, layer).reshape(bsz, seq, D_MODEL)
    return x
```

```python
import functools

import jax
import jax.numpy as jnp
from jax import lax
from jax.experimental import pallas as pl
from jax.experimental.pallas import tpu as pltpu

F32 = jnp.float32
BF16 = jnp.bfloat16

D_MODEL = 1024
CHUNK = 64
EPS = 1e-6
CONV_K = 4
MASKED = -1e30

SSD_HEADS = 8
SSD_HEAD_DIM = 64
SSD_DIM = SSD_HEADS * SSD_HEAD_DIM
SSD_GROUPS = 2
SSD_STATE = 128
SSD_HPG = SSD_HEADS // SSD_GROUPS
SSD_CONV_DIM = SSD_DIM + 2 * SSD_GROUPS * SSD_STATE

ATT_HEADS = 4
ATT_KV_HEADS = 2
ATT_HEAD_DIM = 64
ATT_DIM = ATT_HEADS * ATT_HEAD_DIM
ATT_KV_DIM = ATT_KV_HEADS * ATT_HEAD_DIM
WINDOW = 128
WIN_CHUNKS = WINDOW // CHUNK

GDN_HEADS = 4
GDN_HEAD_K = 64
GDN_HEAD_V = 64
GDN_KDIM = GDN_HEADS * GDN_HEAD_K
GDN_DIM = GDN_HEADS * GDN_HEAD_V
GDN_CONV_DIM = 2 * GDN_KDIM + GDN_DIM

FF = 2816
LANES = 128
SUBLANES = 8
BF16_ROWS = 16

PROJ_WIDTHS = (ATT_DIM, ATT_KV_DIM, ATT_KV_DIM, SSD_DIM, SSD_CONV_DIM, GDN_CONV_DIM, GDN_DIM)
PROJ_OFFS = tuple(sum(PROJ_WIDTHS[:i]) for i in range(len(PROJ_WIDTHS)))
PROJ_COLS = sum(PROJ_WIDTHS)
Q_OFF, K_OFF, V_OFF, Z_OFF, XBC_OFF, GQKV_OFF, GZ_OFF = PROJ_OFFS
CONV_LO = XBC_OFF
CONV_HI = GZ_OFF
CONV_COLS = CONV_HI - CONV_LO
W_SPLIT = GQKV_OFF
W_B_SRC = W_SPLIT + SSD_HEADS
SMALL_ROWS = SSD_HEADS + 2 * GDN_HEADS
SMALL_DT = 0
SMALL_BETA = SSD_HEADS
SMALL_DECAY = SSD_HEADS + GDN_HEADS

VEC_WIDTHS = (D_MODEL, D_MODEL, CONV_COLS, SSD_DIM, SSD_DIM, GDN_DIM)
VEC_OFFS = tuple(sum(VEC_WIDTHS[:i]) for i in range(len(VEC_WIDTHS)))

TM_FFN = 1024
FF_CHUNK = 256
TB = 256
NB = 2
PROJ_CHUNK = 256
ATT_QB = 128
GDN_SPAN = 128
SSD_CHUNK = 128
FILL_GDN_PREP = 20
FILL_GDN_LEVEL = 5
V7X_VMEM_BYTES = 64 * 1024 * 1024
VMEM_LIMIT = V7X_VMEM_BYTES * 7 // 8


def _bf(x):
    return x.astype(BF16)


def _mm(a, b):
    return jnp.dot(_bf(a), _bf(b), preferred_element_type=F32)


def _mm_nt(a, b):
    return lax.dot_general(_bf(a), _bf(b), (((1,), (1,)), ((), ())), preferred_element_type=F32)


def _mm_tn(a, b):
    return lax.dot_general(_bf(a), _bf(b), (((0,), (0,)), ((), ())), preferred_element_type=F32)


def _split(x, terms):
    out = []
    for _ in range(terms - 1):
        t = x.astype(BF16)
        out.append(t)
        x = x - t.astype(F32)
    out.append(x.astype(BF16))
    return out


def _cumsum_lanes(x_t, m_bf16):
    return sum(jnp.dot(t, m_bf16, preferred_element_type=F32) for t in _split(x_t, 3))


def _expand_heads(x_t, r_bf16):
    terms = jnp.concatenate(_split(x_t, 2), axis=0)
    return lax.dot_general(terms, jnp.concatenate([r_bf16, r_bf16], axis=0), (((0,), (0,)), ((), ())),
                           preferred_element_type=F32)


def _columns(x_t):
    pad = jnp.zeros((LANES - x_t.shape[0], x_t.shape[1]), F32)
    return jnp.concatenate([x_t, pad], axis=0).T


def _sigmoid(x):
    return 0.5 + 0.5 * jnp.tanh(0.5 * x)


def _silu(x):
    hx = 0.5 * x
    return hx + hx * jnp.tanh(hx)


def _softplus(x):
    return jnp.maximum(x, 0.0) + jnp.log1p(jnp.exp(-jnp.abs(x)))


def _rmsnorm(x, w):
    return x * lax.rsqrt(jnp.mean(x * x, axis=-1, keepdims=True) + EPS) * w


def _iota2(shape):
    return (lax.broadcasted_iota(jnp.int32, shape, 0), lax.broadcasted_iota(jnp.int32, shape, 1))


def _head_expand_matrix(src_off, n_heads, width):
    ri, ci = _iota2((SMALL_ROWS, n_heads * width))
    return jnp.where(ri - src_off == ci // width, 1.0, 0.0).astype(BF16)


def _block_ones(n, seg):
    ri, ci = _iota2((n, n))
    return jnp.where((ri // seg) == (ci // seg), 1.0, 0.0).astype(BF16)


def _shift_rows(x, tail, j):
    row = lax.broadcasted_iota(jnp.int32, (SUBLANES, x.shape[1]), 0)
    xs = pltpu.roll(x, j, axis=0)
    first = jnp.where(row < j, pltpu.roll(tail, j, axis=0), xs[0:SUBLANES])
    return jnp.concatenate([first, xs[SUBLANES:]], axis=0)


def _causal_conv(x, tail, w):
    assert CONV_K == 4
    x1 = _shift_rows(x, tail, 1)
    u = x * w[3:4, :] + x1 * w[2:3, :]
    v = x * w[1:2, :] + x1 * w[0:1, :]
    v_tail = tail * w[1:2, :] + pltpu.roll(tail, 1, axis=0) * w[0:1, :]
    return u + _shift_rows(v, v_tail, 2)


def _proj_pieces(x_ref, nw_ref, w_refs, wsmall_ref, convw_ref, convb_ref, tail_ref, proj_ref, small_ref,
                 new_sequence=None):
    cell = {}

    def norm():
        x = x_ref[...].reshape(NB * TB, D_MODEL)
        cell["h"] = _bf(_rmsnorm(x, nw_ref[...]))

    def dot(key):
        c0, b = key

        def run():
            w_ref, w0 = (w_refs[0], c0) if c0 < W_SPLIT else (w_refs[1], c0 - W_SPLIT)
            cell[key] = jnp.dot(cell["h"][b * TB:(b + 1) * TB], w_ref[:, w0:w0 + PROJ_CHUNK],
                                preferred_element_type=F32)
        return run

    def post(key):
        c0, b = key

        def run():
            y = cell.pop(key)
            if CONV_LO <= c0 < CONV_HI:
                cols = slice(c0 - CONV_LO, c0 - CONV_LO + PROJ_CHUNK)
                tail = tail_ref[b, :, cols]
                if new_sequence is not None:
                    tail = jnp.where(new_sequence, 0.0, tail)
                conv = _causal_conv(y, tail, convw_ref[:, cols])
                tail_ref[b, :, cols] = y[TB - SUBLANES:TB]
                y = _silu(conv + convb_ref[:, cols])
            proj_ref[b * TB:(b + 1) * TB, c0:c0 + PROJ_CHUNK] = y
        return run

    def small():
        small_ref[...] = lax.dot_general(wsmall_ref[...], cell["h"], (((1,), (1,)), ((), ())),
                                          preferred_element_type=F32)

    keys = [(c0, b) for c0 in range(0, PROJ_COLS, PROJ_CHUNK) for b in range(NB)]
    seq = [norm, dot(keys[0])]
    for prev, key in zip(keys[:-1], keys[1:]):
        seq += [dot(key), post(prev)]
    return seq + [post(keys[-1]), small]


def _attention(q, k, v, b, kprev_ref, vprev_ref, sinks_ref, layer, t, mix_ref):
    nk = ATT_QB + WINDOW
    kext = jnp.concatenate([kprev_ref[b], k], axis=0)
    vext = jnp.concatenate([vprev_ref[b], v], axis=0)
    kprev_ref[b] = k[TB - WINDOW:TB]
    vprev_ref[b] = v[TB - WINDOW:TB]
    qi, kj = _iota2((ATT_QB, nk))
    dist = jnp.abs(qi + WINDOW - kj).astype(F32)
    qc = qi // CHUNK
    kc = kj // CHUNK - WIN_CHUNKS
    in_band = jnp.where(kc <= qc, jnp.where(kc >= qc - WIN_CHUNKS, 1, 0), 0)
    grp = ATT_HEADS // ATT_KV_HEADS
    for a in range(TB // ATT_QB):
        first_chunk = t * (TB // CHUNK) + a * (ATT_QB // CHUNK)
        valid = (in_band * jnp.where(kc + first_chunk >= 0, 1, 0)) > 0
        rows = slice(a * ATT_QB, (a + 1) * ATT_QB)
        out_rows = slice(b * TB + a * ATT_QB, b * TB + (a + 1) * ATT_QB)
        krows = slice(a * ATT_QB, a * ATT_QB + nk)
        for kh in range(ATT_KV_HEADS):
            kk = kext[krows, kh * ATT_HEAD_DIM:(kh + 1) * ATT_HEAD_DIM]
            vv = vext[krows, kh * ATT_HEAD_DIM:(kh + 1) * ATT_HEAD_DIM]
            for gi in range(grp):
                h = kh * grp + gi
                slope = 2.0 ** (-8.0 * (h + 1) / ATT_HEADS)
                sink = sinks_ref[layer * ATT_HEADS + h]
                qh = q[rows, h * ATT_HEAD_DIM:(h + 1) * ATT_HEAD_DIM]
                s = _mm_nt(qh, kk) * (ATT_HEAD_DIM ** -0.5) - slope * dist
                s = jnp.where(valid, s, MASKED)
                m = jnp.maximum(jnp.max(s, axis=-1, keepdims=True), sink)
                p = jnp.exp(s - m)
                den = jnp.sum(p, axis=-1, keepdims=True) + jnp.exp(sink - m)
                mix_ref[out_rows, h * ATT_HEAD_DIM:(h + 1) * ATT_HEAD_DIM] = _mm(p, vv) / den


def _ssd(xbc, z, sp_t, cs_t, cs_end_t, dvec, norm_w, b, state_ref, mix_ref):
    xs = xbc[:, :SSD_DIM]
    bm = xbc[:, SSD_DIM:SSD_DIM + SSD_GROUPS * SSD_STATE]
    cm = xbc[:, SSD_DIM + SSD_GROUPS * SSD_STATE:]
    n_sub = TB // SSD_CHUNK
    expand = _head_expand_matrix(SMALL_DT, SSD_HEADS, SSD_HEAD_DIM)
    cs = _columns(cs_t)
    xc = xs * _expand_heads(sp_t, expand)
    xcd = xc * _expand_heads(jnp.exp(cs_end_t - cs_t), expand)
    ecs = _expand_heads(jnp.exp(cs_t), expand)
    col = lax.broadcasted_iota(jnp.int32, (SMALL_ROWS, SMALL_ROWS), 1)
    totals = jnp.zeros((SMALL_ROWS, SMALL_ROWS), F32)
    for c in range(n_sub):
        totals = jnp.where(col == c, jnp.exp(cs_end_t[:, c * SSD_CHUNK:c * SSD_CHUNK + 1]), totals)
    etot = _expand_heads(totals, expand)
    ri, ci = _iota2((SSD_CHUNK, SSD_CHUNK))
    tri = ci <= ri
    gw = SSD_HPG * SSD_HEAD_DIM
    state = [state_ref[b, g] for g in range(SSD_GROUPS)]
    y_rows = []
    for c in range(n_sub):
        rs = slice(c * SSD_CHUNK, (c + 1) * SSD_CHUNK)
        ys = []
        for g in range(SSD_GROUPS):
            bg = bm[rs, g * SSD_STATE:(g + 1) * SSD_STATE]
            cg = cm[rs, g * SSD_STATE:(g + 1) * SSD_STATE]
            cb = _mm_nt(cg, bg)
            y_off = _mm(cg, state[g]) * ecs[rs, g * gw:(g + 1) * gw]
            state[g] = state[g] * etot[c:c + 1, g * gw:(g + 1) * gw] + _mm_tn(bg, xcd[rs, g * gw:(g + 1) * gw])
            yd = []
            for hh in range(SSD_HPG):
                h = g * SSD_HPG + hh
                diff = cs[rs, h:h + 1] - cs_t[h:h + 1, rs]
                lmat = jnp.exp(jnp.where(tri, diff, MASKED))
                yd.append(_mm(cb * lmat, xc[rs, h * SSD_HEAD_DIM:(h + 1) * SSD_HEAD_DIM]))
            ys.append(jnp.concatenate(yd, axis=1) + y_off)
        y_rows.append(jnp.concatenate(ys, axis=1))
    for g in range(SSD_GROUPS):
        state_ref[b, g] = state[g]
    y = jnp.concatenate(y_rows, axis=0) + xs * dvec
    gated = y * _silu(z)
    rows = slice(b * TB, (b + 1) * TB)
    for g in range(SSD_GROUPS):
        gg = gated[:, g * gw:(g + 1) * gw]
        gg = gg * lax.rsqrt(jnp.mean(gg * gg, axis=-1, keepdims=True) + EPS)
        mix_ref[rows, ATT_DIM + g * gw:ATT_DIM + (g + 1) * gw] = gg * norm_w[:, g * gw:(g + 1) * gw]


def _gdn(per_b, norm_w, state_ref, mix_ref, fill):
    nb = len(per_b)
    ones = _block_ones(GDN_KDIM, GDN_HEAD_K)

    def seg_sum(x2):
        return jnp.dot(_bf(x2), ones, preferred_element_type=F32)

    exp_beta = _head_expand_matrix(SMALL_BETA, GDN_HEADS, GDN_HEAD_K)
    exp_dec = _head_expand_matrix(SMALL_DECAY, GDN_HEADS, GDN_HEAD_K)
    ri, ci = _iota2((GDN_SPAN, GDN_SPAN))
    same = (ri // CHUNK) == (ci // CHUNK)
    tri = jnp.logical_and(same, ci <= ri)
    strict = jnp.logical_and(same, ci < ri)
    heads = range(GDN_HEADS)
    n_chunks = TB // CHUNK
    n_sp = TB // GDN_SPAN
    hs = [slice(h * GDN_HEAD_K, (h + 1) * GDN_HEAD_K) for h in heads]
    rs = [slice(c * CHUNK, (c + 1) * CHUNK) for c in range(n_chunks)]
    spans = [slice(a * GDN_SPAN, (a + 1) * GDN_SPAN) for a in range(n_sp)]
    bh = [(b, h) for b in range(nb) for h in heads]

    p, sol, qk, qeg, kd = {}, {}, {}, [], []
    for b, (gq, _, beta_t, gc_t, g_end_t) in enumerate(per_b):
        q = gq[:, :GDN_KDIM]
        k = gq[:, GDN_KDIM:2 * GDN_KDIM]
        v = gq[:, 2 * GDN_KDIM:]
        qn = q * lax.rsqrt(seg_sum(q * q) + EPS) * (GDN_HEAD_K ** -0.5)
        kn = k * lax.rsqrt(seg_sum(k * k) + EPS)
        beta_b = _expand_heads(beta_t, exp_beta)
        eg_b = _expand_heads(jnp.exp(gc_t), exp_dec)
        kbeta = kn * beta_b
        vbeta = v * beta_b
        kbg = kbeta * eg_b
        qeg.append(qn * eg_b)
        kd.append(kn * _expand_heads(jnp.exp(g_end_t - gc_t), exp_dec))
        gc = _columns(gc_t)
        for a, sp in enumerate(spans):
            for h in heads:
                row = SMALL_DECAY + h
                diff = gc[sp, row:row + 1] - gc_t[row:row + 1, sp]
                decay = jnp.exp(jnp.where(tri, diff, MASKED))
                u = (b, a, h)
                p[u] = jnp.where(strict, _mm_nt(kbeta[sp, hs[h]], kn[sp, hs[h]]) * (-decay), 0.0)
                qk[u] = _mm_nt(qn[sp, hs[h]], kn[sp, hs[h]]) * decay
                sol[u] = jnp.concatenate([vbeta[sp, hs[h]], kbg[sp, hs[h]]], axis=1)
    units = list(p.keys())
    fill(FILL_GDN_PREP)
    steps = CHUNK.bit_length() - 1
    for j in range(steps):
        skip = (1 << j) // BF16_ROWS * BF16_ROWS
        keep = CHUNK - skip

        def live_rows(x):
            if skip == 0:
                return x
            return jnp.concatenate([x[c * CHUNK + skip:(c + 1) * CHUNK] for c in range(GDN_SPAN // CHUNK)], axis=0)

        def all_rows(r):
            if skip == 0:
                return r
            zeros = jnp.zeros((skip, r.shape[1]), r.dtype)
            return jnp.concatenate(
                [piece for c in range(GDN_SPAN // CHUNK) for piece in (zeros, r[c * keep:(c + 1) * keep])], axis=0)

        pb = {u: _bf(p[u]) for u in units}
        sb = {u: _bf(sol[u]) for u in units}
        if j + 1 < steps:
            res = {u: all_rows(jnp.dot(live_rows(pb[u]), jnp.concatenate([pb[u], sb[u]], axis=1),
                                       preferred_element_type=F32)) for u in units}
            p = {u: res[u][:, :GDN_SPAN] for u in units}
            sol = {u: sol[u] + res[u][:, GDN_SPAN:] for u in units}
        else:
            sol = {u: sol[u] + all_rows(jnp.dot(live_rows(pb[u]), sb[u], preferred_element_type=F32))
                   for u in units}
        fill(FILL_GDN_LEVEL)
    qks_u = {u: _mm(qk[u], sol[u]) for u in units}
    solh = {(b, h): jnp.concatenate([sol[(b, a, h)] for a in range(n_sp)], axis=0) for b, h in bh}
    qks = {(b, h): jnp.concatenate([qks_u[(b, a, h)] for a in range(n_sp)], axis=0) for b, h in bh}
    qeff = {(b, h): qeg[b][:, hs[h]] - qks[(b, h)][:, GDN_HEAD_V:] for b, h in bh}
    kuw = [{(b, h): _mm_tn(kd[b][rs[c], hs[h]], solh[(b, h)][rs[c]]) for b, h in bh} for c in range(n_chunks)]
    state = {(b, h): state_ref[b, h] for b, h in bh}
    states = []
    for c in range(n_chunks):
        states.append(state)
        col = (c + 1) * CHUNK - 1
        e_end = [jnp.exp(per_b[b][3][:, col:col + 1]) for b in range(nb)]
        state = {(b, h): state[(b, h)] * e_end[b][SMALL_DECAY + h:SMALL_DECAY + h + 1, :]
                 - _mm(kuw[c][(b, h)][:, GDN_HEAD_V:], state[(b, h)]) + kuw[c][(b, h)][:, :GDN_HEAD_V]
                 for b, h in bh}
    for b, h in bh:
        state_ref[b, h] = state[(b, h)]
    outs = [{(b, h): _mm(qeff[(b, h)][rs[c]], states[c][(b, h)]) + qks[(b, h)][rs[c], :GDN_HEAD_V]
             for b, h in bh} for c in range(n_chunks)]
    for b in range(nb):
        o = jnp.concatenate([jnp.concatenate([outs[c][(b, h)] for h in heads], axis=1)
                             for c in range(n_chunks)], axis=0)
        ms = seg_sum(o * o) * (1.0 / GDN_HEAD_V)
        mix_ref[b * TB:(b + 1) * TB, ATT_DIM + SSD_DIM:] = o * lax.rsqrt(ms + EPS) * norm_w * _silu(per_b[b][1])


def _mixer_kernel(layer, n_blk,
                  sinks_ref, x_ref, xnext_ref, vec_ref, col_ref, win_ref, wsmall_ref, convw_ref, wout_ref,
                  o_ref,
                  proj_ref, small_ref, pnext_ref, snext_ref, tail_ref, kprev_ref, vprev_ref, sstate_ref, gstate_ref, mix_ref,
                  winb_ref):
    t = pl.program_id(1)
    prew_ref, postw_ref, convb_ref, dvec_ref, snormw_ref, gnormw_ref = (
        vec_ref.at[:, off:off + width] for off, width in zip(VEC_OFFS, VEC_WIDTHS))
    bias_ref, alog_ref = col_ref.at[:, 0:1], col_ref.at[:, 1:2]
    proj_args = (prew_ref, (win_ref, winb_ref), wsmall_ref, convw_ref, convb_ref, tail_ref, pnext_ref, snext_ref)

    @pl.when(t == 0)
    def _():
        kprev_ref[...] = jnp.zeros_like(kprev_ref)
        vprev_ref[...] = jnp.zeros_like(vprev_ref)
        sstate_ref[...] = jnp.zeros_like(sstate_ref)
        gstate_ref[...] = jnp.zeros_like(gstate_ref)

    @pl.when(jnp.logical_and(t == 0, pl.program_id(0) == 0))
    def _():
        winb_ref[...] = win_ref[:, W_B_SRC:W_B_SRC + PROJ_COLS - W_SPLIT]
        tail_ref[...] = jnp.zeros_like(tail_ref)
        for piece in _proj_pieces(x_ref, *proj_args):
            piece()

    proj_ref[...] = pnext_ref[...]
    small_ref[...] = snext_ref[...]
    pieces = iter(_proj_pieces(xnext_ref, *proj_args, new_sequence=(t == n_blk - 1)))

    def fill(n):
        for _ in range(n):
            piece = next(pieces, None)
            if piece is not None:
                piece()

    def group(b, off, width):
        return proj_ref[b * TB:(b + 1) * TB, off:off + width]

    row = lax.broadcasted_iota(jnp.int32, (SMALL_ROWS, 1), 0)
    is_decay = jnp.where(row < SMALL_BETA, 1.0, jnp.where(row >= SMALL_DECAY, 1.0, 0.0))
    neg_a = -jnp.exp(alog_ref[...]) * is_decay
    ri, ci = _iota2((TB, TB))
    same = (ri // CHUNK) == (ci // CHUNK)
    same_ssd = (ri // SSD_CHUNK) == (ci // SSD_CHUNK)
    before = jnp.where(ri <= ci, 1.0, 0.0)
    m_ssd = jnp.where(same_ssd, before, 0.0).astype(BF16)
    m_whole_ssd = jnp.where(same_ssd, 1.0, 0.0).astype(BF16)
    m_chunk = jnp.where(same, before, 0.0).astype(BF16)
    m_whole_chunk = jnp.where(same, 1.0, 0.0).astype(BF16)
    gdn_in, ssd_in = [], []
    for b in range(NB):
        small_t = small_ref[:, b * TB:(b + 1) * TB]
        sp_t = _softplus(small_t + bias_ref[...])
        da_t = sp_t * neg_a
        ssd_in.append((sp_t, _cumsum_lanes(da_t, m_ssd), _cumsum_lanes(da_t, m_whole_ssd)))
        gc_t = _cumsum_lanes(da_t, m_chunk)
        g_end_t = _cumsum_lanes(da_t, m_whole_chunk)
        gdn_in.append((group(b, GQKV_OFF, GDN_CONV_DIM), group(b, GZ_OFF, GDN_DIM), _sigmoid(small_t),
                       gc_t, g_end_t))
    _gdn(gdn_in, gnormw_ref[...], gstate_ref, mix_ref, fill)
    for b in range(NB):
        _attention(group(b, Q_OFF, ATT_DIM), group(b, K_OFF, ATT_KV_DIM), group(b, V_OFF, ATT_KV_DIM),
                   b, kprev_ref, vprev_ref, sinks_ref, layer, t, mix_ref)
    for b in range(NB):
        sp_t, cs_t, cs_end_t = ssd_in[b]
        _ssd(group(b, XBC_OFF, SSD_CONV_DIM), group(b, Z_OFF, SSD_DIM), sp_t, cs_t, cs_end_t, dvec_ref[...],
             snormw_ref[...], b, sstate_ref, mix_ref)
    fill(PROJ_COLS)

    mixed = jnp.dot(_bf(mix_ref[...]), wout_ref[...], preferred_element_type=F32)
    y = _rmsnorm(mixed, postw_ref[...])
    for b in range(NB):
        o_ref[b] = x_ref[b] + y[b * TB:(b + 1) * TB]


def _mixer_call(x3d, params, layer):
    bsz, seq, _ = x3d.shape
    n_blk = seq // TB
    grid = (bsz // NB, n_blk)

    def par_spec(arr, **kw):
        shape = arr.shape[1:]
        return pl.BlockSpec((None,) + shape, lambda b, t: (layer,) + (0,) * len(shape), **kw)

    sinks, vecs, cols, w_in_bf, w_small_t, convw, wout = params
    in_specs = [pl.BlockSpec(memory_space=pltpu.SMEM),
                pl.BlockSpec((NB, TB, D_MODEL), lambda b, t: (b, t, 0)),
                pl.BlockSpec((NB, TB, D_MODEL),
                             lambda b, t: (jnp.where(t + 1 < n_blk, b, jnp.minimum(b + 1, bsz // NB - 1)),
                                           jnp.where(t + 1 < n_blk, t + 1, 0), 0)),
                par_spec(vecs),
                par_spec(cols),
                par_spec(w_in_bf, pipeline_mode=pl.Buffered(1)),
                par_spec(w_small_t),
                par_spec(convw),
                par_spec(wout, pipeline_mode=pl.Buffered(1))]
    scratch = [
        pltpu.VMEM((NB * TB, PROJ_COLS), F32),
        pltpu.VMEM((SMALL_ROWS, NB * TB), F32),
        pltpu.VMEM((NB * TB, PROJ_COLS), F32),
        pltpu.VMEM((SMALL_ROWS, NB * TB), F32),
        pltpu.VMEM((NB, SUBLANES, CONV_COLS), F32),
        pltpu.VMEM((NB, WINDOW, ATT_KV_DIM), F32),
        pltpu.VMEM((NB, WINDOW, ATT_KV_DIM), F32),
        pltpu.VMEM((NB, SSD_GROUPS, SSD_STATE, SSD_HPG * SSD_HEAD_DIM), F32),
        pltpu.VMEM((NB, GDN_HEADS, GDN_HEAD_K, GDN_HEAD_V), F32),
        pltpu.VMEM((NB * TB, D_MODEL), F32),
        pltpu.VMEM((D_MODEL, PROJ_COLS - W_SPLIT), BF16),
    ]
    return pl.pallas_call(
        functools.partial(_mixer_kernel, layer, n_blk),
        grid=grid,
        in_specs=in_specs,
        out_specs=pl.BlockSpec((NB, TB, D_MODEL), lambda b, t: (b, t, 0)),
        out_shape=jax.ShapeDtypeStruct(x3d.shape, F32),
        scratch_shapes=scratch,
        compiler_params=pltpu.CompilerParams(
            dimension_semantics=("arbitrary", "arbitrary"), vmem_limit_bytes=VMEM_LIMIT),
        name="mixer",
    )(sinks, x3d, x3d, vecs, cols, w_in_bf, w_small_t, convw, wout)


def _ffn_kernel(x_ref, vec_ref, wg_ref, wu_ref, wd_ref, o_ref):
    prew_ref, postw_ref = vec_ref.at[:, 0:D_MODEL], vec_ref.at[:, D_MODEL:2 * D_MODEL]
    x = x_ref[...]
    h = _bf(_rmsnorm(x, prew_ref[...]))
    acc = jnp.zeros((TM_FFN, D_MODEL), F32)
    for c in range(FF // FF_CHUNK):
        c0, c1 = c * FF_CHUNK, (c + 1) * FF_CHUNK
        gate = jnp.dot(h, wg_ref[:, c0:c1], preferred_element_type=F32)
        up = jnp.dot(h, wu_ref[:, c0:c1], preferred_element_type=F32)
        acc = acc + jnp.dot(_bf(_silu(gate) * up), wd_ref[c0:c1, :], preferred_element_type=F32)
    o_ref[...] = x + _rmsnorm(acc, postw_ref[...])


def _ffn_call(x2d, norms, wg, wu, wd, layer):
    m = x2d.shape[0]

    def wspec(shape):
        return pl.BlockSpec((None,) + shape, lambda i: (layer, 0, 0), pipeline_mode=pl.Buffered(1))

    return pl.pallas_call(
        _ffn_kernel,
        grid=(m // TM_FFN,),
        in_specs=[
            pl.BlockSpec((TM_FFN, D_MODEL), lambda i: (i, 0)),
            pl.BlockSpec((None, 1, 2 * D_MODEL), lambda i: (layer, 0, 0)),
            wspec((D_MODEL, FF)),
            wspec((D_MODEL, FF)),
            wspec((FF, D_MODEL)),
        ],
        out_specs=pl.BlockSpec((TM_FFN, D_MODEL), lambda i: (i, 0)),
        out_shape=jax.ShapeDtypeStruct(x2d.shape, F32),
        compiler_params=pltpu.CompilerParams(
            dimension_semantics=("arbitrary",), vmem_limit_bytes=VMEM_LIMIT),
        name="ffn",
    )(x2d, norms, wg, wu, wd)


def _regroup_w_in(w_in):
    b0 = W_B_SRC + PROJ_COLS - W_SPLIT
    small = jnp.concatenate([w_in[:, :, W_SPLIT:W_B_SRC], w_in[:, :, b0:]], axis=-1)
    small = lax.optimization_barrier(small)
    return w_in.astype(BF16), jnp.swapaxes(small, 1, 2).astype(BF16)


def _pack_small(ssd_vals, gdn_vals):
    depth = ssd_vals.shape[0]
    mid = jnp.zeros((depth, GDN_HEADS), F32)
    return jnp.concatenate([ssd_vals.astype(F32), mid, gdn_vals.astype(F32)], axis=-1)[:, :, None]


def kernel(x, pre_mix_norm, post_mix_norm, pre_ffn_norm, post_ffn_norm, w_in, w_out, attn_sinks, ssd_conv_w, ssd_conv_b, ssd_dt_bias, ssd_A_log, ssd_D, ssd_norm_w, gdn_conv_w, gdn_dt_bias, gdn_A_log, gdn_norm_w, ffn_w_gate, ffn_w_up, ffn_w_down):
    bsz, seq, _ = x.shape
    depth = w_in.shape[0]
    assert seq % TB == 0 and (bsz * seq) % TM_FFN == 0 and bsz % NB == 0

    w_in_bf, w_small_t = _regroup_w_in(w_in)
    convw = jnp.concatenate([ssd_conv_w, gdn_conv_w], axis=-1).astype(F32)
    vecs = jnp.concatenate(
        [pre_mix_norm, post_mix_norm, ssd_conv_b, jnp.zeros((depth, GDN_CONV_DIM), ssd_conv_b.dtype),
         jnp.repeat(ssd_D, SSD_HEAD_DIM, axis=-1), ssd_norm_w, jnp.tile(gdn_norm_w, (1, GDN_HEADS))],
        axis=-1).astype(F32)[:, None, :]
    cols = jnp.concatenate([_pack_small(ssd_dt_bias, gdn_dt_bias), _pack_small(ssd_A_log, gdn_A_log)], axis=-1)
    params = (attn_sinks.reshape(-1).astype(F32), vecs, cols, w_in_bf, w_small_t, convw, w_out.astype(BF16))
    ffn_norms = jnp.concatenate([pre_ffn_norm, post_ffn_norm], axis=-1).astype(F32)[:, None, :]
    wg = ffn_w_gate.astype(BF16)
    wu = ffn_w_up.astype(BF16)
    wd = ffn_w_down.astype(BF16)

    for layer in range(depth):
        x = _mixer_call(x, params, layer)
        x = _ffn_call(x.reshape(bsz * seq, D_MODEL), ffn_norms, wg, wu, wd, layer).reshape(bsz, seq, D_MODEL)
    return x
```

```python
import functools

import jax
import jax.numpy as jnp
from jax import lax
from jax.experimental import pallas as pl
from jax.experimental.pallas import tpu as pltpu

F32 = jnp.float32
BF16 = jnp.bfloat16

D_MODEL = 1024
CHUNK = 64
EPS = 1e-6
CONV_K = 4
MASKED = -1e30

SSD_HEADS = 8
SSD_HEAD_DIM = 64
SSD_DIM = SSD_HEADS * SSD_HEAD_DIM
SSD_GROUPS = 2
SSD_STATE = 128
SSD_HPG = SSD_HEADS // SSD_GROUPS
SSD_CONV_DIM = SSD_DIM + 2 * SSD_GROUPS * SSD_STATE

ATT_HEADS = 4
ATT_KV_HEADS = 2
ATT_HEAD_DIM = 64
ATT_DIM = ATT_HEADS * ATT_HEAD_DIM
ATT_KV_DIM = ATT_KV_HEADS * ATT_HEAD_DIM
WINDOW = 128
WIN_CHUNKS = WINDOW // CHUNK

GDN_HEADS = 4
GDN_HEAD_K = 64
GDN_HEAD_V = 64
GDN_KDIM = GDN_HEADS * GDN_HEAD_K
GDN_DIM = GDN_HEADS * GDN_HEAD_V
GDN_CONV_DIM = 2 * GDN_KDIM + GDN_DIM

FF = 2816
LANES = 128
SUBLANES = 8
BF16_ROWS = 16

PROJ_WIDTHS = (ATT_DIM, ATT_KV_DIM, ATT_KV_DIM, SSD_DIM, SSD_CONV_DIM, GDN_CONV_DIM, GDN_DIM)
PROJ_OFFS = tuple(sum(PROJ_WIDTHS[:i]) for i in range(len(PROJ_WIDTHS)))
PROJ_COLS = sum(PROJ_WIDTHS)
Q_OFF, K_OFF, V_OFF, Z_OFF, XBC_OFF, GQKV_OFF, GZ_OFF = PROJ_OFFS
CONV_LO = XBC_OFF
CONV_HI = GZ_OFF
CONV_COLS = CONV_HI - CONV_LO
W_SPLIT = GQKV_OFF
W_B_SRC = W_SPLIT + SSD_HEADS
SMALL_ROWS = SSD_HEADS + 2 * GDN_HEADS
SMALL_DT = 0
SMALL_BETA = SSD_HEADS
SMALL_DECAY = SSD_HEADS + GDN_HEADS

VEC_WIDTHS = (D_MODEL, D_MODEL, CONV_COLS, SSD_DIM, SSD_DIM, GDN_DIM)
VEC_OFFS = tuple(sum(VEC_WIDTHS[:i]) for i in range(len(VEC_WIDTHS)))

TM_FFN = 1024
FF_CHUNK = 256
TB = 256
NB = 2
PROJ_CHUNK = 256
ATT_QB = 128
GDN_SPAN = 128
SSD_CHUNK = 128
FILL_GDN_PREP = 20
FILL_GDN_LEVEL = 5
V7X_VMEM_BYTES = 64 * 1024 * 1024
VMEM_LIMIT = V7X_VMEM_BYTES * 7 // 8


def _bf(x):
    return x.astype(BF16)


def _mm(a, b):
    return jnp.dot(_bf(a), _bf(b), preferred_element_type=F32)


def _mm_nt(a, b):
    return lax.dot_general(_bf(a), _bf(b), (((1,), (1,)), ((), ())), preferred_element_type=F32)


def _mm_tn(a, b):
    return lax.dot_general(_bf(a), _bf(b), (((0,), (0,)), ((), ())), preferred_element_type=F32)


def _split(x, terms):
    out = []
    for _ in range(terms - 1):
        t = x.astype(BF16)
        out.append(t)
        x = x - t.astype(F32)
    out.append(x.astype(BF16))
    return out


def _cumsum_lanes(x_t, m_bf16):
    return sum(jnp.dot(t, m_bf16, preferred_element_type=F32) for t in _split(x_t, 3))


def _expand_heads(x_t, r_bf16):
    terms = jnp.concatenate(_split(x_t, 2), axis=0)
    return lax.dot_general(terms, jnp.concatenate([r_bf16, r_bf16], axis=0), (((0,), (0,)), ((), ())),
                           preferred_element_type=F32)


def _columns(x_t):
    pad = jnp.zeros((LANES - x_t.shape[0], x_t.shape[1]), F32)
    return jnp.concatenate([x_t, pad], axis=0).T


def _sigmoid(x):
    return 0.5 + 0.5 * jnp.tanh(0.5 * x)


def _silu(x):
    hx = 0.5 * x
    return hx + hx * jnp.tanh(hx)


def _softplus(x):
    return jnp.maximum(x, 0.0) + jnp.log1p(jnp.exp(-jnp.abs(x)))


def _rmsnorm(x, w):
    return x * lax.rsqrt(jnp.mean(x * x, axis=-1, keepdims=True) + EPS) * w


def _iota2(shape):
    return (lax.broadcasted_iota(jnp.int32, shape, 0), lax.broadcasted_iota(jnp.int32, shape, 1))


def _head_expand_matrix(src_off, n_heads, width):
    ri, ci = _iota2((SMALL_ROWS, n_heads * width))
    return jnp.where(ri - src_off == ci // width, 1.0, 0.0).astype(BF16)


def _block_ones(n, seg):
    ri, ci = _iota2((n, n))
    return jnp.where((ri // seg) == (ci // seg), 1.0, 0.0).astype(BF16)


def _shift_rows(x, tail, j):
    row = lax.broadcasted_iota(jnp.int32, (SUBLANES, x.shape[1]), 0)
    xs = pltpu.roll(x, j, axis=0)
    first = jnp.where(row < j, pltpu.roll(tail, j, axis=0), xs[0:SUBLANES])
    return jnp.concatenate([first, xs[SUBLANES:]], axis=0)


def _causal_conv(x, tail, w):
    assert CONV_K == 4
    x1 = _shift_rows(x, tail, 1)
    u = x * w[3:4, :] + x1 * w[2:3, :]
    v = x * w[1:2, :] + x1 * w[0:1, :]
    v_tail = tail * w[1:2, :] + pltpu.roll(tail, 1, axis=0) * w[0:1, :]
    return u + _shift_rows(v, v_tail, 2)


def _proj_pieces(x_ref, nw_ref, w_refs, wsmall_ref, convw_ref, convb_ref, tail_ref, proj_ref, small_ref,
                 new_sequence=None):
    cell = {}

    def norm():
        x = x_ref[...].reshape(NB * TB, D_MODEL)
        cell["h"] = _bf(_rmsnorm(x, nw_ref[...]))

    def dot(key):
        c0, b = key

        def run():
            w_ref, w0 = (w_refs[0], c0) if c0 < W_SPLIT else (w_refs[1], c0 - W_SPLIT)
            cell[key] = jnp.dot(cell["h"][b * TB:(b + 1) * TB], w_ref[:, w0:w0 + PROJ_CHUNK],
                                preferred_element_type=F32)
        return run

    def post(key):
        c0, b = key

        def run():
            y = cell.pop(key)
            if CONV_LO <= c0 < CONV_HI:
                cols = slice(c0 - CONV_LO, c0 - CONV_LO + PROJ_CHUNK)
                tail = tail_ref[b, :, cols]
                if new_sequence is not None:
                    tail = jnp.where(new_sequence, 0.0, tail)
                conv = _causal_conv(y, tail, convw_ref[:, cols])
                tail_ref[b, :, cols] = y[TB - SUBLANES:TB]
                y = _silu(conv + convb_ref[:, cols])
            proj_ref[b * TB:(b + 1) * TB, c0:c0 + PROJ_CHUNK] = y
        return run

    def small():
        small_ref[...] = lax.dot_general(wsmall_ref[...], cell["h"], (((1,), (1,)), ((), ())),
                                          preferred_element_type=F32)

    keys = [(c0, b) for c0 in range(0, PROJ_COLS, PROJ_CHUNK) for b in range(NB)]
    seq = [norm, dot(keys[0])]
    for prev, key in zip(keys[:-1], keys[1:]):
        seq += [dot(key), post(prev)]
    return seq + [post(keys[-1]), small]


def _attention(q, k, v, b, kprev_ref, vprev_ref, sinks_ref, layer, t, mix_ref):
    nk = ATT_QB + WINDOW
    kext = jnp.concatenate([kprev_ref[b], k], axis=0)
    vext = jnp.concatenate([vprev_ref[b], v], axis=0)
    kprev_ref[b] = k[TB - WINDOW:TB]
    vprev_ref[b] = v[TB - WINDOW:TB]
    qi, kj = _iota2((ATT_QB, nk))
    dist = jnp.abs(qi + WINDOW - kj).astype(F32)
    qc = qi // CHUNK
    kc = kj // CHUNK - WIN_CHUNKS
    in_band = jnp.where(kc <= qc, jnp.where(kc >= qc - WIN_CHUNKS, 1, 0), 0)
    grp = ATT_HEADS // ATT_KV_HEADS
    for a in range(TB // ATT_QB):
        first_chunk = t * (TB // CHUNK) + a * (ATT_QB // CHUNK)
        valid = (in_band * jnp.where(kc + first_chunk >= 0, 1, 0)) > 0
        rows = slice(a * ATT_QB, (a + 1) * ATT_QB)
        out_rows = slice(b * TB + a * ATT_QB, b * TB + (a + 1) * ATT_QB)
        krows = slice(a * ATT_QB, a * ATT_QB + nk)
        for kh in range(ATT_KV_HEADS):
            kk = kext[krows, kh * ATT_HEAD_DIM:(kh + 1) * ATT_HEAD_DIM]
            vv = vext[krows, kh * ATT_HEAD_DIM:(kh + 1) * ATT_HEAD_DIM]
            for gi in range(grp):
                h = kh * grp + gi
                slope = 2.0 ** (-8.0 * (h + 1) / ATT_HEADS)
                sink = sinks_ref[layer * ATT_HEADS + h]
                qh = q[rows, h * ATT_HEAD_DIM:(h + 1) * ATT_HEAD_DIM]
                s = _mm_nt(qh, kk) * (ATT_HEAD_DIM ** -0.5) - slope * dist
                s = jnp.where(valid, s, MASKED)
                m = jnp.maximum(jnp.max(s, axis=-1, keepdims=True), sink)
                p = jnp.exp(s - m)
                den = jnp.sum(p, axis=-1, keepdims=True) + jnp.exp(sink - m)
                mix_ref[out_rows, h * ATT_HEAD_DIM:(h + 1) * ATT_HEAD_DIM] = _mm(p, vv) / den


def _ssd(xbc, z, sp_t, cs_t, cs_end_t, dvec, norm_w, b, state_ref, mix_ref):
    xs = xbc[:, :SSD_DIM]
    bm = xbc[:, SSD_DIM:SSD_DIM + SSD_GROUPS * SSD_STATE]
    cm = xbc[:, SSD_DIM + SSD_GROUPS * SSD_STATE:]
    n_sub = TB // SSD_CHUNK
    expand = _head_expand_matrix(SMALL_DT, SSD_HEADS, SSD_HEAD_DIM)
    cs = _columns(cs_t)
    xc = xs * _expand_heads(sp_t, expand)
    xcd = xc * _expand_heads(jnp.exp(cs_end_t - cs_t), expand)
    ecs = _expand_heads(jnp.exp(cs_t), expand)
    col = lax.broadcasted_iota(jnp.int32, (SMALL_ROWS, SMALL_ROWS), 1)
    totals = jnp.zeros((SMALL_ROWS, SMALL_ROWS), F32)
    for c in range(n_sub):
        totals = jnp.where(col == c, jnp.exp(cs_end_t[:, c * SSD_CHUNK:c * SSD_CHUNK + 1]), totals)
    etot = _expand_heads(totals, expand)
    ri, ci = _iota2((SSD_CHUNK, SSD_CHUNK))
    tri = ci <= ri
    gw = SSD_HPG * SSD_HEAD_DIM
    state = [state_ref[b, g] for g in range(SSD_GROUPS)]
    y_rows = []
    for c in range(n_sub):
        rs = slice(c * SSD_CHUNK, (c + 1) * SSD_CHUNK)
        ys = []
        for g in range(SSD_GROUPS):
            bg = bm[rs, g * SSD_STATE:(g + 1) * SSD_STATE]
            cg = cm[rs, g * SSD_STATE:(g + 1) * SSD_STATE]
            cb = _mm_nt(cg, bg)
            y_off = _mm(cg, state[g]) * ecs[rs, g * gw:(g + 1) * gw]
            state[g] = state[g] * etot[c:c + 1, g * gw:(g + 1) * gw] + _mm_tn(bg, xcd[rs, g * gw:(g + 1) * gw])
            yd = []
            for hh in range(SSD_HPG):
                h = g * SSD_HPG + hh
                diff = cs[rs, h:h + 1] - cs_t[h:h + 1, rs]
                lmat = jnp.exp(jnp.where(tri, diff, MASKED))
                yd.append(_mm(cb * lmat, xc[rs, h * SSD_HEAD_DIM:(h + 1) * SSD_HEAD_DIM]))
            ys.append(jnp.concatenate(yd, axis=1) + y_off)
        y_rows.append(jnp.concatenate(ys, axis=1))
    for g in range(SSD_GROUPS):
        state_ref[b, g] = state[g]
    y = jnp.concatenate(y_rows, axis=0) + xs * dvec
    gated = y * _silu(z)
    rows = slice(b * TB, (b + 1) * TB)
    for g in range(SSD_GROUPS):
        gg = gated[:, g * gw:(g + 1) * gw]
        gg = gg * lax.rsqrt(jnp.mean(gg * gg, axis=-1, keepdims=True) + EPS)
        mix_ref[rows, ATT_DIM + g * gw:ATT_DIM + (g + 1) * gw] = gg * norm_w[:, g * gw:(g + 1) * gw]


def _gdn(per_b, norm_w, state_ref, mix_ref, fill):
    nb = len(per_b)
    ones = _block_ones(GDN_KDIM, GDN_HEAD_K)

    def seg_sum(x2):
        return jnp.dot(_bf(x2), ones, preferred_element_type=F32)

    exp_beta = _head_expand_matrix(SMALL_BETA, GDN_HEADS, GDN_HEAD_K)
    exp_dec = _head_expand_matrix(SMALL_DECAY, GDN_HEADS, GDN_HEAD_K)
    ri, ci = _iota2((GDN_SPAN, GDN_SPAN))
    same = (ri // CHUNK) == (ci // CHUNK)
    tri = jnp.logical_and(same, ci <= ri)
    strict = jnp.logical_and(same, ci < ri)
    heads = range(GDN_HEADS)
    n_chunks = TB // CHUNK
    n_sp = TB // GDN_SPAN
    hs = [slice(h * GDN_HEAD_K, (h + 1) * GDN_HEAD_K) for h in heads]
    rs = [slice(c * CHUNK, (c + 1) * CHUNK) for c in range(n_chunks)]
    spans = [slice(a * GDN_SPAN, (a + 1) * GDN_SPAN) for a in range(n_sp)]
    bh = [(b, h) for b in range(nb) for h in heads]

    p, sol, qk, qeg, kd = {}, {}, {}, [], []
    for b, (gq, _, beta_t, gc_t, g_end_t) in enumerate(per_b):
        q = gq[:, :GDN_KDIM]
        k = gq[:, GDN_KDIM:2 * GDN_KDIM]
        v = gq[:, 2 * GDN_KDIM:]
        qn = q * lax.rsqrt(seg_sum(q * q) + EPS) * (GDN_HEAD_K ** -0.5)
        kn = k * lax.rsqrt(seg_sum(k * k) + EPS)
        beta_b = _expand_heads(beta_t, exp_beta)
        eg_b = _expand_heads(jnp.exp(gc_t), exp_dec)
        kbeta = kn * beta_b
        vbeta = v * beta_b
        kbg = kbeta * eg_b
        qeg.append(qn * eg_b)
        kd.append(kn * _expand_heads(jnp.exp(g_end_t - gc_t), exp_dec))
        gc = _columns(gc_t)
        for a, sp in enumerate(spans):
            for h in heads:
                row = SMALL_DECAY + h
                diff = gc[sp, row:row + 1] - gc_t[row:row + 1, sp]
                decay = jnp.exp(jnp.where(tri, diff, MASKED))
                u = (b, a, h)
                p[u] = jnp.where(strict, _mm_nt(kbeta[sp, hs[h]], kn[sp, hs[h]]) * (-decay), 0.0)
                qk[u] = _mm_nt(qn[sp, hs[h]], kn[sp, hs[h]]) * decay
                sol[u] = jnp.concatenate([vbeta[sp, hs[h]], kbg[sp, hs[h]]], axis=1)
    units = list(p.keys())
    fill(FILL_GDN_PREP)
    steps = CHUNK.bit_length() - 1
    for j in range(steps):
        skip = (1 << j) // BF16_ROWS * BF16_ROWS
        keep = CHUNK - skip

        def live_rows(x):
            if skip == 0:
                return x
            return jnp.concatenate([x[c * CHUNK + skip:(c + 1) * CHUNK] for c in range(GDN_SPAN // CHUNK)], axis=0)

        def all_rows(r):
            if skip == 0:
                return r
            zeros = jnp.zeros((skip, r.shape[1]), r.dtype)
            return jnp.concatenate(
                [piece for c in range(GDN_SPAN // CHUNK) for piece in (zeros, r[c * keep:(c + 1) * keep])], axis=0)

        pb = {u: _bf(p[u]) for u in units}
        sb = {u: _bf(sol[u]) for u in units}
        if j + 1 < steps:
            res = {u: all_rows(jnp.dot(live_rows(pb[u]), jnp.concatenate([pb[u], sb[u]], axis=1),
                                       preferred_element_type=F32)) for u in units}
            p = {u: res[u][:, :GDN_SPAN] for u in units}
            sol = {u: sol[u] + res[u][:, GDN_SPAN:] for u in units}
        else:
            sol = {u: sol[u] + all_rows(jnp.dot(live_rows(pb[u]), sb[u], preferred_element_type=F32))
                   for u in units}
        fill(FILL_GDN_LEVEL)
    qks_u = {u: _mm(qk[u], sol[u]) for u in units}
    solh = {(b, h): jnp.concatenate([sol[(b, a, h)] for a in range(n_sp)], axis=0) for b, h in bh}
    qks = {(b, h): jnp.concatenate([qks_u[(b, a, h)] for a in range(n_sp)], axis=0) for b, h in bh}
    qeff = {(b, h): qeg[b][:, hs[h]] - qks[(b, h)][:, GDN_HEAD_V:] for b, h in bh}
    kuw = [{(b, h): _mm_tn(kd[b][rs[c], hs[h]], solh[(b, h)][rs[c]]) for b, h in bh} for c in range(n_chunks)]
    state = {(b, h): state_ref[b, h] for b, h in bh}
    states = []
    for c in range(n_chunks):
        states.append(state)
        col = (c + 1) * CHUNK - 1
        e_end = [jnp.exp(per_b[b][3][:, col:col + 1]) for b in range(nb)]
        state = {(b, h): state[(b, h)] * e_end[b][SMALL_DECAY + h:SMALL_DECAY + h + 1, :]
                 - _mm(kuw[c][(b, h)][:, GDN_HEAD_V:], state[(b, h)]) + kuw[c][(b, h)][:, :GDN_HEAD_V]
                 for b, h in bh}
    for b, h in bh:
        state_ref[b, h] = state[(b, h)]
    outs = [{(b, h): _mm(qeff[(b, h)][rs[c]], states[c][(b, h)]) + qks[(b, h)][rs[c], :GDN_HEAD_V]
             for b, h in bh} for c in range(n_chunks)]
    for b in range(nb):
        o = jnp.concatenate([jnp.concatenate([outs[c][(b, h)] for h in heads], axis=1)
                             for c in range(n_chunks)], axis=0)
        ms = seg_sum(o * o) * (1.0 / GDN_HEAD_V)
        mix_ref[b * TB:(b + 1) * TB, ATT_DIM + SSD_DIM:] = o * lax.rsqrt(ms + EPS) * norm_w * _silu(per_b[b][1])


def _mixer_kernel(layer, n_blk,
                  sinks_ref, x_ref, xnext_ref, vec_ref, col_ref, win_ref, wsmall_ref, convw_ref, wout_ref,
                  o_ref,
                  proj_ref, small_ref, pnext_ref, snext_ref, tail_ref, kprev_ref, vprev_ref, sstate_ref, gstate_ref, mix_ref,
                  winb_ref, woutb_ref):
    t = pl.program_id(1)
    prew_ref, postw_ref, convb_ref, dvec_ref, snormw_ref, gnormw_ref = (
        vec_ref.at[:, off:off + width] for off, width in zip(VEC_OFFS, VEC_WIDTHS))
    bias_ref, alog_ref = col_ref.at[:, 0:1], col_ref.at[:, 1:2]
    proj_args = (prew_ref, (win_ref, winb_ref), wsmall_ref, convw_ref, convb_ref, tail_ref, pnext_ref, snext_ref)

    @pl.when(t == 0)
    def _():
        kprev_ref[...] = jnp.zeros_like(kprev_ref)
        vprev_ref[...] = jnp.zeros_like(vprev_ref)
        sstate_ref[...] = jnp.zeros_like(sstate_ref)
        gstate_ref[...] = jnp.zeros_like(gstate_ref)

    @pl.when(jnp.logical_and(t == 0, pl.program_id(0) == 0))
    def _():
        winb_ref[...] = win_ref[:, W_B_SRC:W_B_SRC + PROJ_COLS - W_SPLIT]
        woutb_ref[...] = _bf(wout_ref[...])
        tail_ref[...] = jnp.zeros_like(tail_ref)
        for piece in _proj_pieces(x_ref, *proj_args):
            piece()

    proj_ref[...] = pnext_ref[...]
    small_ref[...] = snext_ref[...]
    pieces = iter(_proj_pieces(xnext_ref, *proj_args, new_sequence=(t == n_blk - 1)))

    def fill(n):
        for _ in range(n):
            piece = next(pieces, None)
            if piece is not None:
                piece()

    def group(b, off, width):
        return proj_ref[b * TB:(b + 1) * TB, off:off + width]

    row = lax.broadcasted_iota(jnp.int32, (SMALL_ROWS, 1), 0)
    is_decay = jnp.where(row < SMALL_BETA, 1.0, jnp.where(row >= SMALL_DECAY, 1.0, 0.0))
    neg_a = -jnp.exp(alog_ref[...]) * is_decay
    ri, ci = _iota2((TB, TB))
    same = (ri // CHUNK) == (ci // CHUNK)
    same_ssd = (ri // SSD_CHUNK) == (ci // SSD_CHUNK)
    before = jnp.where(ri <= ci, 1.0, 0.0)
    m_ssd = jnp.where(same_ssd, before, 0.0).astype(BF16)
    m_whole_ssd = jnp.where(same_ssd, 1.0, 0.0).astype(BF16)
    m_chunk = jnp.where(same, before, 0.0).astype(BF16)
    m_whole_chunk = jnp.where(same, 1.0, 0.0).astype(BF16)
    gdn_in, ssd_in = [], []
    for b in range(NB):
        small_t = small_ref[:, b * TB:(b + 1) * TB]
        sp_t = _softplus(small_t + bias_ref[...])
        da_t = sp_t * neg_a
        ssd_in.append((sp_t, _cumsum_lanes(da_t, m_ssd), _cumsum_lanes(da_t, m_whole_ssd)))
        gc_t = _cumsum_lanes(da_t, m_chunk)
        g_end_t = _cumsum_lanes(da_t, m_whole_chunk)
        gdn_in.append((group(b, GQKV_OFF, GDN_CONV_DIM), group(b, GZ_OFF, GDN_DIM), _sigmoid(small_t),
                       gc_t, g_end_t))
    _gdn(gdn_in, gnormw_ref[...], gstate_ref, mix_ref, fill)
    for b in range(NB):
        _attention(group(b, Q_OFF, ATT_DIM), group(b, K_OFF, ATT_KV_DIM), group(b, V_OFF, ATT_KV_DIM),
                   b, kprev_ref, vprev_ref, sinks_ref, layer, t, mix_ref)
    for b in range(NB):
        sp_t, cs_t, cs_end_t = ssd_in[b]
        _ssd(group(b, XBC_OFF, SSD_CONV_DIM), group(b, Z_OFF, SSD_DIM), sp_t, cs_t, cs_end_t, dvec_ref[...],
             snormw_ref[...], b, sstate_ref, mix_ref)
    fill(PROJ_COLS)

    mixed = jnp.dot(_bf(mix_ref[...]), woutb_ref[...], preferred_element_type=F32)
    y = _rmsnorm(mixed, postw_ref[...])
    for b in range(NB):
        o_ref[b] = x_ref[b] + y[b * TB:(b + 1) * TB]


def _mixer_call(x3d, params, layer):
    bsz, seq, _ = x3d.shape
    n_blk = seq // TB
    grid = (bsz // NB, n_blk)

    def par_spec(arr, **kw):
        shape = arr.shape[1:]
        return pl.BlockSpec((None,) + shape, lambda b, t: (layer,) + (0,) * len(shape), **kw)

    sinks, vecs, cols, w_in_bf, w_small_t, convw, wout = params
    in_specs = [pl.BlockSpec(memory_space=pltpu.SMEM),
                pl.BlockSpec((NB, TB, D_MODEL), lambda b, t: (b, t, 0)),
                pl.BlockSpec((NB, TB, D_MODEL),
                             lambda b, t: (jnp.where(t + 1 < n_blk, b, jnp.minimum(b + 1, bsz // NB - 1)),
                                           jnp.where(t + 1 < n_blk, t + 1, 0), 0)),
                par_spec(vecs),
                par_spec(cols),
                par_spec(w_in_bf, pipeline_mode=pl.Buffered(1)),
                par_spec(w_small_t),
                par_spec(convw),
                par_spec(wout, pipeline_mode=pl.Buffered(1))]
    scratch = [
        pltpu.VMEM((NB * TB, PROJ_COLS), F32),
        pltpu.VMEM((SMALL_ROWS, NB * TB), F32),
        pltpu.VMEM((NB * TB, PROJ_COLS), F32),
        pltpu.VMEM((SMALL_ROWS, NB * TB), F32),
        pltpu.VMEM((NB, SUBLANES, CONV_COLS), F32),
        pltpu.VMEM((NB, WINDOW, ATT_KV_DIM), F32),
        pltpu.VMEM((NB, WINDOW, ATT_KV_DIM), F32),
        pltpu.VMEM((NB, SSD_GROUPS, SSD_STATE, SSD_HPG * SSD_HEAD_DIM), F32),
        pltpu.VMEM((NB, GDN_HEADS, GDN_HEAD_K, GDN_HEAD_V), F32),
        pltpu.VMEM((NB * TB, D_MODEL), F32),
        pltpu.VMEM((D_MODEL, PROJ_COLS - W_SPLIT), BF16),
        pltpu.VMEM((D_MODEL, D_MODEL), BF16),
    ]
    return pl.pallas_call(
        functools.partial(_mixer_kernel, layer, n_blk),
        grid=grid,
        in_specs=in_specs,
        out_specs=pl.BlockSpec((NB, TB, D_MODEL), lambda b, t: (b, t, 0)),
        out_shape=jax.ShapeDtypeStruct(x3d.shape, F32),
        scratch_shapes=scratch,
        compiler_params=pltpu.CompilerParams(
            dimension_semantics=("arbitrary", "arbitrary"), vmem_limit_bytes=VMEM_LIMIT),
        name="mixer",
    )(sinks, x3d, x3d, vecs, cols, w_in_bf, w_small_t, convw, wout)


def _ffn_kernel(x_ref, vec_ref, wg_ref, wu_ref, wd_ref, o_ref):
    prew_ref, postw_ref = vec_ref.at[:, 0:D_MODEL], vec_ref.at[:, D_MODEL:2 * D_MODEL]
    x = x_ref[...]
    h = _bf(_rmsnorm(x, prew_ref[...]))
    acc = None
    for c in range(FF // FF_CHUNK):
        c0, c1 = c * FF_CHUNK, (c + 1) * FF_CHUNK
        gate = jnp.dot(h, wg_ref[:, c0:c1], preferred_element_type=F32)
        up = jnp.dot(h, wu_ref[:, c0:c1], preferred_element_type=F32)
        down = jnp.dot(_bf(_silu(gate) * up), wd_ref[c0:c1, :], preferred_element_type=F32)
        acc = down if acc is None else acc + down
    o_ref[...] = x + _rmsnorm(acc, postw_ref[...])


def _ffn_call(x2d, norms, wg, wu, wd, layer):
    m = x2d.shape[0]

    def wspec(shape):
        return pl.BlockSpec((None,) + shape, lambda i: (layer, 0, 0), pipeline_mode=pl.Buffered(1))

    return pl.pallas_call(
        _ffn_kernel,
        grid=(m // TM_FFN,),
        in_specs=[
            pl.BlockSpec((TM_FFN, D_MODEL), lambda i: (i, 0)),
            pl.BlockSpec((None, 1, 2 * D_MODEL), lambda i: (layer, 0, 0)),
            wspec((D_MODEL, FF)),
            wspec((D_MODEL, FF)),
            wspec((FF, D_MODEL)),
        ],
        out_specs=pl.BlockSpec((TM_FFN, D_MODEL), lambda i: (i, 0)),
        out_shape=jax.ShapeDtypeStruct(x2d.shape, F32),
        compiler_params=pltpu.CompilerParams(
            dimension_semantics=("arbitrary",), vmem_limit_bytes=VMEM_LIMIT),
        name="ffn",
    )(x2d, norms, wg, wu, wd)


def _regroup_w_in(w_in):
    b0 = W_B_SRC + PROJ_COLS - W_SPLIT
    small = jnp.concatenate([w_in[:, :, W_SPLIT:W_B_SRC], w_in[:, :, b0:]], axis=-1)
    small = lax.optimization_barrier(small)
    return w_in.astype(BF16), jnp.swapaxes(small, 1, 2).astype(BF16)


def _pack_small(ssd_vals, gdn_vals):
    depth = ssd_vals.shape[0]
    mid = jnp.zeros((depth, GDN_HEADS), F32)
    return jnp.concatenate([ssd_vals.astype(F32), mid, gdn_vals.astype(F32)], axis=-1)[:, :, None]


def kernel(x, pre_mix_norm, post_mix_norm, pre_ffn_norm, post_ffn_norm, w_in, w_out, attn_sinks, ssd_conv_w, ssd_conv_b, ssd_dt_bias, ssd_A_log, ssd_D, ssd_norm_w, gdn_conv_w, gdn_dt_bias, gdn_A_log, gdn_norm_w, ffn_w_gate, ffn_w_up, ffn_w_down):
    bsz, seq, _ = x.shape
    depth = w_in.shape[0]
    assert seq % TB == 0 and (bsz * seq) % TM_FFN == 0 and bsz % NB == 0

    w_in_bf, w_small_t = _regroup_w_in(w_in)
    convw = jnp.concatenate([ssd_conv_w, gdn_conv_w], axis=-1).astype(F32)
    vecs = jnp.concatenate(
        [pre_mix_norm, post_mix_norm, ssd_conv_b, jnp.zeros((depth, GDN_CONV_DIM), ssd_conv_b.dtype),
         jnp.repeat(ssd_D, SSD_HEAD_DIM, axis=-1), ssd_norm_w, jnp.tile(gdn_norm_w, (1, GDN_HEADS))],
        axis=-1).astype(F32)[:, None, :]
    cols = jnp.concatenate([_pack_small(ssd_dt_bias, gdn_dt_bias), _pack_small(ssd_A_log, gdn_A_log)], axis=-1)
    params = (attn_sinks.reshape(-1).astype(F32), vecs, cols, w_in_bf, w_small_t, convw, w_out.astype(F32))
    ffn_norms = jnp.concatenate([pre_ffn_norm, post_ffn_norm], axis=-1).astype(F32)[:, None, :]
    wg = ffn_w_gate.astype(BF16)
    wu = ffn_w_up.astype(BF16)
    wd = ffn_w_down.astype(BF16)

    for layer in range(depth):
        x = _mixer_call(x, params, layer)
        x = _ffn_call(x.reshape(bsz * seq, D_MODEL), ffn_norms, wg, wu, wd, layer).reshape(bsz, seq, D_MODEL)
    return x
```
